```python
import jax, jax.numpy as jnp
from jax import lax
import numpy as np

D_MODEL = 1024
BATCH = 8
SEQ = 2048
DEPTH = 2
DEC_BATCH = 128
DEC_SEQ = 4
PAST_LEN = 16384
PAGE_SIZE = 128

LRU_BW = 128
D_RNN = (4 * D_MODEL // 3) // LRU_BW * LRU_BW
LRU_BLOCKS = D_RNN // LRU_BW
CONV_W = 4
LRU_C = 8.0
GLA_HEADS = 4
GLA_DK_TOTAL = D_MODEL // 2
GLA_DV_TOTAL = D_MODEL
GLA_DK = GLA_DK_TOTAL // GLA_HEADS
GLA_DV = GLA_DV_TOTAL // GLA_HEADS
GLA_RANK = 16
GLA_TAU = 16.0
GLA_CHUNK = 16
D_FF = ((8 * D_MODEL // 3 + 255) // 256) * 256
EPS = 1e-6
IN_SIZES = (D_RNN, D_RNN, GLA_DK_TOTAL, GLA_DK_TOTAL, GLA_DV_TOTAL, GLA_DV_TOTAL, GLA_RANK, D_MODEL, D_MODEL)
N_IN = sum(IN_SIZES)

kernel_name = "hybrid_rglru_gla_adaln_decode_step"


def _rmsnorm(x, g):
    xf = x.astype(jnp.float32)
    return xf * lax.rsqrt(jnp.mean(xf * xf, axis=-1, keepdims=True) + EPS) * g.astype(jnp.float32)


def _lin_combine(e1, e2):
    a1, b1 = e1
    a2, b2 = e2
    return a1 * a2, a2 * b1 + b2


def _causal_conv(u, buf, w, b):
    xa = jnp.concatenate([buf.astype(jnp.float32), u], axis=1)
    T = u.shape[1]
    out = b.astype(jnp.float32)
    for i in range(CONV_W):
        out = out + xa[:, i:i + T] * w[i].astype(jnp.float32)
    return out, xa[:, -(CONV_W - 1):]


def _rg_lru(xc, h0, wa, ba, wx, bx, lam):
    B, T, _ = xc.shape
    xb = xc.reshape(B, T, LRU_BLOCKS, LRU_BW)
    r = jax.nn.sigmoid(jnp.einsum('btnw,nwv->btnv', xb, wa.astype(jnp.float32)).reshape(B, T, D_RNN) + ba)
    i = jax.nn.sigmoid(jnp.einsum('btnw,nwv->btnv', xb, wx.astype(jnp.float32)).reshape(B, T, D_RNN) + bx)
    log_a = -LRU_C * jax.nn.softplus(-lam.astype(jnp.float32)) * r
    a = jnp.exp(log_a)
    b = jnp.sqrt(-jnp.expm1(2.0 * log_a)) * (i * xc)
    b = b.at[:, 0].add(a[:, 0] * h0.astype(jnp.float32))
    _, h = lax.associative_scan(_lin_combine, (a, b), axis=1)
    return h, h[:, -1]


def _gla_chunked(q, k, v, g, S0):
    B, T, H, K = q.shape
    V = v.shape[-1]
    C = min(GLA_CHUNK, T)
    n = -(-T // C)
    pad = n * C - T

    def blocks(z):
        z = jnp.pad(z, ((0, 0), (0, pad), (0, 0), (0, 0)))
        return z.reshape(B, n, C, H, z.shape[-1]).transpose(1, 0, 3, 2, 4)

    mask = jnp.tril(jnp.ones((C, C), dtype=bool))[:, :, None]

    def step(S, inp):
        qc, kc, vc, gc = inp
        bc = jnp.cumsum(gc, axis=2)
        decay = jnp.exp(jnp.where(mask, bc[:, :, :, None, :] - bc[:, :, None, :, :], -jnp.inf))
        att = jnp.einsum('bhik,bhjk,bhijk->bhij', qc, kc, decay)
        o = jnp.einsum('bhij,bhjv->bhiv', att, vc) + jnp.einsum('bhik,bhkv->bhiv', qc * jnp.exp(bc), S)
        b_last = bc[:, :, -1:, :]
        S = jnp.exp(b_last[:, :, 0, :])[..., None] * S + jnp.einsum('bhjk,bhjv->bhkv', kc * jnp.exp(b_last - bc), vc)
        return S, o

    S_T, o = lax.scan(step, S0.astype(jnp.float32), (blocks(q), blocks(k), blocks(v), blocks(g)))
    o = o.transpose(1, 0, 3, 2, 4).reshape(B, n * C, H, V)[:, :T]
    return o, S_T


def _layer(x, c, conv_buf, h0, S0, norm1_g, norm2_g, ada_w, ada_b, w_in, conv_w, conv_b,
           lru_wa, lru_ba, lru_wx, lru_bx, lru_lambda, gla_wa2, gla_ba, gla_norm_g,
           proj_a, proj_b, w_out, ffn_w1, ffn_w2):
    B, T, _ = x.shape
    mod = jax.nn.silu(c.astype(jnp.float32)) @ ada_w + ada_b
    sh1, sc1, gt1, sh2, sc2, gt2 = jnp.split(mod[:, None, :], 6, axis=-1)
    h = _rmsnorm(x, norm1_g) * (1.0 + sc1) + sh1
    u = h @ w_in
    idx = [int(s) for s in np.cumsum(IN_SIZES)[:-1]]
    u_x, u_g, u_q, u_k, u_v, u_r, u_lr, u_ga, u_gb = jnp.split(u, idx, axis=-1)
    xc, new_conv = _causal_conv(u_x, conv_buf, conv_w, conv_b)
    h_lru, h_T = _rg_lru(xc, h0, lru_wa, lru_ba, lru_wx, lru_bx, lru_lambda)
    y_a = h_lru * jax.nn.gelu(u_g)
    q = u_q.reshape(B, T, GLA_HEADS, GLA_DK) * (GLA_DK ** -0.5)
    k = u_k.reshape(B, T, GLA_HEADS, GLA_DK)
    v = u_v.reshape(B, T, GLA_HEADS, GLA_DV)
    g = (jax.nn.log_sigmoid(u_lr @ gla_wa2 + gla_ba) / GLA_TAU).reshape(B, T, GLA_HEADS, GLA_DK)
    o, S_T = _gla_chunked(q, k, v, g, S0)
    o = o * lax.rsqrt(jnp.mean(o * o, axis=-1, keepdims=True) + EPS) * gla_norm_g.reshape(GLA_HEADS, GLA_DV)
    y_b = o.reshape(B, T, GLA_DV_TOTAL) * jax.nn.silu(u_r)
    m = jax.nn.sigmoid(u_ga) * (y_a @ proj_a) + jax.nn.sigmoid(u_gb) * (y_b @ proj_b)
    x = (x + gt1 * (m @ w_out)).astype(x.dtype)
    h2 = _rmsnorm(x, norm2_g) * (1.0 + sc2) + sh2
    f1, f2 = jnp.split(h2 @ ffn_w1, 2, axis=-1)
    x = (x + gt2 * ((jax.nn.silu(f1) * f2) @ ffn_w2)).astype(x.dtype)
    return x, new_conv.astype(x.dtype), h_T.astype(x.dtype), S_T.astype(x.dtype)


def setup_inputs(seed: int = 0) -> dict:
    key = jax.random.key(seed)
    ks = jax.random.split(key, 32)
    nrm = jax.random.normal
    D = D_MODEL
    u = jax.random.uniform(ks[20], (DEPTH, D_RNN), minval=0.9, maxval=0.999)
    s = u ** (1.0 / LRU_C)
    lru_lambda = jnp.log(s) - jnp.log1p(-s)
    return {
        "x_prompt": nrm(ks[0], (BATCH, SEQ, D)),
        "x_sample": nrm(ks[1], (DEC_BATCH, DEC_SEQ, D)),
        "c_prompt": nrm(ks[2], (BATCH, D)),
        "c_sample": nrm(ks[3], (DEC_BATCH, D)),
        "state_conv": nrm(ks[4], (DEPTH, DEC_BATCH, CONV_W - 1, D_RNN)) * 0.5,
        "state_lru": nrm(ks[5], (DEPTH, DEC_BATCH, D_RNN)) * 0.5,
        "state_gla": nrm(ks[6], (DEPTH, DEC_BATCH, GLA_HEADS, GLA_DK, GLA_DV)) * 0.3,
        "norm1_g": 1.0 + 0.05 * nrm(ks[7], (DEPTH, D)),
        "norm2_g": 1.0 + 0.05 * nrm(ks[8], (DEPTH, D)),
        "ada_w": nrm(ks[9], (DEPTH, D, 6 * D)) * (0.3 * D ** -0.5),
        "ada_b": 0.05 * nrm(ks[10], (DEPTH, 6 * D)),
        "w_in": nrm(ks[11], (DEPTH, D, N_IN)) * D ** -0.5,
        "conv_w": nrm(ks[12], (DEPTH, CONV_W, D_RNN)) * CONV_W ** -0.5,
        "conv_b": 0.02 * nrm(ks[13], (DEPTH, D_RNN)),
        "lru_wa": nrm(ks[14], (DEPTH, LRU_BLOCKS, LRU_BW, LRU_BW)) * LRU_BW ** -0.5,
        "lru_ba": 0.02 * nrm(ks[15], (DEPTH, D_RNN)),
        "lru_wx": nrm(ks[16], (DEPTH, LRU_BLOCKS, LRU_BW, LRU_BW)) * LRU_BW ** -0.5,
        "lru_bx": 0.02 * nrm(ks[17], (DEPTH, D_RNN)),
        "lru_lambda": lru_lambda,
        "gla_wa2": nrm(ks[18], (DEPTH, GLA_RANK, GLA_DK_TOTAL)) * GLA_RANK ** -0.5,
        "gla_ba": 1.0 + 0.1 * nrm(ks[19], (DEPTH, GLA_DK_TOTAL)),
        "gla_norm_g": 1.0 + 0.05 * nrm(ks[21], (DEPTH, GLA_DV_TOTAL)),
        "proj_a": nrm(ks[22], (DEPTH, D_RNN, D)) * D_RNN ** -0.5,
        "proj_b": nrm(ks[23], (DEPTH, GLA_DV_TOTAL, D)) * GLA_DV_TOTAL ** -0.5,
        "w_out": nrm(ks[24], (DEPTH, D, D)) * D ** -0.5,
        "ffn_w1": nrm(ks[25], (DEPTH, D, 2 * D_FF)) * D ** -0.5,
        "ffn_w2": nrm(ks[26], (DEPTH, D_FF, D)) * D_FF ** -0.5,
        "final_g": 1.0 + 0.05 * nrm(ks[27], (D,)),
    }


def reference(x_prompt, x_sample, c_prompt, c_sample, state_conv, state_lru, state_gla,
              norm1_g, norm2_g, ada_w, ada_b, w_in, conv_w, conv_b, lru_wa, lru_ba, lru_wx, lru_bx,
              lru_lambda, gla_wa2, gla_ba, gla_norm_g, proj_a, proj_b, w_out, ffn_w1, ffn_w2, final_g):
    Bp = x_prompt.shape[0]
    dt = x_prompt.dtype
    xp, xs = x_prompt, x_sample
    conv_p, lru_p, gla_p, conv_s, lru_s, gla_s = [], [], [], [], [], []
    for l in range(DEPTH):
        p = (norm1_g[l], norm2_g[l], ada_w[l], ada_b[l], w_in[l], conv_w[l], conv_b[l],
             lru_wa[l], lru_ba[l], lru_wx[l], lru_bx[l], lru_lambda[l], gla_wa2[l], gla_ba[l],
             gla_norm_g[l], proj_a[l], proj_b[l], w_out[l], ffn_w1[l], ffn_w2[l])
        xp, cb, hT, ST = _layer(xp, c_prompt,
                                jnp.zeros((Bp, CONV_W - 1, D_RNN), dt),
                                jnp.zeros((Bp, D_RNN), dt),
                                jnp.zeros((Bp, GLA_HEADS, GLA_DK, GLA_DV), dt), *p)
        conv_p.append(cb); lru_p.append(hT); gla_p.append(ST)
        xs, cb, hT, ST = _layer(xs, c_sample, state_conv[l], state_lru[l], state_gla[l], *p)
        conv_s.append(cb); lru_s.append(hT); gla_s.append(ST)
    y_prompt = _rmsnorm(xp, final_g).astype(x_prompt.dtype)
    y_sample = _rmsnorm(xs, final_g).astype(x_sample.dtype)
    return (y_prompt, y_sample, jnp.stack(conv_p), jnp.stack(lru_p), jnp.stack(gla_p),
            jnp.stack(conv_s), jnp.stack(lru_s), jnp.stack(gla_s))
```

```python
import functools

import jax
import jax.numpy as jnp
from jax import lax
from jax.experimental import pallas as pl
from jax.experimental.pallas import tpu as pltpu

F32 = jnp.float32
BF16 = jnp.bfloat16

D_MODEL = 1024
D_RNN = 1280
LRU_BW = 128
LRU_BLOCKS = D_RNN // LRU_BW
CONV_W = 4
LRU_C = 8.0
GLA_HEADS = 4
GLA_DK = 128
GLA_DV = 256
GLA_DK_TOTAL = GLA_HEADS * GLA_DK
GLA_DV_TOTAL = GLA_HEADS * GLA_DV
GLA_RANK = 16
GLA_TAU = 16.0
D_FF = 2816
EPS = 1e-6
N_MOD = 6 * D_MODEL

LANES = 128
SUBLANES = 8
VMEM_LIMIT_BYTES = 60 * 1024 * 1024

_QO, _KO, _VO, _RO, _GAO, _GBO, _LRO = 0, 512, 1024, 2048, 3072, 4096, 5120
N_IN_B = _LRO + LANES
N_IN_A = 2 * D_RNN

PROMPT_TILE = 256
GLA_CHUNK = 128
FFN_TILE = 256
SAMPLE_ROW_BLOCK = 128
SAMPLE_STATE_BLOCK = 8
SAMPLE_ROW_GROUP = 16


def _cparams(sem):
    return pltpu.CompilerParams(dimension_semantics=sem, vmem_limit_bytes=VMEM_LIMIT_BYTES)


def _full_spec(shape):
    n = len(shape)
    return pl.BlockSpec(shape, lambda *_: (0,) * n)


def _softplus(y):
    return jnp.maximum(y, 0.0) + jnp.log1p(jnp.exp(-jnp.abs(y)))


def _silu(y):
    return y * jax.nn.sigmoid(y)


def _gelu_tanh(y):
    return 0.5 * y * (1.0 + jnp.tanh(0.7978845608028654 * (y + 0.044715 * (y * y * y))))


def _rms(x):
    return lax.rsqrt(jnp.mean(x * x, axis=-1, keepdims=True) + EPS)


def _norm_mod(x, g, scale, shift):
    return x * _rms(x) * g * (1.0 + scale) + shift


def _mod_part(m, i):
    return m[:, i * D_MODEL:(i + 1) * D_MODEL]


def _dot(a, b):
    return jnp.dot(a.astype(BF16), b.astype(BF16), preferred_element_type=F32)


def _dot_nt(a, b):
    return lax.dot_general(a.astype(BF16), b.astype(BF16), (((1,), (1,)), ((), ())),
                           preferred_element_type=F32)


def _lru_coeffs(xc, wg_ref, ba, bx, lam, a_ref, b_ref):
    lamc = -LRU_C * _softplus(-lam)
    for n in range(LRU_BLOCKS):
        sl = slice(n * LRU_BW, (n + 1) * LRU_BW)
        xb = xc[:, sl]
        pre = _dot(xb, wg_ref[n])
        r = jax.nn.sigmoid(pre[:, :LRU_BW] + ba[:, sl])
        i = jax.nn.sigmoid(pre[:, LRU_BW:] + bx[:, sl])
        a = jnp.exp(lamc[:, sl] * r)
        a_ref[:, sl] = a
        b_ref[:, sl] = jnp.sqrt(1.0 - a * a) * (i * xb)


def _gla_gate(u, wa2_ref, gba):
    z = _dot(u[:, _LRO:_LRO + LANES], wa2_ref[...]) + gba
    return (jnp.minimum(z, 0.0) - jnp.log1p(jnp.exp(-jnp.abs(z)))) * (1.0 / GLA_TAU)


def _group_cumsum(g, group):
    rin = lax.broadcasted_iota(jnp.int32, g.shape, 0) % group
    x = g
    k = 1
    while k < group:
        x = x + jnp.where(rin >= k, pltpu.roll(x, k, 0), 0.0)
        k *= 2
    return x


def _col_bcast(row):
    return jnp.broadcast_to(row, (LANES, LANES)).T


def _head_norm_gate(o, gng, ur):
    parts = []
    for hd in range(GLA_HEADS):
        sl = slice(hd * GLA_DV, (hd + 1) * GLA_DV)
        oh = o[:, sl]
        parts.append(oh * _rms(oh) * gng[:, sl])
    return jnp.concatenate(parts, axis=1) * _silu(ur)


def _merge_out(x, gt1, u, ya, yb, pb_ref, wo_ref):
    pbv = _dot(yb, pb_ref[...])
    mm = jax.nn.sigmoid(u[:, _GAO:_GAO + D_MODEL]) * ya + jax.nn.sigmoid(u[:, _GBO:_GBO + D_MODEL]) * pbv
    return x + gt1 * _dot(mm, wo_ref[...])


def _mod_kernel(c_ref, w_ref, b_ref, o_ref):
    o_ref[0] = _dot(_silu(c_ref[...]), w_ref[0]) + b_ref[0]


def _modulation(c_all, ada_w, ada_b):
    depth = ada_w.shape[0]
    rows = c_all.shape[0]
    nblk = N_MOD // D_MODEL
    return pl.pallas_call(
        _mod_kernel,
        grid=(depth, nblk),
        in_specs=[
            pl.BlockSpec((rows, D_MODEL), lambda l, j: (0, 0)),
            pl.BlockSpec((1, D_MODEL, D_MODEL), lambda l, j: (l, 0, j)),
            pl.BlockSpec((1, 1, D_MODEL), lambda l, j: (l, 0, j)),
        ],
        out_specs=pl.BlockSpec((1, rows, D_MODEL), lambda l, j: (l, 0, j)),
        out_shape=jax.ShapeDtypeStruct((depth, rows, N_MOD), F32),
        compiler_params=_cparams(("arbitrary", "arbitrary")),
        name="modulation",
    )(c_all, ada_w, ada_b.reshape(depth, 1, N_MOD))


def _prompt_lru_kernel(x_ref, mod_ref, g1_ref, win_ref, cw_ref, cb_ref, wg_ref, ba_ref, bx_ref, lam_ref, pa_ref,
                       ya_ref, conv_ref, lru_ref, xa_s, a_s, b_s, h_s):
    t = pl.program_id(1)
    tt = x_ref.shape[1]

    @pl.when(t == 0)
    def _():
        xa_s[0:SUBLANES, :] = jnp.zeros((SUBLANES, D_RNN), F32)
        h_s[...] = jnp.zeros_like(h_s)

    x = x_ref[0]
    m = mod_ref[0]
    h = _norm_mod(x, g1_ref[...], _mod_part(m, 1), _mod_part(m, 0))
    u = _dot(h, win_ref[...])
    xa_s[SUBLANES:SUBLANES + tt, :] = u[:, :D_RNN]
    cw = cw_ref[...]
    xc = cb_ref[...]
    for i in range(CONV_W):
        off = SUBLANES - (CONV_W - 1) + i
        xc = xc + xa_s[off:off + tt, :] * cw[i:i + 1, :]
    _lru_coeffs(xc, wg_ref, ba_ref[...], bx_ref[...], lam_ref[...], a_s, b_s)

    def step(i, hc):
        r0 = pl.multiple_of(i * SUBLANES, SUBLANES)
        a8 = a_s[pl.ds(r0, SUBLANES), :]
        b8 = b_s[pl.ds(r0, SUBLANES), :]
        rows = []
        for j in range(SUBLANES):
            hc = a8[j:j + 1, :] * hc + b8[j:j + 1, :]
            rows.append(hc)
        b_s[pl.ds(r0, SUBLANES), :] = jnp.concatenate(rows, axis=0)
        return hc

    h_last = lax.fori_loop(0, tt // SUBLANES, step, h_s[0:1, :])
    h_s[0:1, :] = h_last
    ya_ref[0] = _dot(b_s[...] * _gelu_tanh(u[:, D_RNN:]), pa_ref[...])
    xa_s[0:SUBLANES, :] = xa_s[tt:tt + SUBLANES, :]

    @pl.when(t == pl.num_programs(1) - 1)
    def _():
        conv_ref[0] = xa_s[SUBLANES - (CONV_W - 1):SUBLANES, :]
        lru_ref[0] = h_last


def _prompt_lru(x, mod_p, w):
    b, t, _ = x.shape
    tt = PROMPT_TILE
    seq_spec = pl.BlockSpec((1, tt, D_MODEL), lambda i, j: (i, j, 0))
    return pl.pallas_call(
        _prompt_lru_kernel,
        grid=(b, t // tt),
        in_specs=[
            seq_spec,
            pl.BlockSpec((1, 1, N_MOD), lambda i, j: (i, 0, 0)),
            _full_spec((1, D_MODEL)),
            _full_spec((D_MODEL, N_IN_A)),
            _full_spec((CONV_W, D_RNN)),
            _full_spec((1, D_RNN)),
            _full_spec((LRU_BLOCKS, LRU_BW, 2 * LRU_BW)),
            _full_spec((1, D_RNN)),
            _full_spec((1, D_RNN)),
            _full_spec((1, D_RNN)),
            _full_spec((D_RNN, D_MODEL)),
        ],
        out_specs=[
            seq_spec,
            pl.BlockSpec((1, CONV_W - 1, D_RNN), lambda i, j: (i, 0, 0)),
            pl.BlockSpec((1, 1, D_RNN), lambda i, j: (i, 0, 0)),
        ],
        out_shape=[
            jax.ShapeDtypeStruct((b, t, D_MODEL), F32),
            jax.ShapeDtypeStruct((b, CONV_W - 1, D_RNN), F32),
            jax.ShapeDtypeStruct((b, 1, D_RNN), F32),
        ],
        scratch_shapes=[
            pltpu.VMEM((tt + SUBLANES, D_RNN), F32),
            pltpu.VMEM((tt, D_RNN), F32),
            pltpu.VMEM((tt, D_RNN), F32),
            pltpu.VMEM((SUBLANES, D_RNN), F32),
        ],
        compiler_params=_cparams(("arbitrary", "arbitrary")),
        name="prompt_lru",
    )(x, mod_p, w["g1"], w["win_a"], w["conv_w"], w["conv_b"], w["wgate"], w["lru_ba"], w["lru_bx"],
      w["lru_lambda"], w["proj_a"])


def _prompt_gla_kernel(x_ref, mod_ref, g1_ref, win_ref, wa2_ref, gba_ref, gng_ref, pb_ref, wo_ref, ya_ref,
                       x1_ref, st_ref, s_s, o_s):
    t = pl.program_id(1)
    tt = x_ref.shape[1]
    c = GLA_CHUNK

    @pl.when(t == 0)
    def _():
        s_s[...] = jnp.zeros_like(s_s)

    x = x_ref[0]
    m = mod_ref[0]
    h = _norm_mod(x, g1_ref[...], _mod_part(m, 1), _mod_part(m, 0))
    u = _dot(h, win_ref[...])
    g = _gla_gate(u, wa2_ref, gba_ref[...])
    bc = _group_cumsum(g, c)
    qt = u[:, _QO:_QO + GLA_DK_TOTAL] * (jnp.exp(bc) * (GLA_DK ** -0.5))
    uk = u[:, _KO:_KO + GLA_DK_TOTAL]
    kt = uk * jnp.exp(-bc)
    ri = lax.broadcasted_iota(jnp.int32, (c, c), 0)
    ci = lax.broadcasted_iota(jnp.int32, (c, c), 1)
    causal = ri >= ci
    for ch in range(tt // c):
        rows = slice(ch * c, (ch + 1) * c)
        bl = bc[ch * c + c - 1:ch * c + c, :]
        kd = uk[rows, :] * jnp.exp(bl - bc[rows, :])
        dec = jnp.exp(bl)
        for hd in range(GLA_HEADS):
            ks = slice(hd * GLA_DK, (hd + 1) * GLA_DK)
            vs = slice(_VO + hd * GLA_DV, _VO + (hd + 1) * GLA_DV)
            qh = qt[rows, ks].astype(BF16)
            vh = u[rows, vs].astype(BF16)
            att = jnp.where(causal, _dot_nt(qh, kt[rows, ks]), 0.0)
            s_old = s_s[hd]
            o_s[rows, hd * GLA_DV:(hd + 1) * GLA_DV] = _dot(att, vh) + _dot(qh, s_old)
            dm = _col_bcast(dec[:, ks])
            s_s[hd] = s_old * jnp.concatenate([dm, dm], axis=1) + _dot(kd[:, ks].T, vh)
    yb = _head_norm_gate(o_s[...], gng_ref[...], u[:, _RO:_RO + D_MODEL])
    x1_ref[0] = _merge_out(x, _mod_part(m, 2), u, ya_ref[0], yb, pb_ref, wo_ref)

    @pl.when(t == pl.num_programs(1) - 1)
    def _():
        st_ref[0] = s_s[...]


def _prompt_gla(x, mod_p, ya, w):
    b, t, _ = x.shape
    tt = PROMPT_TILE
    seq_spec = pl.BlockSpec((1, tt, D_MODEL), lambda i, j: (i, j, 0))
    return pl.pallas_call(
        _prompt_gla_kernel,
        grid=(b, t // tt),
        in_specs=[
            seq_spec,
            pl.BlockSpec((1, 1, N_MOD), lambda i, j: (i, 0, 0)),
            _full_spec((1, D_MODEL)),
            _full_spec((D_MODEL, N_IN_B)),
            _full_spec((LANES, GLA_DK_TOTAL)),
            _full_spec((1, GLA_DK_TOTAL)),
            _full_spec((1, GLA_DV_TOTAL)),
            _full_spec((GLA_DV_TOTAL, D_MODEL)),
            _full_spec((D_MODEL, D_MODEL)),
            seq_spec,
        ],
        out_specs=[
            seq_spec,
            pl.BlockSpec((1, GLA_HEADS, GLA_DK, GLA_DV), lambda i, j: (i, 0, 0, 0)),
        ],
        out_shape=[
            jax.ShapeDtypeStruct((b, t, D_MODEL), F32),
            jax.ShapeDtypeStruct((b, GLA_HEADS, GLA_DK, GLA_DV), F32),
        ],
        scratch_shapes=[
            pltpu.VMEM((GLA_HEADS, GLA_DK, GLA_DV), F32),
            pltpu.VMEM((tt, GLA_DV_TOTAL), F32),
        ],
        compiler_params=_cparams(("arbitrary", "arbitrary")),
        name="prompt_gla",
    )(x, mod_p, w["g1"], w["win_b"], w["wa2"], w["gla_ba"], w["gla_norm_g"], w["proj_b"], w["w_out"], ya)


def _ffn_kernel(x_ref, mod_ref, g2_ref, w1_ref, w2_ref, fg_ref, o_ref, *, final_norm):
    x = x_ref[...]
    m = mod_ref[0]
    h2 = _norm_mod(x, g2_ref[...], _mod_part(m, 4), _mod_part(m, 3))
    f = _dot(h2, w1_ref[...])
    x2 = x + _mod_part(m, 5) * _dot(_silu(f[:, :D_FF]) * f[:, D_FF:], w2_ref[...])
    if final_norm:
        x2 = x2 * _rms(x2) * fg_ref[...]
    o_ref[...] = x2


def _ffn(x2d, mod3, rows_per_mod, tile, w, final_g, final_norm):
    mrows = x2d.shape[0]
    r = mod3.shape[1]
    per = rows_per_mod // tile
    row_spec = pl.BlockSpec((tile, D_MODEL), lambda i: (i, 0))
    return pl.pallas_call(
        functools.partial(_ffn_kernel, final_norm=final_norm),
        grid=(mrows // tile,),
        in_specs=[
            row_spec,
            pl.BlockSpec((1, r, N_MOD), lambda i: (i // per, 0, 0)),
            _full_spec((1, D_MODEL)),
            _full_spec((D_MODEL, 2 * D_FF)),
            _full_spec((D_FF, D_MODEL)),
            _full_spec((1, D_MODEL)),
        ],
        out_specs=row_spec,
        out_shape=jax.ShapeDtypeStruct((mrows, D_MODEL), F32),
        compiler_params=_cparams(("arbitrary",)),
        name="ffn_final" if final_norm else "ffn",
    )(x2d, mod3, w["g2"], w["ffn_w1"], w["ffn_w2"], final_g)


def _sample_lru_kernel(x_ref, mod_ref, g1_ref, win_ref, cw_ref, cb_ref, wg_ref, ba_ref, bx_ref, lam_ref, pa_ref,
                       conv0_ref, h0_ref, ya_ref, conv_ref, lru_ref, a_s, b_s):
    ns = h0_ref.shape[0]
    nt = x_ref.shape[0] // ns
    m = mod_ref[...]
    tile = lambda v: jnp.concatenate([v] * nt, axis=0)
    h = _norm_mod(x_ref[...], g1_ref[...], tile(_mod_part(m, 1)), tile(_mod_part(m, 0)))
    u = _dot(h, win_ref[...])
    xa = [conv0_ref[i] for i in range(CONV_W - 1)] + [u[i * ns:(i + 1) * ns, :D_RNN] for i in range(nt)]
    cw = cw_ref[...]
    xcs = []
    for ti in range(nt):
        acc = cb_ref[...]
        for i in range(CONV_W):
            acc = acc + xa[ti + i] * cw[i:i + 1, :]
        xcs.append(acc)
    _lru_coeffs(jnp.concatenate(xcs, axis=0), wg_ref, ba_ref[...], bx_ref[...], lam_ref[...], a_s, b_s)
    hc = h0_ref[...]
    for ti in range(nt):
        rows = slice(ti * ns, (ti + 1) * ns)
        hc = a_s[rows, :] * hc + b_s[rows, :]
        b_s[rows, :] = hc
    ya_ref[...] = _dot(b_s[...] * _gelu_tanh(u[:, D_RNN:]), pa_ref[...])
    for i in range(CONV_W - 1):
        conv_ref[i] = xa[nt + i]
    lru_ref[...] = hc


def _sample_lru(x_tm, mod_s, conv0_tm, h0, w):
    rows = x_tm.shape[0]
    ns = h0.shape[0]
    return pl.pallas_call(
        _sample_lru_kernel,
        grid=(1,),
        in_specs=[
            _full_spec((rows, D_MODEL)),
            _full_spec((ns, N_MOD)),
            _full_spec((1, D_MODEL)),
            _full_spec((D_MODEL, N_IN_A)),
            _full_spec((CONV_W, D_RNN)),
            _full_spec((1, D_RNN)),
            _full_spec((LRU_BLOCKS, LRU_BW, 2 * LRU_BW)),
            _full_spec((1, D_RNN)),
            _full_spec((1, D_RNN)),
            _full_spec((1, D_RNN)),
            _full_spec((D_RNN, D_MODEL)),
            _full_spec((CONV_W - 1, ns, D_RNN)),
            _full_spec((ns, D_RNN)),
        ],
        out_specs=[
            _full_spec((rows, D_MODEL)),
            _full_spec((CONV_W - 1, ns, D_RNN)),
            _full_spec((ns, D_RNN)),
        ],
        out_shape=[
            jax.ShapeDtypeStruct((rows, D_MODEL), F32),
            jax.ShapeDtypeStruct((CONV_W - 1, ns, D_RNN), F32),
            jax.ShapeDtypeStruct((ns, D_RNN), F32),
        ],
        scratch_shapes=[pltpu.VMEM((rows, D_RNN), F32), pltpu.VMEM((rows, D_RNN), F32)],
        compiler_params=_cparams(("arbitrary",)),
        name="sample_lru",
    )(x_tm, mod_s, w["g1"], w["win_a"], w["conv_w"], w["conv_b"], w["wgate"], w["lru_ba"], w["lru_bx"],
      w["lru_lambda"], w["proj_a"], conv0_tm, h0)


def _sample_inproj_kernel(x_ref, mod_ref, g1_ref, win_ref, u_ref):
    nt = x_ref.shape[0] // mod_ref.shape[0]
    m = mod_ref[...]
    tile = lambda v: jnp.concatenate([v] * nt, axis=0)
    h = _norm_mod(x_ref[...], g1_ref[...], tile(_mod_part(m, 1)), tile(_mod_part(m, 0)))
    u_ref[...] = _dot(h, win_ref[...])


def _sample_inproj(x_tm, mod_s, w):
    rows = x_tm.shape[0]
    return pl.pallas_call(
        _sample_inproj_kernel,
        grid=(1,),
        in_specs=[
            _full_spec((rows, D_MODEL)),
            _full_spec(mod_s.shape),
            _full_spec((1, D_MODEL)),
            _full_spec((D_MODEL, N_IN_B)),
        ],
        out_specs=_full_spec((rows, N_IN_B)),
        out_shape=jax.ShapeDtypeStruct((rows, N_IN_B), F32),
        compiler_params=_cparams(("arbitrary",)),
        name="sample_inproj",
    )(x_tm, mod_s, w["g1"], w["win_b"])


def _sample_gla_kernel(u_ref, wa2_ref, gba_ref, gng_ref, s0_ref, yb_ref, st_ref, o_s, q_s, v_s, kdt_s, bl_s, *, nt):
    rows = u_ref.shape[0]
    steps_per_rows = rows // (SAMPLE_STATE_BLOCK * nt)
    sub = pl.program_id(0) % steps_per_rows
    rg = SAMPLE_ROW_GROUP
    seq_per_group = rg // nt

    @pl.when(sub == 0)
    def _():
        u = u_ref[...]
        g = _gla_gate(u, wa2_ref, gba_ref[...])
        bc = _group_cumsum(g, nt)
        rin = lax.broadcasted_iota(jnp.int32, g.shape, 0) % nt
        sfx = jnp.zeros_like(g)
        for k in range(1, nt):
            sfx = sfx + jnp.where(rin + k < nt, pltpu.roll(g, rows - k, 0), 0.0)
        qt = u[:, _QO:_QO + GLA_DK_TOTAL] * (jnp.exp(bc) * (GLA_DK ** -0.5))
        uk = u[:, _KO:_KO + GLA_DK_TOTAL]
        kt = uk * jnp.exp(-bc)
        kd = uk * jnp.exp(sfx)
        q_s[...] = qt.astype(BF16)
        v_s[...] = u[:, _VO:_VO + GLA_DV_TOTAL].astype(BF16)
        bl_s[...] = bc + sfx
        ri = lax.broadcasted_iota(jnp.int32, (rows, rows), 0)
        ci = lax.broadcasted_iota(jnp.int32, (rows, rows), 1)
        same_seq_causal = (ri // nt == ci // nt) & (ri >= ci)
        for hd in range(GLA_HEADS):
            ks = slice(hd * GLA_DK, (hd + 1) * GLA_DK)
            att = jnp.where(same_seq_causal, _dot_nt(qt[:, ks], kt[:, ks]), 0.0)
            o_s[:, hd * GLA_DV:(hd + 1) * GLA_DV] = _dot(att, v_s[:, hd * GLA_DV:(hd + 1) * GLA_DV])
            kdt_s[hd] = kd[:, ks].T

    lane_seq = lax.broadcasted_iota(jnp.int32, (GLA_DK, rows), 1) // nt
    row_seq = lax.broadcasted_iota(jnp.int32, (rg, GLA_DV), 0) // nt

    def group(gi, carry):
        r0 = pl.multiple_of((sub * (SAMPLE_STATE_BLOCK // seq_per_group) + gi) * rg, rg)
        dec_g = jnp.exp(bl_s[pl.ds(r0, rg), :])
        for hd in range(GLA_HEADS):
            ks = slice(hd * GLA_DK, (hd + 1) * GLA_DK)
            vsl = slice(hd * GLA_DV, (hd + 1) * GLA_DV)
            qg = q_s[pl.ds(r0, rg), ks]
            vh = v_s[:, vsl]
            inter = jnp.zeros((rg, GLA_DV), F32)
            for j in range(seq_per_group):
                s_loc = gi * seq_per_group + j
                s_row = sub * SAMPLE_STATE_BLOCK + s_loc
                s_old = s0_ref[s_loc, hd]
                inter = jnp.where(row_seq == j, _dot(qg, s_old), inter)
                kdt = jnp.where(lane_seq == s_row, kdt_s[hd], 0.0)
                dm = _col_bcast(dec_g[j * nt:j * nt + 1, ks])
                st_ref[s_loc, hd] = s_old * jnp.concatenate([dm, dm], axis=1) + _dot(kdt, vh)
            o_s[pl.ds(r0, rg), vsl] = o_s[pl.ds(r0, rg), vsl] + inter
        return carry

    lax.fori_loop(0, SAMPLE_STATE_BLOCK // seq_per_group, group, 0)

    @pl.when(sub == steps_per_rows - 1)
    def _():
        yb_ref[...] = _head_norm_gate(o_s[...], gng_ref[...], u_ref[:, _RO:_RO + D_MODEL])


def _sample_gla(u_sm, s0, w, nt):
    rows = u_sm.shape[0]
    br = SAMPLE_ROW_BLOCK
    sb = SAMPLE_STATE_BLOCK
    per = br // (sb * nt)
    return pl.pallas_call(
        functools.partial(_sample_gla_kernel, nt=nt),
        grid=(rows // (sb * nt),),
        in_specs=[
            pl.BlockSpec((br, N_IN_B), lambda i: (i // per, 0)),
            _full_spec((LANES, GLA_DK_TOTAL)),
            _full_spec((1, GLA_DK_TOTAL)),
            _full_spec((1, GLA_DV_TOTAL)),
            pl.BlockSpec((sb, GLA_HEADS, GLA_DK, GLA_DV), lambda i: (i, 0, 0, 0)),
        ],
        out_specs=[
            pl.BlockSpec((br, D_MODEL), lambda i: (i // per, 0)),
            pl.BlockSpec((sb, GLA_HEADS, GLA_DK, GLA_DV), lambda i: (i, 0, 0, 0)),
        ],
        out_shape=[
            jax.ShapeDtypeStruct((rows, D_MODEL), F32),
            jax.ShapeDtypeStruct(s0.shape, F32),
        ],
        scratch_shapes=[
            pltpu.VMEM((br, GLA_DV_TOTAL), F32),
            pltpu.VMEM((br, GLA_DK_TOTAL), BF16),
            pltpu.VMEM((br, GLA_DV_TOTAL), BF16),
            pltpu.VMEM((GLA_HEADS, GLA_DK, br), F32),
            pltpu.VMEM((br, GLA_DK_TOTAL), F32),
        ],
        compiler_params=_cparams(("arbitrary",)),
        name="sample_gla",
    )(u_sm, w["wa2"], w["gla_ba"], w["gla_norm_g"], s0)


def _sample_merge_kernel(x_ref, mod_ref, u_ref, ya_ref, yb_ref, pb_ref, wo_ref, x1_ref):
    nt = x_ref.shape[0] // mod_ref.shape[0]
    gt1 = jnp.concatenate([_mod_part(mod_ref[...], 2)] * nt, axis=0)
    x1_ref[...] = _merge_out(x_ref[...], gt1, u_ref[...], ya_ref[...], yb_ref[...], pb_ref, wo_ref)


def _sample_merge(x_tm, mod_s, u_tm, ya, yb_tm, w):
    rows = x_tm.shape[0]
    return pl.pallas_call(
        _sample_merge_kernel,
        grid=(1,),
        in_specs=[
            _full_spec((rows, D_MODEL)),
            _full_spec(mod_s.shape),
            _full_spec((rows, N_IN_B)),
            _full_spec((rows, D_MODEL)),
            _full_spec((rows, D_MODEL)),
            _full_spec((GLA_DV_TOTAL, D_MODEL)),
            _full_spec((D_MODEL, D_MODEL)),
        ],
        out_specs=_full_spec((rows, D_MODEL)),
        out_shape=jax.ShapeDtypeStruct((rows, D_MODEL), F32),
        compiler_params=_cparams(("arbitrary",)),
        name="sample_merge",
    )(x_tm, mod_s, u_tm, ya, yb_tm, w["proj_b"], w["w_out"])


def _layer_weights(l, norm1_g, norm2_g, w_in, conv_w, conv_b, lru_wa, lru_ba, lru_wx, lru_bx, lru_lambda,
                   gla_wa2, gla_ba, gla_norm_g, proj_a, proj_b, w_out, ffn_w1, ffn_w2):
    wi = w_in[l]
    o_q = 2 * D_RNN
    o_lr = o_q + 2 * GLA_DK_TOTAL + 2 * GLA_DV_TOTAL
    o_ga = o_lr + GLA_RANK
    win_b = jnp.concatenate(
        [wi[:, o_q:o_lr], wi[:, o_ga:], wi[:, o_lr:o_ga], jnp.zeros((D_MODEL, LANES - GLA_RANK), wi.dtype)], axis=1)
    row = lambda v: v[l].reshape(1, -1)
    return {
        "g1": row(norm1_g), "g2": row(norm2_g),
        "win_a": wi[:, :o_q].astype(BF16), "win_b": win_b.astype(BF16),
        "conv_w": conv_w[l], "conv_b": row(conv_b),
        "wgate": jnp.concatenate([lru_wa[l], lru_wx[l]], axis=-1).astype(BF16),
        "lru_ba": row(lru_ba), "lru_bx": row(lru_bx), "lru_lambda": row(lru_lambda),
        "wa2": jnp.pad(gla_wa2[l], ((0, LANES - GLA_RANK), (0, 0))).astype(BF16),
        "gla_ba": row(gla_ba), "gla_norm_g": row(gla_norm_g),
        "proj_a": proj_a[l].astype(BF16), "proj_b": proj_b[l].astype(BF16), "w_out": w_out[l].astype(BF16),
        "ffn_w1": ffn_w1[l].astype(BF16), "ffn_w2": ffn_w2[l].astype(BF16),
    }


def kernel(x_prompt, x_sample, c_prompt, c_sample, state_conv, state_lru, state_gla, norm1_g, norm2_g, ada_w, ada_b,
           w_in, conv_w, conv_b, lru_wa, lru_ba, lru_wx, lru_bx, lru_lambda, gla_wa2, gla_ba, gla_norm_g, proj_a,
           proj_b, w_out, ffn_w1, ffn_w2, final_g):
    bp, tp, _ = x_prompt.shape
    bs, ts, _ = x_sample.shape
    depth = w_in.shape[0]
    mod = _modulation(jnp.concatenate([c_prompt, c_sample], axis=0), ada_w, ada_b)
    fg = final_g.reshape(1, D_MODEL)
    xp = x_prompt
    xs = x_sample.transpose(1, 0, 2).reshape(ts * bs, D_MODEL)
    conv_p, lru_p, gla_p, conv_s, lru_s, gla_s = [], [], [], [], [], []
    for l in range(depth):
        w = _layer_weights(l, norm1_g, norm2_g, w_in, conv_w, conv_b, lru_wa, lru_ba, lru_wx, lru_bx, lru_lambda,
                           gla_wa2, gla_ba, gla_norm_g, proj_a, proj_b, w_out, ffn_w1, ffn_w2)
        last = l == depth - 1
        mod_p = mod[l, :bp].reshape(bp, 1, N_MOD)
        mod_s = mod[l, bp:]
        ya, cb, ht = _prompt_lru(xp, mod_p, w)
        x1, st = _prompt_gla(xp, mod_p, ya, w)
        xp = _ffn(x1.reshape(bp * tp, D_MODEL), mod_p, tp, FFN_TILE, w, fg, last).reshape(bp, tp, D_MODEL)
        conv_p.append(cb)
        lru_p.append(ht.reshape(bp, D_RNN))
        gla_p.append(st)
        ya_s, cb_s, ht_s = _sample_lru(xs, mod_s, state_conv[l].transpose(1, 0, 2), state_lru[l], w)
        u_tm = _sample_inproj(xs, mod_s, w)
        u_sm = u_tm.reshape(ts, bs, N_IN_B).transpose(1, 0, 2).reshape(bs * ts, N_IN_B)
        yb_sm, st_s = _sample_gla(u_sm, state_gla[l], w, ts)
        yb_tm = yb_sm.reshape(bs, ts, D_MODEL).transpose(1, 0, 2).reshape(ts * bs, D_MODEL)
        x1_s = _sample_merge(xs, mod_s, u_tm, ya_s, yb_tm, w)
        xs = _ffn(x1_s, mod_s.reshape(1, bs, N_MOD), bs * ts, bs, w, fg, last)
        conv_s.append(cb_s.transpose(1, 0, 2))
        lru_s.append(ht_s)
        gla_s.append(st_s)
    y_sample = xs.reshape(ts, bs, D_MODEL).transpose(1, 0, 2)
    return (xp, y_sample, jnp.stack(conv_p), jnp.stack(lru_p), jnp.stack(gla_p),
            jnp.stack(conv_s), jnp.stack(lru_s), jnp.stack(gla_s))
```

```python
import functools

import jax
import jax.numpy as jnp
from jax import lax
from jax.experimental import pallas as pl
from jax.experimental.pallas import tpu as pltpu

F32 = jnp.float32
BF16 = jnp.bfloat16

D_MODEL = 1024
D_RNN = 1280
LRU_BW = 128
LRU_BLOCKS = D_RNN // LRU_BW
CONV_W = 4
LRU_C = 8.0
GLA_HEADS = 4
GLA_DK = 128
GLA_DV = 256
GLA_DK_TOTAL = GLA_HEADS * GLA_DK
GLA_DV_TOTAL = GLA_HEADS * GLA_DV
GLA_RANK = 16
GLA_TAU = 16.0
D_FF = 2816
EPS = 1e-6
N_MOD = 6 * D_MODEL

LANES = 128
SUBLANES = 8
VMEM_LIMIT_BYTES = 60 * 1024 * 1024

_QO, _KO, _VO, _RO, _GAO, _GBO, _LRO = 0, 512, 1024, 2048, 3072, 4096, 5120
N_IN_B = _LRO + LANES
N_IN_A = 2 * D_RNN

PROMPT_TILE = 512
GLA_CHUNK = 128
FFN_TILE = 512
SAMPLE_ROW_BLOCK = 128
SAMPLE_STATE_BLOCK = 8
SAMPLE_ROW_GROUP = 16


def _cparams(sem):
    return pltpu.CompilerParams(dimension_semantics=sem, vmem_limit_bytes=VMEM_LIMIT_BYTES)


def _full_spec(shape):
    n = len(shape)
    return pl.BlockSpec(shape, lambda *_: (0,) * n, pipeline_mode=pl.Buffered(1))


def _full_out(shape):
    n = len(shape)
    return pl.BlockSpec(shape, lambda *_: (0,) * n)


def _wspec(l, shape):
    n = len(shape)
    return pl.BlockSpec((None,) + tuple(shape), lambda *_: (l,) + (0,) * n, pipeline_mode=pl.Buffered(1))


def _softplus(y):
    return jnp.maximum(y, 0.0) + jnp.log1p(jnp.exp(-jnp.abs(y)))


def _silu(y):
    return y * jax.nn.sigmoid(y)


def _gelu_tanh(y):
    return 0.5 * y * (1.0 + jnp.tanh(0.7978845608028654 * (y + 0.044715 * (y * y * y))))


def _rms(x):
    return lax.rsqrt(jnp.mean(x * x, axis=-1, keepdims=True) + EPS)


def _norm_mod(x, g, scale, shift):
    return x * _rms(x) * g * (1.0 + scale) + shift


def _mod_part(m, i):
    return m[:, i * D_MODEL:(i + 1) * D_MODEL]


def _dot(a, b):
    return jnp.dot(a.astype(BF16), b.astype(BF16), preferred_element_type=F32)


def _dot_nt(a, b):
    return lax.dot_general(a.astype(BF16), b.astype(BF16), (((1,), (1,)), ((), ())),
                           preferred_element_type=F32)


def _lru_coeffs(xc, wg_ref, ba, bx, lam, a_ref, b_ref):
    lamc = -LRU_C * _softplus(-lam)
    for n in range(LRU_BLOCKS):
        sl = slice(n * LRU_BW, (n + 1) * LRU_BW)
        xb = xc[:, sl]
        pre = _dot(xb, wg_ref[n])
        r = jax.nn.sigmoid(pre[:, :LRU_BW] + ba[:, sl])
        i = jax.nn.sigmoid(pre[:, LRU_BW:] + bx[:, sl])
        a = jnp.exp(lamc[:, sl] * r)
        a_ref[:, sl] = a
        b_ref[:, sl] = jnp.sqrt(1.0 - a * a) * (i * xb)


def _gla_gate(u, wa2_ref, gba):
    z = _dot(u[:, _LRO:_LRO + LANES], wa2_ref[...]) + gba
    return (jnp.minimum(z, 0.0) - jnp.log1p(jnp.exp(-jnp.abs(z)))) * (1.0 / GLA_TAU)


def _group_cumsum(g, group):
    rin = lax.broadcasted_iota(jnp.int32, g.shape, 0) % group
    x = g
    k = 1
    while k < group:
        x = x + jnp.where(rin >= k, pltpu.roll(x, k, 0), 0.0)
        k *= 2
    return x


def _col_bcast(row):
    return jnp.broadcast_to(row, (LANES, LANES)).T


def _head_norm_gate(o, gng, ur):
    parts = []
    for hd in range(GLA_HEADS):
        sl = slice(hd * GLA_DV, (hd + 1) * GLA_DV)
        oh = o[:, sl]
        parts.append(oh * _rms(oh) * gng[:, sl])
    return jnp.concatenate(parts, axis=1) * _silu(ur)


def _merge_out(x, gt1, u, ya, yb, pb_ref, wo_ref):
    pbv = _dot(yb, pb_ref[...])
    mm = jax.nn.sigmoid(u[:, _GAO:_GAO + D_MODEL]) * ya + jax.nn.sigmoid(u[:, _GBO:_GBO + D_MODEL]) * pbv
    return x + gt1 * _dot(mm, wo_ref[...])


def _mod_kernel(c_ref, w_ref, b_ref, o_ref):
    o_ref[0] = _dot(_silu(c_ref[...]), w_ref[0]) + b_ref[0]


def _modulation(c_all, ada_w, ada_b):
    depth = ada_w.shape[0]
    rows = c_all.shape[0]
    nblk = N_MOD // D_MODEL
    return pl.pallas_call(
        _mod_kernel,
        grid=(depth, nblk),
        in_specs=[
            pl.BlockSpec((rows, D_MODEL), lambda l, j: (0, 0)),
            pl.BlockSpec((1, D_MODEL, D_MODEL), lambda l, j: (l, 0, j)),
            pl.BlockSpec((1, 1, D_MODEL), lambda l, j: (l, 0, j)),
        ],
        out_specs=pl.BlockSpec((1, rows, D_MODEL), lambda l, j: (l, 0, j)),
        out_shape=jax.ShapeDtypeStruct((depth, rows, N_MOD), F32),
        compiler_params=_cparams(("arbitrary", "arbitrary")),
        name="modulation",
    )(c_all, ada_w, ada_b.reshape(depth, 1, N_MOD))


def _prompt_lru_kernel(x_ref, mod_ref, g1_ref, win_ref, cw_ref, cb_ref, wg_ref, ba_ref, bx_ref, lam_ref, pa_ref,
                       ya_ref, conv_ref, lru_ref, xa_s, a_s, b_s, h_s):
    t = pl.program_id(1)
    tt = x_ref.shape[1]

    @pl.when(t == 0)
    def _():
        xa_s[0:SUBLANES, :] = jnp.zeros((SUBLANES, D_RNN), F32)
        h_s[...] = jnp.zeros_like(h_s)

    x = x_ref[0]
    m = mod_ref[0]
    h = _norm_mod(x, g1_ref[...], _mod_part(m, 1), _mod_part(m, 0))
    u = _dot(h, win_ref[...])
    xa_s[SUBLANES:SUBLANES + tt, :] = u[:, :D_RNN]
    cw = cw_ref[...]
    xc = cb_ref[...]
    for i in range(CONV_W):
        off = SUBLANES - (CONV_W - 1) + i
        xc = xc + xa_s[off:off + tt, :] * cw[i:i + 1, :]
    _lru_coeffs(xc, wg_ref, ba_ref[...], bx_ref[...], lam_ref[...], a_s, b_s)

    def step(i, hc):
        r0 = pl.multiple_of(i * SUBLANES, SUBLANES)
        a8 = a_s[pl.ds(r0, SUBLANES), :]
        b8 = b_s[pl.ds(r0, SUBLANES), :]
        rows = []
        for j in range(SUBLANES):
            hc = a8[j:j + 1, :] * hc + b8[j:j + 1, :]
            rows.append(hc)
        b_s[pl.ds(r0, SUBLANES), :] = jnp.concatenate(rows, axis=0)
        return hc

    h_last = lax.fori_loop(0, tt // SUBLANES, step, h_s[0:1, :])
    h_s[0:1, :] = h_last
    ya_ref[0] = _dot(b_s[...] * _gelu_tanh(u[:, D_RNN:]), pa_ref[...])
    xa_s[0:SUBLANES, :] = xa_s[tt:tt + SUBLANES, :]

    @pl.when(t == pl.num_programs(1) - 1)
    def _():
        conv_ref[0] = xa_s[SUBLANES - (CONV_W - 1):SUBLANES, :]
        lru_ref[0] = h_last


def _prompt_lru(x, mod_p, w, l):
    b, t, _ = x.shape
    tt = PROMPT_TILE
    seq_spec = pl.BlockSpec((1, tt, D_MODEL), lambda i, j: (i, j, 0))
    return pl.pallas_call(
        _prompt_lru_kernel,
        grid=(b, t // tt),
        in_specs=[
            seq_spec,
            pl.BlockSpec((1, 1, N_MOD), lambda i, j: (i, 0, 0)),
            _wspec(l, (1, D_MODEL)),
            _wspec(l, (D_MODEL, N_IN_A)),
            _wspec(l, (CONV_W, D_RNN)),
            _wspec(l, (1, D_RNN)),
            _wspec(l, (LRU_BLOCKS, LRU_BW, 2 * LRU_BW)),
            _wspec(l, (1, D_RNN)),
            _wspec(l, (1, D_RNN)),
            _wspec(l, (1, D_RNN)),
            _wspec(l, (D_RNN, D_MODEL)),
        ],
        out_specs=[
            seq_spec,
            pl.BlockSpec((1, CONV_W - 1, D_RNN), lambda i, j: (i, 0, 0)),
            pl.BlockSpec((1, 1, D_RNN), lambda i, j: (i, 0, 0)),
        ],
        out_shape=[
            jax.ShapeDtypeStruct((b, t, D_MODEL), F32),
            jax.ShapeDtypeStruct((b, CONV_W - 1, D_RNN), F32),
            jax.ShapeDtypeStruct((b, 1, D_RNN), F32),
        ],
        scratch_shapes=[
            pltpu.VMEM((tt + SUBLANES, D_RNN), F32),
            pltpu.VMEM((tt, D_RNN), F32),
            pltpu.VMEM((tt, D_RNN), F32),
            pltpu.VMEM((SUBLANES, D_RNN), F32),
        ],
        compiler_params=_cparams(("arbitrary", "arbitrary")),
        name="prompt_lru",
    )(x, mod_p, w["g1"], w["win"], w["conv_w"], w["conv_b"], w["wgate"], w["lru_ba"], w["lru_bx"],
      w["lru_lambda"], w["proj_a"])


def _prompt_gla_kernel(x_ref, mod_ref, g1_ref, win_ref, wa2_ref, gba_ref, gng_ref, pb_ref, wo_ref, ya_ref,
                       x1_ref, st_ref, s_s, o_s):
    t = pl.program_id(1)
    tt = x_ref.shape[1]
    c = GLA_CHUNK

    @pl.when(t == 0)
    def _():
        s_s[...] = jnp.zeros_like(s_s)

    x = x_ref[0]
    m = mod_ref[0]
    h = _norm_mod(x, g1_ref[...], _mod_part(m, 1), _mod_part(m, 0))
    u = _dot(h, win_ref[...])
    g = _gla_gate(u, wa2_ref, gba_ref[...])
    bc = _group_cumsum(g, c)
    qt = u[:, _QO:_QO + GLA_DK_TOTAL] * (jnp.exp(bc) * (GLA_DK ** -0.5))
    uk = u[:, _KO:_KO + GLA_DK_TOTAL]
    kt = uk * jnp.exp(-bc)
    ri = lax.broadcasted_iota(jnp.int32, (c, c), 0)
    ci = lax.broadcasted_iota(jnp.int32, (c, c), 1)
    causal = ri >= ci
    for ch in range(tt // c):
        rows = slice(ch * c, (ch + 1) * c)
        bl = bc[ch * c + c - 1:ch * c + c, :]
        kd = uk[rows, :] * jnp.exp(bl - bc[rows, :])
        dec = jnp.exp(bl)
        for hd in range(GLA_HEADS):
            ks = slice(hd * GLA_DK, (hd + 1) * GLA_DK)
            vs = slice(_VO + hd * GLA_DV, _VO + (hd + 1) * GLA_DV)
            qh = qt[rows, ks].astype(BF16)
            vh = u[rows, vs].astype(BF16)
            att = jnp.where(causal, _dot_nt(qh, kt[rows, ks]), 0.0)
            s_old = s_s[hd]
            o_s[rows, hd * GLA_DV:(hd + 1) * GLA_DV] = _dot(att, vh) + _dot(qh, s_old)
            dm = _col_bcast(dec[:, ks])
            s_s[hd] = s_old * jnp.concatenate([dm, dm], axis=1) + _dot(kd[:, ks].T, vh)
    yb = _head_norm_gate(o_s[...], gng_ref[...], u[:, _RO:_RO + D_MODEL])
    x1_ref[0] = _merge_out(x, _mod_part(m, 2), u, ya_ref[0], yb, pb_ref, wo_ref)

    @pl.when(t == pl.num_programs(1) - 1)
    def _():
        st_ref[0] = s_s[...]


def _prompt_gla(x, mod_p, ya, w, l):
    b, t, _ = x.shape
    tt = PROMPT_TILE
    seq_spec = pl.BlockSpec((1, tt, D_MODEL), lambda i, j: (i, j, 0))
    return pl.pallas_call(
        _prompt_gla_kernel,
        grid=(b, t // tt),
        in_specs=[
            seq_spec,
            pl.BlockSpec((1, 1, N_MOD), lambda i, j: (i, 0, 0)),
            _wspec(l, (1, D_MODEL)),
            _wspec(l, (D_MODEL, N_IN_B)),
            _wspec(l, (LANES, GLA_DK_TOTAL)),
            _wspec(l, (1, GLA_DK_TOTAL)),
            _wspec(l, (1, GLA_DV_TOTAL)),
            _wspec(l, (GLA_DV_TOTAL, D_MODEL)),
            _wspec(l, (D_MODEL, D_MODEL)),
            seq_spec,
        ],
        out_specs=[
            seq_spec,
            pl.BlockSpec((1, GLA_HEADS, GLA_DK, GLA_DV), lambda i, j: (i, 0, 0, 0)),
        ],
        out_shape=[
            jax.ShapeDtypeStruct((b, t, D_MODEL), F32),
            jax.ShapeDtypeStruct((b, GLA_HEADS, GLA_DK, GLA_DV), F32),
        ],
        scratch_shapes=[
            pltpu.VMEM((GLA_HEADS, GLA_DK, GLA_DV), F32),
            pltpu.VMEM((tt, GLA_DV_TOTAL), F32),
        ],
        compiler_params=_cparams(("arbitrary", "arbitrary")),
        name="prompt_gla",
    )(x, mod_p, w["g1"], w["win_b"], w["wa2"], w["gla_ba"], w["gla_norm_g"], w["proj_b"], w["w_out"], ya)


def _ffn_kernel(x_ref, mod_ref, g2_ref, w1_ref, w2_ref, fg_ref, o_ref, *, final_norm):
    x = x_ref[...]
    m = mod_ref[0]
    h2 = _norm_mod(x, g2_ref[...], _mod_part(m, 4), _mod_part(m, 3))
    f = _dot(h2, w1_ref[...])
    x2 = x + _mod_part(m, 5) * _dot(_silu(f[:, :D_FF]) * f[:, D_FF:], w2_ref[...])
    if final_norm:
        x2 = x2 * _rms(x2) * fg_ref[...]
    o_ref[...] = x2


def _ffn(x2d, mod3, rows_per_mod, tile, w, l, final_g, final_norm):
    mrows = x2d.shape[0]
    r = mod3.shape[1]
    per = rows_per_mod // tile
    row_spec = pl.BlockSpec((tile, D_MODEL), lambda i: (i, 0))
    return pl.pallas_call(
        functools.partial(_ffn_kernel, final_norm=final_norm),
        grid=(mrows // tile,),
        in_specs=[
            row_spec,
            pl.BlockSpec((1, r, N_MOD), lambda i: (i // per, 0, 0)),
            _wspec(l, (1, D_MODEL)),
            _wspec(l, (D_MODEL, 2 * D_FF)),
            _wspec(l, (D_FF, D_MODEL)),
            _full_spec((1, D_MODEL)),
        ],
        out_specs=row_spec,
        out_shape=jax.ShapeDtypeStruct((mrows, D_MODEL), F32),
        compiler_params=_cparams(("arbitrary",)),
        name="ffn_final" if final_norm else "ffn",
    )(x2d, mod3, w["g2"], w["ffn_w1"], w["ffn_w2"], final_g)


def _sample_lru_kernel(x_ref, mod_ref, g1_ref, win_ref, cw_ref, cb_ref, wg_ref, ba_ref, bx_ref, lam_ref, pa_ref,
                       conv0_ref, h0_ref, ya_ref, conv_ref, lru_ref, a_s, b_s):
    ns = h0_ref.shape[0]
    nt = x_ref.shape[0] // ns
    m = mod_ref[...]
    tile = lambda v: jnp.concatenate([v] * nt, axis=0)
    h = _norm_mod(x_ref[...], g1_ref[...], tile(_mod_part(m, 1)), tile(_mod_part(m, 0)))
    u = _dot(h, win_ref[...])
    xa = [conv0_ref[i] for i in range(CONV_W - 1)] + [u[i * ns:(i + 1) * ns, :D_RNN] for i in range(nt)]
    cw = cw_ref[...]
    xcs = []
    for ti in range(nt):
        acc = cb_ref[...]
        for i in range(CONV_W):
            acc = acc + xa[ti + i] * cw[i:i + 1, :]
        xcs.append(acc)
    _lru_coeffs(jnp.concatenate(xcs, axis=0), wg_ref, ba_ref[...], bx_ref[...], lam_ref[...], a_s, b_s)
    hc = h0_ref[...]
    for ti in range(nt):
        rows = slice(ti * ns, (ti + 1) * ns)
        hc = a_s[rows, :] * hc + b_s[rows, :]
        b_s[rows, :] = hc
    ya_ref[...] = _dot(b_s[...] * _gelu_tanh(u[:, D_RNN:]), pa_ref[...])
    for i in range(CONV_W - 1):
        conv_ref[i] = xa[nt + i]
    lru_ref[...] = hc


def _sample_lru(x_tm, mod_s, conv0_tm, h0_all, w, l):
    rows = x_tm.shape[0]
    ns = h0_all.shape[1]
    return pl.pallas_call(
        _sample_lru_kernel,
        grid=(1,),
        in_specs=[
            _full_spec((rows, D_MODEL)),
            _full_spec((ns, N_MOD)),
            _wspec(l, (1, D_MODEL)),
            _wspec(l, (D_MODEL, N_IN_A)),
            _wspec(l, (CONV_W, D_RNN)),
            _wspec(l, (1, D_RNN)),
            _wspec(l, (LRU_BLOCKS, LRU_BW, 2 * LRU_BW)),
            _wspec(l, (1, D_RNN)),
            _wspec(l, (1, D_RNN)),
            _wspec(l, (1, D_RNN)),
            _wspec(l, (D_RNN, D_MODEL)),
            _full_spec((CONV_W - 1, ns, D_RNN)),
            _wspec(l, (ns, D_RNN)),
        ],
        out_specs=[
            _full_out((rows, D_MODEL)),
            _full_out((CONV_W - 1, ns, D_RNN)),
            _full_out((ns, D_RNN)),
        ],
        out_shape=[
            jax.ShapeDtypeStruct((rows, D_MODEL), F32),
            jax.ShapeDtypeStruct((CONV_W - 1, ns, D_RNN), F32),
            jax.ShapeDtypeStruct((ns, D_RNN), F32),
        ],
        scratch_shapes=[pltpu.VMEM((rows, D_RNN), F32), pltpu.VMEM((rows, D_RNN), F32)],
        compiler_params=_cparams(("arbitrary",)),
        name="sample_lru",
    )(x_tm, mod_s, w["g1"], w["win"], w["conv_w"], w["conv_b"], w["wgate"], w["lru_ba"], w["lru_bx"],
      w["lru_lambda"], w["proj_a"], conv0_tm, h0_all)


def _sample_inproj_kernel(x_ref, mod_ref, g1_ref, win_ref, u_ref):
    nt = x_ref.shape[0] // mod_ref.shape[0]
    m = mod_ref[...]
    tile = lambda v: jnp.concatenate([v] * nt, axis=0)
    h = _norm_mod(x_ref[...], g1_ref[...], tile(_mod_part(m, 1)), tile(_mod_part(m, 0)))
    u_ref[...] = _dot(h, win_ref[...])


def _sample_inproj(x_tm, mod_s, w, l):
    rows = x_tm.shape[0]
    return pl.pallas_call(
        _sample_inproj_kernel,
        grid=(1,),
        in_specs=[
            _full_spec((rows, D_MODEL)),
            _full_spec(mod_s.shape),
            _wspec(l, (1, D_MODEL)),
            _wspec(l, (D_MODEL, N_IN_B)),
        ],
        out_specs=_full_out((rows, N_IN_B)),
        out_shape=jax.ShapeDtypeStruct((rows, N_IN_B), F32),
        compiler_params=_cparams(("arbitrary",)),
        name="sample_inproj",
    )(x_tm, mod_s, w["g1"], w["win_b"])


def _sample_gla_kernel(u_ref, wa2_ref, gba_ref, gng_ref, s0_ref, *rest, nt, carries_buffer):
    yb_ref, st_ref, o_s, q_s, v_s, kdt_s, bl_s = rest[1:] if carries_buffer else rest
    rows = u_ref.shape[0]
    steps_per_rows = rows // (SAMPLE_STATE_BLOCK * nt)
    sub = pl.program_id(0) % steps_per_rows
    rg = SAMPLE_ROW_GROUP
    seq_per_group = rg // nt

    @pl.when(sub == 0)
    def _():
        u = u_ref[...]
        g = _gla_gate(u, wa2_ref, gba_ref[...])
        bc = _group_cumsum(g, nt)
        rin = lax.broadcasted_iota(jnp.int32, g.shape, 0) % nt
        sfx = jnp.zeros_like(g)
        for k in range(1, nt):
            sfx = sfx + jnp.where(rin + k < nt, pltpu.roll(g, rows - k, 0), 0.0)
        qt = u[:, _QO:_QO + GLA_DK_TOTAL] * (jnp.exp(bc) * (GLA_DK ** -0.5))
        uk = u[:, _KO:_KO + GLA_DK_TOTAL]
        kt = uk * jnp.exp(-bc)
        kd = uk * jnp.exp(sfx)
        q_s[...] = qt.astype(BF16)
        v_s[...] = u[:, _VO:_VO + GLA_DV_TOTAL].astype(BF16)
        bl_s[...] = bc + sfx
        ri = lax.broadcasted_iota(jnp.int32, (rows, rows), 0)
        ci = lax.broadcasted_iota(jnp.int32, (rows, rows), 1)
        same_seq_causal = (ri // nt == ci // nt) & (ri >= ci)
        for hd in range(GLA_HEADS):
            ks = slice(hd * GLA_DK, (hd + 1) * GLA_DK)
            att = jnp.where(same_seq_causal, _dot_nt(qt[:, ks], kt[:, ks]), 0.0)
            o_s[:, hd * GLA_DV:(hd + 1) * GLA_DV] = _dot(att, v_s[:, hd * GLA_DV:(hd + 1) * GLA_DV])
            kdt_s[hd] = kd[:, ks].T

    lane_seq = lax.broadcasted_iota(jnp.int32, (GLA_DK, rows), 1) // nt
    row_seq = lax.broadcasted_iota(jnp.int32, (rg, GLA_DV), 0) // nt

    def group(gi, carry):
        r0 = pl.multiple_of((sub * (SAMPLE_STATE_BLOCK // seq_per_group) + gi) * rg, rg)
        dec_g = jnp.exp(bl_s[pl.ds(r0, rg), :])
        for hd in range(GLA_HEADS):
            ks = slice(hd * GLA_DK, (hd + 1) * GLA_DK)
            vsl = slice(hd * GLA_DV, (hd + 1) * GLA_DV)
            qg = q_s[pl.ds(r0, rg), ks]
            vh = v_s[:, vsl]
            inter = jnp.zeros((rg, GLA_DV), F32)
            for j in range(seq_per_group):
                s_loc = gi * seq_per_group + j
                s_row = sub * SAMPLE_STATE_BLOCK + s_loc
                s_old = s0_ref[s_loc, hd]
                inter = jnp.where(row_seq == j, _dot(qg, s_old), inter)
                kdt = jnp.where(lane_seq == s_row, kdt_s[hd], 0.0)
                dm = _col_bcast(dec_g[j * nt:j * nt + 1, ks])
                st_ref[s_loc, hd] = s_old * jnp.concatenate([dm, dm], axis=1) + _dot(kdt, vh)
            o_s[pl.ds(r0, rg), vsl] = o_s[pl.ds(r0, rg), vsl] + inter
        return carry

    lax.fori_loop(0, SAMPLE_STATE_BLOCK // seq_per_group, group, 0)

    @pl.when(sub == steps_per_rows - 1)
    def _():
        yb_ref[...] = _head_norm_gate(o_s[...], gng_ref[...], u_ref[:, _RO:_RO + D_MODEL])


def _sample_gla(u_sm, state_all, new_state_all, w, l, nt):
    rows = u_sm.shape[0]
    br = SAMPLE_ROW_BLOCK
    sb = SAMPLE_STATE_BLOCK
    per = br // (sb * nt)
    state_spec = pl.BlockSpec((None, sb, GLA_HEADS, GLA_DK, GLA_DV), lambda i: (l, i, 0, 0, 0))
    in_specs = [
        pl.BlockSpec((br, N_IN_B), lambda i: (i // per, 0)),
        _wspec(l, (LANES, GLA_DK_TOTAL)),
        _wspec(l, (1, GLA_DK_TOTAL)),
        _wspec(l, (1, GLA_DV_TOTAL)),
        state_spec,
    ]
    args = [u_sm, w["wa2"], w["gla_ba"], w["gla_norm_g"], state_all]
    aliases = {}
    if new_state_all is not None:
        in_specs.append(pl.BlockSpec(memory_space=pl.ANY))
        args.append(new_state_all)
        aliases = {len(args) - 1: 1}
    return pl.pallas_call(
        functools.partial(_sample_gla_kernel, nt=nt, carries_buffer=new_state_all is not None),
        grid=(rows // (sb * nt),),
        in_specs=in_specs,
        out_specs=[
            pl.BlockSpec((br, D_MODEL), lambda i: (i // per, 0)),
            state_spec,
        ],
        out_shape=[
            jax.ShapeDtypeStruct((rows, D_MODEL), F32),
            jax.ShapeDtypeStruct(state_all.shape, F32),
        ],
        input_output_aliases=aliases,
        scratch_shapes=[
            pltpu.VMEM((br, GLA_DV_TOTAL), F32),
            pltpu.VMEM((br, GLA_DK_TOTAL), BF16),
            pltpu.VMEM((br, GLA_DV_TOTAL), BF16),
            pltpu.VMEM((GLA_HEADS, GLA_DK, br), F32),
            pltpu.VMEM((br, GLA_DK_TOTAL), F32),
        ],
        compiler_params=_cparams(("arbitrary",)),
        name="sample_gla",
    )(*args)


def _sample_merge_kernel(x_ref, mod_ref, u_ref, ya_ref, yb_ref, pb_ref, wo_ref, x1_ref):
    nt = x_ref.shape[0] // mod_ref.shape[0]
    gt1 = jnp.concatenate([_mod_part(mod_ref[...], 2)] * nt, axis=0)
    x1_ref[...] = _merge_out(x_ref[...], gt1, u_ref[...], ya_ref[...], yb_ref[...], pb_ref, wo_ref)


def _sample_merge(x_tm, mod_s, u_tm, ya, yb_tm, w, l):
    rows = x_tm.shape[0]
    return pl.pallas_call(
        _sample_merge_kernel,
        grid=(1,),
        in_specs=[
            _full_spec((rows, D_MODEL)),
            _full_spec(mod_s.shape),
            _full_spec((rows, N_IN_B)),
            _full_spec((rows, D_MODEL)),
            _full_spec((rows, D_MODEL)),
            _wspec(l, (GLA_DV_TOTAL, D_MODEL)),
            _wspec(l, (D_MODEL, D_MODEL)),
        ],
        out_specs=_full_out((rows, D_MODEL)),
        out_shape=jax.ShapeDtypeStruct((rows, D_MODEL), F32),
        compiler_params=_cparams(("arbitrary",)),
        name="sample_merge",
    )(x_tm, mod_s, u_tm, ya, yb_tm, w["proj_b"], w["w_out"])


def _stacked_weights(norm1_g, norm2_g, w_in, conv_w, conv_b, lru_wa, lru_ba, lru_wx, lru_bx, lru_lambda,
                     gla_wa2, gla_ba, gla_norm_g, proj_a, proj_b, w_out, ffn_w1, ffn_w2):
    depth = w_in.shape[0]
    o_q = 2 * D_RNN
    o_lr = o_q + 2 * GLA_DK_TOTAL + 2 * GLA_DV_TOTAL
    o_ga = o_lr + GLA_RANK
    win = w_in.astype(BF16)
    win_b = jnp.concatenate(
        [win[:, :, o_q:o_lr], win[:, :, o_ga:], win[:, :, o_lr:o_ga],
         jnp.zeros((depth, D_MODEL, LANES - GLA_RANK), BF16)], axis=2)
    row = lambda v: v.reshape(depth, 1, -1)
    return {
        "g1": row(norm1_g), "g2": row(norm2_g),
        "win": win, "win_b": win_b,
        "conv_w": conv_w, "conv_b": row(conv_b),
        "wgate": jnp.concatenate([lru_wa, lru_wx], axis=-1).astype(BF16),
        "lru_ba": row(lru_ba), "lru_bx": row(lru_bx), "lru_lambda": row(lru_lambda),
        "wa2": jnp.pad(gla_wa2, ((0, 0), (0, LANES - GLA_RANK), (0, 0))).astype(BF16),
        "gla_ba": row(gla_ba), "gla_norm_g": row(gla_norm_g),
        "proj_a": proj_a.astype(BF16), "proj_b": proj_b.astype(BF16), "w_out": w_out.astype(BF16),
        "ffn_w1": ffn_w1.astype(BF16), "ffn_w2": ffn_w2.astype(BF16),
    }


def kernel(x_prompt, x_sample, c_prompt, c_sample, state_conv, state_lru, state_gla, norm1_g, norm2_g, ada_w, ada_b,
           w_in, conv_w, conv_b, lru_wa, lru_ba, lru_wx, lru_bx, lru_lambda, gla_wa2, gla_ba, gla_norm_g, proj_a,
           proj_b, w_out, ffn_w1, ffn_w2, final_g):
    bp, tp, _ = x_prompt.shape
    bs, ts, _ = x_sample.shape
    depth = w_in.shape[0]
    mod = _modulation(jnp.concatenate([c_prompt, c_sample], axis=0), ada_w, ada_b)
    w = _stacked_weights(norm1_g, norm2_g, w_in, conv_w, conv_b, lru_wa, lru_ba, lru_wx, lru_bx, lru_lambda,
                         gla_wa2, gla_ba, gla_norm_g, proj_a, proj_b, w_out, ffn_w1, ffn_w2)
    fg = final_g.reshape(1, D_MODEL)
    xp = x_prompt
    xs = x_sample.transpose(1, 0, 2).reshape(ts * bs, D_MODEL)
    conv_p, lru_p, gla_p, conv_s, lru_s = [], [], [], [], []
    gla_s = None
    for l in range(depth):
        last = l == depth - 1
        mod_p = mod[l, :bp].reshape(bp, 1, N_MOD)
        mod_s = mod[l, bp:]
        ya, cb, ht = _prompt_lru(xp, mod_p, w, l)
        x1, st = _prompt_gla(xp, mod_p, ya, w, l)
        xp = _ffn(x1.reshape(bp * tp, D_MODEL), mod_p, tp, FFN_TILE, w, l, fg, last).reshape(bp, tp, D_MODEL)
        conv_p.append(cb)
        lru_p.append(ht.reshape(bp, D_RNN))
        gla_p.append(st)
        ya_s, cb_s, ht_s = _sample_lru(xs, mod_s, state_conv[l].transpose(1, 0, 2), state_lru, w, l)
        u_tm = _sample_inproj(xs, mod_s, w, l)
        u_sm = u_tm.reshape(ts, bs, N_IN_B).transpose(1, 0, 2).reshape(bs * ts, N_IN_B)
        yb_sm, gla_s = _sample_gla(u_sm, state_gla, gla_s, w, l, ts)
        yb_tm = yb_sm.reshape(bs, ts, D_MODEL).transpose(1, 0, 2).reshape(ts * bs, D_MODEL)
        x1_s = _sample_merge(xs, mod_s, u_tm, ya_s, yb_tm, w, l)
        xs = _ffn(x1_s, mod_s.reshape(1, bs, N_MOD), bs * ts, bs, w, l, fg, last)
        conv_s.append(cb_s.transpose(1, 0, 2))
        lru_s.append(ht_s)
    y_sample = xs.reshape(ts, bs, D_MODEL).transpose(1, 0, 2)
    return (xp, y_sample, jnp.stack(conv_p), jnp.stack(lru_p), jnp.stack(gla_p),
            jnp.stack(conv_s), jnp.stack(lru_s), gla_s)
```

```python
import functools

import jax
import jax.numpy as jnp
from jax import lax
from jax.experimental import pallas as pl
from jax.experimental.pallas import tpu as pltpu

F32 = jnp.float32
BF16 = jnp.bfloat16

D_MODEL = 1024
D_RNN = 1280
LRU_BW = 128
LRU_BLOCKS = D_RNN // LRU_BW
CONV_W = 4
LRU_C = 8.0
GLA_HEADS = 4
GLA_DK = 128
GLA_DV = 256
GLA_DK_TOTAL = GLA_HEADS * GLA_DK
GLA_DV_TOTAL = GLA_HEADS * GLA_DV
GLA_RANK = 16
GLA_TAU = 16.0
D_FF = 2816
EPS = 1e-6
TINY_F32 = 1.1754944e-38
N_MOD = 6 * D_MODEL

LANES = 128
SUBLANES = 8
VMEM_LIMIT_BYTES = 60 * 1024 * 1024

_LRO, _QO, _KO, _VO, _RO, _GAO, _GBO = 0, 128, 640, 1152, 2176, 3200, 4224
N_IN_B = _GBO + D_MODEL
N_IN_A = 2 * D_RNN

PROMPT_TILE = 256
GLA_CHUNK = 128
FFN_TILE = 512
SAMPLE_ROW_BLOCK = 128
SAMPLE_STATE_BLOCK = 8
SAMPLE_ROW_GROUP = 16


def _cparams(sem):
    return pltpu.CompilerParams(dimension_semantics=sem, vmem_limit_bytes=VMEM_LIMIT_BYTES)


def _full_spec(shape):
    n = len(shape)
    return pl.BlockSpec(shape, lambda *_: (0,) * n, pipeline_mode=pl.Buffered(1))


def _full_out(shape):
    n = len(shape)
    return pl.BlockSpec(shape, lambda *_: (0,) * n)


def _wspec(l, shape):
    n = len(shape)
    return pl.BlockSpec((None,) + tuple(shape), lambda *_: (l,) + (0,) * n, pipeline_mode=pl.Buffered(1))


def _softplus(y):
    return jnp.maximum(y, 0.0) + jnp.log1p(jnp.exp(-jnp.abs(y)))


def _sigmoid(y):
    return 0.5 * jnp.tanh(0.5 * y) + 0.5


def _silu(y):
    return y * _sigmoid(y)


def _sqrt_nonneg(s):
    return s * lax.rsqrt(jnp.maximum(s, TINY_F32))


def _gelu_tanh(y):
    return 0.5 * y * (1.0 + jnp.tanh(0.7978845608028654 * (y + 0.044715 * (y * y * y))))


def _rms(x):
    return lax.rsqrt(jnp.mean(x * x, axis=-1, keepdims=True) + EPS)


def _norm_mod(x, g, scale, shift):
    return x * _rms(x) * g * (1.0 + scale) + shift


def _mod_part(m, i):
    return m[:, i * D_MODEL:(i + 1) * D_MODEL]


def _dot(a, b):
    return jnp.dot(a.astype(BF16), b.astype(BF16), preferred_element_type=F32)


def _dot_nt(a, b):
    return lax.dot_general(a.astype(BF16), b.astype(BF16), (((1,), (1,)), ((), ())),
                           preferred_element_type=F32)


def _lru_coeffs(xc, wg_ref, ba, bx, lam):
    lamc = -LRU_C * _softplus(-lam)
    a_parts, b_parts = [], []
    for n in range(LRU_BLOCKS):
        sl = slice(n * LRU_BW, (n + 1) * LRU_BW)
        xb = xc[:, sl]
        pre = _dot(xb, wg_ref[n])
        r = _sigmoid(pre[:, :LRU_BW] + ba[:, sl])
        i = _sigmoid(pre[:, LRU_BW:] + bx[:, sl])
        a = jnp.exp(lamc[:, sl] * r)
        a_parts.append(a)
        b_parts.append(_sqrt_nonneg(1.0 - a * a) * (i * xb))
    return jnp.concatenate(a_parts, axis=1), jnp.concatenate(b_parts, axis=1)


def _gla_gate(u, wa2_ref, gba):
    z = _dot(u[:, _LRO:_LRO + LANES], wa2_ref[...]) + gba
    return (jnp.minimum(z, 0.0) - jnp.log1p(jnp.exp(-jnp.abs(z)))) * (1.0 / GLA_TAU)


def _group_cumsum(g, group):
    rin = lax.broadcasted_iota(jnp.int32, g.shape, 0) % group
    x = g
    k = 1
    while k < group:
        x = x + jnp.where(rin >= k, pltpu.roll(x, k, 0), 0.0)
        k *= 2
    return x


def _col_bcast(row):
    return jnp.broadcast_to(row, (LANES, LANES)).T


def _head_norm_gate(o, gng, ur):
    parts = []
    for hd in range(GLA_HEADS):
        sl = slice(hd * GLA_DV, (hd + 1) * GLA_DV)
        oh = o[:, sl]
        parts.append(oh * _rms(oh) * gng[:, sl])
    return jnp.concatenate(parts, axis=1) * _silu(ur)


def _merge_out(x, gt1, u, ya, yb, pb_ref, wo_ref):
    pbv = _dot(yb, pb_ref[...])
    mm = _sigmoid(u[:, _GAO:_GAO + D_MODEL]) * ya + _sigmoid(u[:, _GBO:_GBO + D_MODEL]) * pbv
    return x + gt1 * _dot(mm, wo_ref[...])


def _mod_kernel(c_ref, w_ref, b_ref, o_ref):
    o_ref[0] = _dot(_silu(c_ref[...]), w_ref[0]) + b_ref[0]


def _modulation(c_all, ada_w, ada_b):
    depth = ada_w.shape[0]
    rows = c_all.shape[0]
    nblk = N_MOD // D_MODEL
    return pl.pallas_call(
        _mod_kernel,
        grid=(depth, nblk),
        in_specs=[
            pl.BlockSpec((rows, D_MODEL), lambda l, j: (0, 0)),
            pl.BlockSpec((1, D_MODEL, D_MODEL), lambda l, j: (l, 0, j)),
            pl.BlockSpec((1, 1, D_MODEL), lambda l, j: (l, 0, j)),
        ],
        out_specs=pl.BlockSpec((1, rows, D_MODEL), lambda l, j: (l, 0, j)),
        out_shape=jax.ShapeDtypeStruct((depth, rows, N_MOD), F32),
        compiler_params=_cparams(("arbitrary", "arbitrary")),
        name="modulation",
    )(c_all, ada_w, ada_b.reshape(depth, 1, N_MOD))


def _shift_rows_in(blk, first_row):
    top = lax.broadcasted_iota(jnp.int32, blk.shape, 0) == 0
    return jnp.where(top, first_row, pltpu.roll(blk, 1, 0))


def _prompt_lru_branch(h, wina_ref, cw_ref, cb_ref, wg_ref, ba, bx, lam, pa_ref, conv_s, hc_s):
    tt = h.shape[0]
    ng = tt // SUBLANES
    ri = lax.broadcasted_iota(jnp.int32, (tt, tt), 0)
    ci = lax.broadcasted_iota(jnp.int32, (tt, tt), 1)
    to_blocks = jnp.where(ci == SUBLANES * (ri % ng) + ri // ng, 1.0, 0.0).astype(BF16)
    from_blocks = jnp.where(ci == ng * (ri % SUBLANES) + ri // SUBLANES, 1.0, 0.0).astype(BF16)
    hb = jnp.dot(to_blocks, h.astype(BF16), preferred_element_type=F32).astype(BF16)
    u = jnp.dot(hb, wina_ref[...], preferred_element_type=F32)
    xs = [u[j * ng:(j + 1) * ng, :D_RNN] for j in range(SUBLANES)]
    prev = {k: _shift_rows_in(xs[k], conv_s[k - (SUBLANES - CONV_W + 1):k - (SUBLANES - CONV_W), :])
            for k in range(SUBLANES - CONV_W + 1, SUBLANES)}
    cw = cw_ref[...]
    xcs = []
    for j in range(SUBLANES):
        acc = cb_ref[...]
        for i in range(CONV_W):
            d = CONV_W - 1 - i
            src = xs[j - d] if j >= d else prev[j - d + SUBLANES]
            acc = acc + src * cw[i:i + 1, :]
        xcs.append(acc)
    a, bb = _lru_coeffs(jnp.concatenate(xcs, axis=0), wg_ref, ba, bx, lam)
    loc = [bb[0:ng, :]]
    cum = [a[0:ng, :]]
    for j in range(1, SUBLANES):
        aj = a[j * ng:(j + 1) * ng, :]
        loc.append(aj * loc[-1] + bb[j * ng:(j + 1) * ng, :])
        cum.append(aj * cum[-1])
    ga, gb = cum[-1], loc[-1]
    rowg = lax.broadcasted_iota(jnp.int32, ga.shape, 0)
    k = 1
    while k < ng:
        ga_sh = jnp.where(rowg >= k, pltpu.roll(ga, k, 0), 1.0)
        gb_sh = jnp.where(rowg >= k, pltpu.roll(gb, k, 0), 0.0)
        gb = ga * gb_sh + gb
        ga = ga * ga_sh
        k *= 2
    carry = hc_s[0:1, :]
    leaving = gb + ga * carry
    entering = _shift_rows_in(leaving, carry)
    hs = jnp.concatenate([loc[j] + cum[j] * entering for j in range(SUBLANES)], axis=0)
    ya = (hs * _gelu_tanh(u[:, D_RNN:])).astype(BF16)
    ya = jnp.dot(from_blocks, ya, preferred_element_type=F32).astype(BF16)
    for k in range(SUBLANES - CONV_W + 1, SUBLANES):
        conv_s[k - (SUBLANES - CONV_W + 1):k - (SUBLANES - CONV_W), :] = xs[k][ng - 1:ng, :]
    hc_s[0:1, :] = leaving[ng - 1:ng, :]
    return jnp.dot(ya, pa_ref[...], preferred_element_type=F32)


def _prompt_gla_branch(u, wa2_ref, gba, gng, s_s):
    tt = u.shape[0]
    c = GLA_CHUNK
    g = _gla_gate(u, wa2_ref, gba)
    bc = _group_cumsum(g, c)
    qt = u[:, _QO:_QO + GLA_DK_TOTAL] * (jnp.exp(bc) * (GLA_DK ** -0.5))
    uk = u[:, _KO:_KO + GLA_DK_TOTAL]
    kt = uk * jnp.exp(-bc)
    ri = lax.broadcasted_iota(jnp.int32, (c, c), 0)
    ci = lax.broadcasted_iota(jnp.int32, (c, c), 1)
    causal = ri >= ci
    o_rows = []
    for ch in range(tt // c):
        rows = slice(ch * c, (ch + 1) * c)
        bl = bc[ch * c + c - 1:ch * c + c, :]
        kd = uk[rows, :] * jnp.exp(bl - bc[rows, :])
        dec = jnp.exp(bl)
        o_heads = []
        for hd in range(GLA_HEADS):
            ks = slice(hd * GLA_DK, (hd + 1) * GLA_DK)
            vs = slice(_VO + hd * GLA_DV, _VO + (hd + 1) * GLA_DV)
            qh = qt[rows, ks].astype(BF16)
            vh = u[rows, vs].astype(BF16)
            att = jnp.where(causal, _dot_nt(qh, kt[rows, ks]), 0.0)
            s_old = s_s[hd]
            o_heads.append(_dot(att, vh) + _dot(qh, s_old))
            dm = _col_bcast(dec[:, ks])
            s_s[hd] = s_old * jnp.concatenate([dm, dm], axis=1) + _dot(kd[:, ks].T, vh)
        o_rows.append(jnp.concatenate(o_heads, axis=1))
    return _head_norm_gate(jnp.concatenate(o_rows, axis=0), gng, u[:, _RO:_RO + D_MODEL])


def _prompt_mixer_kernel(x_ref, mod_ref, g1_ref, wina_ref, winb_ref, cw_ref, cb_ref, wg_ref, ba_ref, bx_ref, lam_ref,
                         pa_ref, wa2_ref, gba_ref, gng_ref, pb_ref, wo_ref,
                         x1_ref, conv_ref, lru_ref, st_ref, conv_s, hc_s, s_s):
    t = pl.program_id(1)

    @pl.when(t == 0)
    def _():
        conv_s[...] = jnp.zeros_like(conv_s)
        hc_s[...] = jnp.zeros_like(hc_s)
        s_s[...] = jnp.zeros_like(s_s)

    x = x_ref[0]
    m = mod_ref[0]
    h = _norm_mod(x, g1_ref[...], _mod_part(m, 1), _mod_part(m, 0))
    ya = _prompt_lru_branch(h, wina_ref, cw_ref, cb_ref, wg_ref, ba_ref[...], bx_ref[...], lam_ref[...], pa_ref,
                            conv_s, hc_s)
    u = _dot(h, winb_ref[...])
    yb = _prompt_gla_branch(u, wa2_ref, gba_ref[...], gng_ref[...], s_s)
    x1_ref[0] = _merge_out(x, _mod_part(m, 2), u, ya, yb, pb_ref, wo_ref)

    @pl.when(t == pl.num_programs(1) - 1)
    def _():
        conv_ref[0] = conv_s[0:CONV_W - 1, :]
        lru_ref[0] = hc_s[0:1, :]
        st_ref[0] = s_s[...]


def _prompt_mixer(x, mod_p, w, l):
    b, t, _ = x.shape
    tt = PROMPT_TILE
    seq_spec = pl.BlockSpec((1, tt, D_MODEL), lambda i, j: (i, j, 0))
    return pl.pallas_call(
        _prompt_mixer_kernel,
        grid=(b, t // tt),
        in_specs=[
            seq_spec,
            pl.BlockSpec((1, 1, N_MOD), lambda i, j: (i, 0, 0)),
            _wspec(l, (1, D_MODEL)),
            _wspec(l, (D_MODEL, N_IN_A)),
            _wspec(l, (D_MODEL, N_IN_B)),
            _wspec(l, (CONV_W, D_RNN)),
            _wspec(l, (1, D_RNN)),
            _wspec(l, (LRU_BLOCKS, LRU_BW, 2 * LRU_BW)),
            _wspec(l, (1, D_RNN)),
            _wspec(l, (1, D_RNN)),
            _wspec(l, (1, D_RNN)),
            _wspec(l, (D_RNN, D_MODEL)),
            _wspec(l, (LANES, GLA_DK_TOTAL)),
            _wspec(l, (1, GLA_DK_TOTAL)),
            _wspec(l, (1, GLA_DV_TOTAL)),
            _wspec(l, (GLA_DV_TOTAL, D_MODEL)),
            _wspec(l, (D_MODEL, D_MODEL)),
        ],
        out_specs=[
            seq_spec,
            pl.BlockSpec((1, CONV_W - 1, D_RNN), lambda i, j: (i, 0, 0)),
            pl.BlockSpec((1, 1, D_RNN), lambda i, j: (i, 0, 0)),
            pl.BlockSpec((1, GLA_HEADS, GLA_DK, GLA_DV), lambda i, j: (i, 0, 0, 0)),
        ],
        out_shape=[
            jax.ShapeDtypeStruct((b, t, D_MODEL), F32),
            jax.ShapeDtypeStruct((b, CONV_W - 1, D_RNN), F32),
            jax.ShapeDtypeStruct((b, 1, D_RNN), F32),
            jax.ShapeDtypeStruct((b, GLA_HEADS, GLA_DK, GLA_DV), F32),
        ],
        scratch_shapes=[
            pltpu.VMEM((SUBLANES, D_RNN), F32),
            pltpu.VMEM((SUBLANES, D_RNN), F32),
            pltpu.VMEM((GLA_HEADS, GLA_DK, GLA_DV), F32),
        ],
        compiler_params=_cparams(("arbitrary", "arbitrary")),
        name="prompt_mixer",
    )(x, mod_p, w["g1"], w["win"], w["win_b"], w["conv_w"], w["conv_b"], w["wgate"], w["lru_ba"], w["lru_bx"],
      w["lru_lambda"], w["proj_a"], w["wa2"], w["gla_ba"], w["gla_norm_g"], w["proj_b"], w["w_out"])


def _ffn_kernel(x_ref, mod_ref, g2_ref, w1_ref, w2_ref, fg_ref, o_ref, *, final_norm):
    x = x_ref[...]
    m = mod_ref[0]
    h2 = _norm_mod(x, g2_ref[...], _mod_part(m, 4), _mod_part(m, 3))
    f = _dot(h2, w1_ref[...])
    x2 = x + _mod_part(m, 5) * _dot(_silu(f[:, :D_FF]) * f[:, D_FF:], w2_ref[...])
    if final_norm:
        x2 = x2 * _rms(x2) * fg_ref[...]
    o_ref[...] = x2


def _ffn(x2d, mod3, rows_per_mod, tile, w, l, final_g, final_norm):
    mrows = x2d.shape[0]
    r = mod3.shape[1]
    per = rows_per_mod // tile
    row_spec = pl.BlockSpec((tile, D_MODEL), lambda i: (i, 0))
    return pl.pallas_call(
        functools.partial(_ffn_kernel, final_norm=final_norm),
        grid=(mrows // tile,),
        in_specs=[
            row_spec,
            pl.BlockSpec((1, r, N_MOD), lambda i: (i // per, 0, 0)),
            _wspec(l, (1, D_MODEL)),
            _wspec(l, (D_MODEL, 2 * D_FF)),
            _wspec(l, (D_FF, D_MODEL)),
            _full_spec((1, D_MODEL)),
        ],
        out_specs=row_spec,
        out_shape=jax.ShapeDtypeStruct((mrows, D_MODEL), F32),
        compiler_params=_cparams(("arbitrary",)),
        name="ffn_final" if final_norm else "ffn",
    )(x2d, mod3, w["g2"], w["ffn_w1"], w["ffn_w2"], final_g)


def _sample_lru_kernel(x_ref, mod_ref, g1_ref, win_ref, cw_ref, cb_ref, wg_ref, ba_ref, bx_ref, lam_ref, pa_ref,
                       conv0_ref, h0_ref, ya_ref, conv_ref, lru_ref):
    ns = h0_ref.shape[0]
    nt = x_ref.shape[0] // ns
    m = mod_ref[...]
    tile = lambda v: jnp.concatenate([v] * nt, axis=0)
    h = _norm_mod(x_ref[...], g1_ref[...], tile(_mod_part(m, 1)), tile(_mod_part(m, 0)))
    u = _dot(h, win_ref[...])
    xa = [conv0_ref[i] for i in range(CONV_W - 1)] + [u[i * ns:(i + 1) * ns, :D_RNN] for i in range(nt)]
    cw = cw_ref[...]
    xcs = []
    for ti in range(nt):
        acc = cb_ref[...]
        for i in range(CONV_W):
            acc = acc + xa[ti + i] * cw[i:i + 1, :]
        xcs.append(acc)
    a, bb = _lru_coeffs(jnp.concatenate(xcs, axis=0), wg_ref, ba_ref[...], bx_ref[...], lam_ref[...])
    hc = h0_ref[...]
    hs = []
    for ti in range(nt):
        rows = slice(ti * ns, (ti + 1) * ns)
        hc = a[rows, :] * hc + bb[rows, :]
        hs.append(hc)
    ya_ref[...] = _dot(jnp.concatenate(hs, axis=0) * _gelu_tanh(u[:, D_RNN:]), pa_ref[...])
    for i in range(CONV_W - 1):
        conv_ref[i] = xa[nt + i]
    lru_ref[...] = hc


def _sample_lru(x_tm, mod_s, conv0_tm, h0_all, w, l):
    rows = x_tm.shape[0]
    ns = h0_all.shape[1]
    return pl.pallas_call(
        _sample_lru_kernel,
        grid=(1,),
        in_specs=[
            _full_spec((rows, D_MODEL)),
            _full_spec((ns, N_MOD)),
            _wspec(l, (1, D_MODEL)),
            _wspec(l, (D_MODEL, N_IN_A)),
            _wspec(l, (CONV_W, D_RNN)),
            _wspec(l, (1, D_RNN)),
            _wspec(l, (LRU_BLOCKS, LRU_BW, 2 * LRU_BW)),
            _wspec(l, (1, D_RNN)),
            _wspec(l, (1, D_RNN)),
            _wspec(l, (1, D_RNN)),
            _wspec(l, (D_RNN, D_MODEL)),
            _full_spec((CONV_W - 1, ns, D_RNN)),
            _wspec(l, (ns, D_RNN)),
        ],
        out_specs=[
            _full_out((rows, D_MODEL)),
            _full_out((CONV_W - 1, ns, D_RNN)),
            _full_out((ns, D_RNN)),
        ],
        out_shape=[
            jax.ShapeDtypeStruct((rows, D_MODEL), F32),
            jax.ShapeDtypeStruct((CONV_W - 1, ns, D_RNN), F32),
            jax.ShapeDtypeStruct((ns, D_RNN), F32),
        ],
        compiler_params=_cparams(("arbitrary",)),
        name="sample_lru",
    )(x_tm, mod_s, w["g1"], w["win"], w["conv_w"], w["conv_b"], w["wgate"], w["lru_ba"], w["lru_bx"],
      w["lru_lambda"], w["proj_a"], conv0_tm, h0_all)


def _sample_inproj_kernel(x_ref, mod_ref, g1_ref, win_ref, u_ref):
    nt = x_ref.shape[0] // mod_ref.shape[0]
    m = mod_ref[...]
    tile = lambda v: jnp.concatenate([v] * nt, axis=0)
    h = _norm_mod(x_ref[...], g1_ref[...], tile(_mod_part(m, 1)), tile(_mod_part(m, 0)))
    u_ref[...] = _dot(h, win_ref[...])


def _sample_inproj(x_tm, mod_s, w, l):
    rows = x_tm.shape[0]
    return pl.pallas_call(
        _sample_inproj_kernel,
        grid=(1,),
        in_specs=[
            _full_spec((rows, D_MODEL)),
            _full_spec(mod_s.shape),
            _wspec(l, (1, D_MODEL)),
            _wspec(l, (D_MODEL, N_IN_B)),
        ],
        out_specs=_full_out((rows, N_IN_B)),
        out_shape=jax.ShapeDtypeStruct((rows, N_IN_B), F32),
        compiler_params=_cparams(("arbitrary",)),
        name="sample_inproj",
    )(x_tm, mod_s, w["g1"], w["win_b"])


def _sample_gla_kernel(u_ref, wa2_ref, gba_ref, gng_ref, s0_ref, *rest, nt, carries_buffer):
    yb_ref, st_ref, o_s, q_s, v_s, kdt_s, bl_s = rest[1:] if carries_buffer else rest
    rows = u_ref.shape[0]
    steps_per_rows = rows // (SAMPLE_STATE_BLOCK * nt)
    sub = pl.program_id(0) % steps_per_rows
    rg = SAMPLE_ROW_GROUP
    seq_per_group = rg // nt

    @pl.when(sub == 0)
    def _():
        u = u_ref[...]
        g = _gla_gate(u, wa2_ref, gba_ref[...])
        bc = _group_cumsum(g, nt)
        rin = lax.broadcasted_iota(jnp.int32, g.shape, 0) % nt
        sfx = jnp.zeros_like(g)
        for k in range(1, nt):
            sfx = sfx + jnp.where(rin + k < nt, pltpu.roll(g, rows - k, 0), 0.0)
        qt = u[:, _QO:_QO + GLA_DK_TOTAL] * (jnp.exp(bc) * (GLA_DK ** -0.5))
        uk = u[:, _KO:_KO + GLA_DK_TOTAL]
        kt = uk * jnp.exp(-bc)
        kd = uk * jnp.exp(sfx)
        q_s[...] = qt.astype(BF16)
        v_s[...] = u[:, _VO:_VO + GLA_DV_TOTAL].astype(BF16)
        bl_s[...] = bc + sfx
        ri = lax.broadcasted_iota(jnp.int32, (rows, rows), 0)
        ci = lax.broadcasted_iota(jnp.int32, (rows, rows), 1)
        same_seq_causal = (ri // nt == ci // nt) & (ri >= ci)
        for hd in range(GLA_HEADS):
            ks = slice(hd * GLA_DK, (hd + 1) * GLA_DK)
            att = jnp.where(same_seq_causal, _dot_nt(qt[:, ks], kt[:, ks]), 0.0)
            o_s[:, hd * GLA_DV:(hd + 1) * GLA_DV] = _dot(att, v_s[:, hd * GLA_DV:(hd + 1) * GLA_DV])
            kdt_s[hd] = kd[:, ks].T

    lane_seq = lax.broadcasted_iota(jnp.int32, (GLA_DK, rows), 1) // nt
    row_seq = lax.broadcasted_iota(jnp.int32, (rg, GLA_DV), 0) // nt

    def group(gi, carry):
        r0 = pl.multiple_of((sub * (SAMPLE_STATE_BLOCK // seq_per_group) + gi) * rg, rg)
        dec_g = jnp.exp(bl_s[pl.ds(r0, rg), :])
        for hd in range(GLA_HEADS):
            ks = slice(hd * GLA_DK, (hd + 1) * GLA_DK)
            vsl = slice(hd * GLA_DV, (hd + 1) * GLA_DV)
            qg = q_s[pl.ds(r0, rg), ks]
            vh = v_s[:, vsl]
            inter = jnp.zeros((rg, GLA_DV), F32)
            for j in range(seq_per_group):
                s_loc = gi * seq_per_group + j
                s_row = sub * SAMPLE_STATE_BLOCK + s_loc
                s_old = s0_ref[s_loc, hd]
                inter = jnp.where(row_seq == j, _dot(qg, s_old), inter)
                kdt = jnp.where(lane_seq == s_row, kdt_s[hd], 0.0)
                dm = _col_bcast(dec_g[j * nt:j * nt + 1, ks])
                st_ref[s_loc, hd] = s_old * jnp.concatenate([dm, dm], axis=1) + _dot(kdt, vh)
            o_s[pl.ds(r0, rg), vsl] = o_s[pl.ds(r0, rg), vsl] + inter
        return carry

    lax.fori_loop(0, SAMPLE_STATE_BLOCK // seq_per_group, group, 0)

    @pl.when(sub == steps_per_rows - 1)
    def _():
        yb_ref[...] = _head_norm_gate(o_s[...], gng_ref[...], u_ref[:, _RO:_RO + D_MODEL])


def _sample_gla(u_sm, state_all, new_state_all, w, l, nt):
    rows = u_sm.shape[0]
    br = SAMPLE_ROW_BLOCK
    sb = SAMPLE_STATE_BLOCK
    per = br // (sb * nt)
    state_spec = pl.BlockSpec((None, sb, GLA_HEADS, GLA_DK, GLA_DV), lambda i: (l, i, 0, 0, 0))
    in_specs = [
        pl.BlockSpec((br, N_IN_B), lambda i: (i // per, 0)),
        _wspec(l, (LANES, GLA_DK_TOTAL)),
        _wspec(l, (1, GLA_DK_TOTAL)),
        _wspec(l, (1, GLA_DV_TOTAL)),
        state_spec,
    ]
    args = [u_sm, w["wa2"], w["gla_ba"], w["gla_norm_g"], state_all]
    aliases = {}
    if new_state_all is not None:
        in_specs.append(pl.BlockSpec(memory_space=pl.ANY))
        args.append(new_state_all)
        aliases = {len(args) - 1: 1}
    return pl.pallas_call(
        functools.partial(_sample_gla_kernel, nt=nt, carries_buffer=new_state_all is not None),
        grid=(rows // (sb * nt),),
        in_specs=in_specs,
        out_specs=[
            pl.BlockSpec((br, D_MODEL), lambda i: (i // per, 0)),
            state_spec,
        ],
        out_shape=[
            jax.ShapeDtypeStruct((rows, D_MODEL), F32),
            jax.ShapeDtypeStruct(state_all.shape, F32),
        ],
        input_output_aliases=aliases,
        scratch_shapes=[
            pltpu.VMEM((br, GLA_DV_TOTAL), F32),
            pltpu.VMEM((br, GLA_DK_TOTAL), BF16),
            pltpu.VMEM((br, GLA_DV_TOTAL), BF16),
            pltpu.VMEM((GLA_HEADS, GLA_DK, br), F32),
            pltpu.VMEM((br, GLA_DK_TOTAL), F32),
        ],
        compiler_params=_cparams(("arbitrary",)),
        name="sample_gla",
    )(*args)


def _sample_merge_kernel(x_ref, mod_ref, u_ref, ya_ref, yb_ref, pb_ref, wo_ref, x1_ref):
    nt = x_ref.shape[0] // mod_ref.shape[0]
    gt1 = jnp.concatenate([_mod_part(mod_ref[...], 2)] * nt, axis=0)
    x1_ref[...] = _merge_out(x_ref[...], gt1, u_ref[...], ya_ref[...], yb_ref[...], pb_ref, wo_ref)


def _sample_merge(x_tm, mod_s, u_tm, ya, yb_tm, w, l):
    rows = x_tm.shape[0]
    return pl.pallas_call(
        _sample_merge_kernel,
        grid=(1,),
        in_specs=[
            _full_spec((rows, D_MODEL)),
            _full_spec(mod_s.shape),
            _full_spec((rows, N_IN_B)),
            _full_spec((rows, D_MODEL)),
            _full_spec((rows, D_MODEL)),
            _wspec(l, (GLA_DV_TOTAL, D_MODEL)),
            _wspec(l, (D_MODEL, D_MODEL)),
        ],
        out_specs=_full_out((rows, D_MODEL)),
        out_shape=jax.ShapeDtypeStruct((rows, D_MODEL), F32),
        compiler_params=_cparams(("arbitrary",)),
        name="sample_merge",
    )(x_tm, mod_s, u_tm, ya, yb_tm, w["proj_b"], w["w_out"])


def _stacked_weights(norm1_g, norm2_g, w_in, conv_w, conv_b, lru_wa, lru_ba, lru_wx, lru_bx, lru_lambda,
                     gla_wa2, gla_ba, gla_norm_g, proj_a, proj_b, w_out, ffn_w1, ffn_w2):
    depth = w_in.shape[0]
    o_q = 2 * D_RNN
    o_lr = o_q + 2 * GLA_DK_TOTAL + 2 * GLA_DV_TOTAL
    o_ga = o_lr + GLA_RANK
    win = w_in.astype(BF16)
    win_b = jnp.concatenate(
        [win[:, :, o_lr:o_ga], jnp.zeros((depth, D_MODEL, LANES - GLA_RANK), BF16),
         win[:, :, o_q:o_lr], win[:, :, o_ga:]], axis=2)
    row = lambda v: v.reshape(depth, 1, -1)
    return {
        "g1": row(norm1_g), "g2": row(norm2_g),
        "win": win, "win_b": win_b,
        "conv_w": conv_w, "conv_b": row(conv_b),
        "wgate": jnp.concatenate([lru_wa, lru_wx], axis=-1).astype(BF16),
        "lru_ba": row(lru_ba), "lru_bx": row(lru_bx), "lru_lambda": row(lru_lambda),
        "wa2": jnp.pad(gla_wa2, ((0, 0), (0, LANES - GLA_RANK), (0, 0))).astype(BF16),
        "gla_ba": row(gla_ba), "gla_norm_g": row(gla_norm_g),
        "proj_a": proj_a.astype(BF16), "proj_b": proj_b.astype(BF16), "w_out": w_out.astype(BF16),
        "ffn_w1": ffn_w1.astype(BF16), "ffn_w2": ffn_w2.astype(BF16),
    }


def kernel(x_prompt, x_sample, c_prompt, c_sample, state_conv, state_lru, state_gla, norm1_g, norm2_g, ada_w, ada_b,
           w_in, conv_w, conv_b, lru_wa, lru_ba, lru_wx, lru_bx, lru_lambda, gla_wa2, gla_ba, gla_norm_g, proj_a,
           proj_b, w_out, ffn_w1, ffn_w2, final_g):
    bp, tp, _ = x_prompt.shape
    bs, ts, _ = x_sample.shape
    depth = w_in.shape[0]
    mod = _modulation(jnp.concatenate([c_prompt, c_sample], axis=0), ada_w, ada_b)
    w = _stacked_weights(norm1_g, norm2_g, w_in, conv_w, conv_b, lru_wa, lru_ba, lru_wx, lru_bx, lru_lambda,
                         gla_wa2, gla_ba, gla_norm_g, proj_a, proj_b, w_out, ffn_w1, ffn_w2)
    fg = final_g.reshape(1, D_MODEL)
    xp = x_prompt
    xs = x_sample.transpose(1, 0, 2).reshape(ts * bs, D_MODEL)
    conv_p, lru_p, gla_p, conv_s, lru_s = [], [], [], [], []
    gla_s = None
    for l in range(depth):
        last = l == depth - 1
        mod_p = mod[l, :bp].reshape(bp, 1, N_MOD)
        mod_s = mod[l, bp:]
        x1, cb, ht, st = _prompt_mixer(xp, mod_p, w, l)
        xp = _ffn(x1.reshape(bp * tp, D_MODEL), mod_p, tp, FFN_TILE, w, l, fg, last).reshape(bp, tp, D_MODEL)
        conv_p.append(cb)
        lru_p.append(ht.reshape(bp, D_RNN))
        gla_p.append(st)
        ya_s, cb_s, ht_s = _sample_lru(xs, mod_s, state_conv[l].transpose(1, 0, 2), state_lru, w, l)
        u_tm = _sample_inproj(xs, mod_s, w, l)
        u_sm = u_tm.reshape(ts, bs, N_IN_B).transpose(1, 0, 2).reshape(bs * ts, N_IN_B)
        yb_sm, gla_s = _sample_gla(u_sm, state_gla, gla_s, w, l, ts)
        yb_tm = yb_sm.reshape(bs, ts, D_MODEL).transpose(1, 0, 2).reshape(ts * bs, D_MODEL)
        x1_s = _sample_merge(xs, mod_s, u_tm, ya_s, yb_tm, w, l)
        xs = _ffn(x1_s, mod_s.reshape(1, bs, N_MOD), bs * ts, bs, w, l, fg, last)
        conv_s.append(cb_s.transpose(1, 0, 2))
        lru_s.append(ht_s)
    y_sample = xs.reshape(ts, bs, D_MODEL).transpose(1, 0, 2)
    return (xp, y_sample, jnp.stack(conv_p), jnp.stack(lru_p), jnp.stack(gla_p),
            jnp.stack(conv_s), jnp.stack(lru_s), gla_s)
```

```python
import functools

import jax
import jax.numpy as jnp
from jax import lax
from jax.experimental import pallas as pl
from jax.experimental.pallas import tpu as pltpu

F32 = jnp.float32
BF16 = jnp.bfloat16

D_MODEL = 1024
D_RNN = 1280
LRU_BW = 128
LRU_BLOCKS = D_RNN // LRU_BW
CONV_W = 4
LRU_C = 8.0
GLA_HEADS = 4
GLA_DK = 128
GLA_DV = 256
GLA_DK_TOTAL = GLA_HEADS * GLA_DK
GLA_DV_TOTAL = GLA_HEADS * GLA_DV
GLA_RANK = 16
GLA_TAU = 16.0
D_FF = 2816
EPS = 1e-6
TINY_F32 = 1.1754944e-38
N_MOD = 6 * D_MODEL

LANES = 128
SUBLANES = 8
VMEM_LIMIT_BYTES = 60 * 1024 * 1024

_LRO, _QO, _KO, _VO, _RO, _GAO, _GBO = 0, 128, 640, 1152, 2176, 3200, 4224
N_IN_B = _GBO + D_MODEL
N_IN_A = 2 * D_RNN

PROMPT_TILE = 256
GLA_CHUNK = 128
FFN_TILE = 512
VRG_PIECE = 512
SAMPLE_ROW_BLOCK = 128
SAMPLE_STATE_BLOCK = 8
SAMPLE_ROW_GROUP = 16


def _cparams(sem):
    return pltpu.CompilerParams(dimension_semantics=sem, vmem_limit_bytes=VMEM_LIMIT_BYTES)


def _full_spec(shape):
    n = len(shape)
    return pl.BlockSpec(shape, lambda *_: (0,) * n, pipeline_mode=pl.Buffered(1))


def _full_out(shape):
    n = len(shape)
    return pl.BlockSpec(shape, lambda *_: (0,) * n)


def _wspec(l, shape):
    n = len(shape)
    return pl.BlockSpec((None,) + tuple(shape), lambda *_: (l,) + (0,) * n, pipeline_mode=pl.Buffered(1))


def _softplus(y):
    return jnp.maximum(y, 0.0) + jnp.log1p(jnp.exp(-jnp.abs(y)))


def _sigmoid(y):
    return 0.5 * jnp.tanh(0.5 * y) + 0.5


def _silu(y):
    return y * _sigmoid(y)


def _sqrt_nonneg(s):
    return s * lax.rsqrt(jnp.maximum(s, TINY_F32))


def _gelu_tanh(y):
    return 0.5 * y * (1.0 + jnp.tanh(0.7978845608028654 * (y + 0.044715 * (y * y * y))))


def _rms(x):
    return lax.rsqrt(jnp.mean(x * x, axis=-1, keepdims=True) + EPS)


def _norm_mod(x, g, scale, shift):
    return x * _rms(x) * g * (1.0 + scale) + shift


def _mod_part(m, i):
    return m[:, i * D_MODEL:(i + 1) * D_MODEL]


def _dot(a, b):
    return jnp.dot(a.astype(BF16), b.astype(BF16), preferred_element_type=F32)


def _dot_nt(a, b):
    return lax.dot_general(a.astype(BF16), b.astype(BF16), (((1,), (1,)), ((), ())),
                           preferred_element_type=F32)


def _lru_coeffs_block(xb, wg, ba, bx, lam):
    lamc = -LRU_C * _softplus(-lam)
    pre = _dot(xb, wg)
    r = _sigmoid(pre[:, :LRU_BW] + ba)
    i = _sigmoid(pre[:, LRU_BW:] + bx)
    a = jnp.exp(lamc * r)
    return a, _sqrt_nonneg(1.0 - a * a) * (i * xb)


def _gla_gate(ulr, wa2_ref, gba):
    z = _dot(ulr, wa2_ref[...]) + gba
    return (jnp.minimum(z, 0.0) - jnp.log1p(jnp.exp(-jnp.abs(z)))) * (1.0 / GLA_TAU)


def _group_cumsum(g, group):
    rin = lax.broadcasted_iota(jnp.int32, g.shape, 0) % group
    x = g
    k = 1
    while k < group:
        x = x + jnp.where(rin >= k, pltpu.roll(x, k, 0), 0.0)
        k *= 2
    return x


def _col_bcast(row):
    return jnp.broadcast_to(row, (LANES, LANES)).T


def _head_norm_gate(o, gng, ur):
    parts = []
    for hd in range(GLA_HEADS):
        sl = slice(hd * GLA_DV, (hd + 1) * GLA_DV)
        oh = o[:, sl]
        parts.append(oh * _rms(oh) * gng[:, sl])
    return jnp.concatenate(parts, axis=1) * _silu(ur)


def _merge_out(x, gt1, uga, ugb, ya, yb, pb_ref, wo_ref):
    pbv = _dot(yb, pb_ref[...])
    mm = _sigmoid(uga) * ya + _sigmoid(ugb) * pbv
    return x + gt1 * _dot(mm, wo_ref[...])


def _mod_kernel(c_ref, w_ref, b_ref, o_ref):
    o_ref[0] = _dot(_silu(c_ref[...]), w_ref[0]) + b_ref[0]


def _modulation(c_all, ada_w, ada_b):
    depth = ada_w.shape[0]
    rows = c_all.shape[0]
    nblk = N_MOD // D_MODEL
    return pl.pallas_call(
        _mod_kernel,
        grid=(depth, nblk),
        in_specs=[
            pl.BlockSpec((rows, D_MODEL), lambda l, j: (0, 0)),
            pl.BlockSpec((1, D_MODEL, D_MODEL), lambda l, j: (l, 0, j)),
            pl.BlockSpec((1, 1, D_MODEL), lambda l, j: (l, 0, j)),
        ],
        out_specs=pl.BlockSpec((1, rows, D_MODEL), lambda l, j: (l, 0, j)),
        out_shape=jax.ShapeDtypeStruct((depth, rows, N_MOD), F32),
        compiler_params=_cparams(("arbitrary", "arbitrary")),
        name="modulation",
    )(c_all, ada_w, ada_b.reshape(depth, 1, N_MOD))


def _shift_rows_in(blk, first_row):
    top = lax.broadcasted_iota(jnp.int32, blk.shape, 0) == 0
    return jnp.where(top, first_row, pltpu.roll(blk, 1, 0))


def _regroup_matrices(tt):
    ng = tt // SUBLANES
    ri = lax.broadcasted_iota(jnp.int32, (tt, tt), 0)
    ci = lax.broadcasted_iota(jnp.int32, (tt, tt), 1)
    to_blocks = jnp.where(ci == SUBLANES * (ri % ng) + ri // ng, 1.0, 0.0).astype(BF16)
    from_blocks = jnp.where(ci == ng * (ri % SUBLANES) + ri // SUBLANES, 1.0, 0.0).astype(BF16)
    return to_blocks, from_blocks


def _lru_conv(ux, cw, cb, conv_rows):
    ng = ux.shape[0] // SUBLANES
    first_tail = SUBLANES - (CONV_W - 1)
    xs = [ux[j * ng:(j + 1) * ng, :] for j in range(SUBLANES)]
    prev = {k: _shift_rows_in(xs[k], conv_rows[k - first_tail:k - first_tail + 1, :])
            for k in range(first_tail, SUBLANES)}
    xcs = []
    for j in range(SUBLANES):
        acc = cb
        for i in range(CONV_W):
            d = CONV_W - 1 - i
            src = xs[j - d] if j >= d else prev[j - d + SUBLANES]
            acc = acc + src * cw[i:i + 1, :]
        xcs.append(acc)
    tail = jnp.concatenate([xs[k][ng - 1:ng, :] for k in range(first_tail, SUBLANES)], axis=0)
    return jnp.concatenate(xcs, axis=0), tail


def _lru_scan(a, bb, carry):
    ng = a.shape[0] // SUBLANES
    loc = [bb[0:ng, :]]
    cum = [a[0:ng, :]]
    for j in range(1, SUBLANES):
        aj = a[j * ng:(j + 1) * ng, :]
        loc.append(aj * loc[-1] + bb[j * ng:(j + 1) * ng, :])
        cum.append(aj * cum[-1])
    ga, gb = cum[-1], loc[-1]
    rowg = lax.broadcasted_iota(jnp.int32, ga.shape, 0)
    k = 1
    while k < ng:
        ga_sh = jnp.where(rowg >= k, pltpu.roll(ga, k, 0), 1.0)
        gb_sh = jnp.where(rowg >= k, pltpu.roll(gb, k, 0), 0.0)
        gb = ga * gb_sh + gb
        ga = ga * ga_sh
        k *= 2
    leaving = gb + ga * carry
    entering = _shift_rows_in(leaving, carry)
    hs = jnp.concatenate([loc[j] + cum[j] * entering for j in range(SUBLANES)], axis=0)
    return hs, leaving[ng - 1:ng, :]


def _lru_lane_block(n, uan, cw_ref, cb_ref, wg_ref, ba_ref, bx_ref, lam_ref, conv_s, hc_s):
    sl = slice(n * LRU_BW, (n + 1) * LRU_BW)
    xc, tail = _lru_conv(uan[:, :LRU_BW], cw_ref[:, sl], cb_ref[:, sl], conv_s[0:CONV_W - 1, sl])
    conv_s[0:CONV_W - 1, sl] = tail
    a, bb = _lru_coeffs_block(xc, wg_ref[n], ba_ref[:, sl], bx_ref[:, sl], lam_ref[:, sl])
    hs, last = _lru_scan(a, bb, hc_s[0:1, sl])
    hc_s[0:1, sl] = last
    return (hs * _gelu_tanh(uan[:, LRU_BW:])).astype(BF16)


def _gla_prep(ulr, uq, uk, wa2_ref, gba):
    g = _gla_gate(ulr, wa2_ref, gba)
    bc = _group_cumsum(g, GLA_CHUNK)
    qt = uq * (jnp.exp(bc) * (GLA_DK ** -0.5))
    kt = uk * jnp.exp(-bc)
    return bc, qt, kt


def _gla_chunks(bc, qt, kt, uk, uv, s_s):
    tt = bc.shape[0]
    c = GLA_CHUNK
    ri = lax.broadcasted_iota(jnp.int32, (c, c), 0)
    ci = lax.broadcasted_iota(jnp.int32, (c, c), 1)
    causal = ri >= ci
    o_rows = []
    for ch in range(tt // c):
        rows = slice(ch * c, (ch + 1) * c)
        bl = bc[ch * c + c - 1:ch * c + c, :]
        kd = uk[rows, :] * jnp.exp(bl - bc[rows, :])
        dec = jnp.exp(bl)
        o_heads = []
        for hd in range(GLA_HEADS):
            ks = slice(hd * GLA_DK, (hd + 1) * GLA_DK)
            qh = qt[rows, ks].astype(BF16)
            vh = uv[rows, hd * GLA_DV:(hd + 1) * GLA_DV].astype(BF16)
            att = jnp.where(causal, _dot_nt(qh, kt[rows, ks]), 0.0)
            s_old = s_s[hd]
            o_heads.append(_dot(att, vh) + _dot(qh, s_old))
            dm = _col_bcast(dec[:, ks])
            s_s[hd] = s_old * jnp.concatenate([dm, dm], axis=1) + _dot(kd[:, ks].T, vh)
        o_rows.append(jnp.concatenate(o_heads, axis=1))
    return jnp.concatenate(o_rows, axis=0)


def _prompt_mixer_kernel(x_ref, mod_ref, g1_ref, wina_ref, winb_ref, cw_ref, cb_ref, wg_ref, ba_ref, bx_ref, lam_ref,
                         pa_ref, wa2_ref, gba_ref, gng_ref, pb_ref, wo_ref,
                         x1_ref, conv_ref, lru_ref, st_ref, conv_s, hc_s, s_s):
    t = pl.program_id(1)

    @pl.when(t == 0)
    def _():
        conv_s[...] = jnp.zeros_like(conv_s)
        hc_s[...] = jnp.zeros_like(hc_s)
        s_s[...] = jnp.zeros_like(s_s)

    x = x_ref[0]
    m = mod_ref[0]
    hb16 = _norm_mod(x, g1_ref[...], _mod_part(m, 1), _mod_part(m, 0)).astype(BF16)
    to_blocks, from_blocks = _regroup_matrices(x.shape[0])
    u_gqk = jnp.dot(hb16, winb_ref[:, :_VO], preferred_element_type=F32)
    hb = jnp.dot(to_blocks, hb16, preferred_element_type=F32).astype(BF16)
    uk = u_gqk[:, _KO:_KO + GLA_DK_TOTAL]
    bc, qt, kt = _gla_prep(u_gqk[:, _LRO:_LRO + LANES], u_gqk[:, _QO:_QO + GLA_DK_TOTAL], uk, wa2_ref, gba_ref[...])
    n_piece = (N_IN_B - _VO) // VRG_PIECE
    vrg, ya_blocks = [], []
    for n in range(LRU_BLOCKS):
        uan = jnp.dot(hb, wina_ref[:, 2 * n * LRU_BW:2 * (n + 1) * LRU_BW], preferred_element_type=F32)
        if n < n_piece:
            c0 = _VO + n * VRG_PIECE
            vrg.append(jnp.dot(hb16, winb_ref[:, c0:c0 + VRG_PIECE], preferred_element_type=F32))
        ya_blocks.append(_lru_lane_block(n, uan, cw_ref, cb_ref, wg_ref, ba_ref, bx_ref, lam_ref, conv_s, hc_s))
    u_vrg = jnp.concatenate(vrg, axis=1)
    o = _gla_chunks(bc, qt, kt, uk, u_vrg[:, :GLA_DV_TOTAL], s_s)
    ya = jnp.dot(from_blocks, jnp.concatenate(ya_blocks, axis=1), preferred_element_type=F32).astype(BF16)
    ya = jnp.dot(ya, pa_ref[...], preferred_element_type=F32)
    yb = _head_norm_gate(o, gng_ref[...], u_vrg[:, _RO - _VO:_RO - _VO + D_MODEL])
    x1_ref[0] = _merge_out(x, _mod_part(m, 2), u_vrg[:, _GAO - _VO:_GAO - _VO + D_MODEL],
                           u_vrg[:, _GBO - _VO:_GBO - _VO + D_MODEL], ya, yb, pb_ref, wo_ref)

    @pl.when(t == pl.num_programs(1) - 1)
    def _():
        conv_ref[0] = conv_s[0:CONV_W - 1, :]
        lru_ref[0] = hc_s[0:1, :]
        st_ref[0] = s_s[...]


def _prompt_mixer(x, mod_p, w, l):
    b, t, _ = x.shape
    tt = PROMPT_TILE
    seq_spec = pl.BlockSpec((1, tt, D_MODEL), lambda i, j: (i, j, 0))
    return pl.pallas_call(
        _prompt_mixer_kernel,
        grid=(b, t // tt),
        in_specs=[
            seq_spec,
            pl.BlockSpec((1, 1, N_MOD), lambda i, j: (i, 0, 0)),
            _wspec(l, (1, D_MODEL)),
            _wspec(l, (D_MODEL, N_IN_A)),
            _wspec(l, (D_MODEL, N_IN_B)),
            _wspec(l, (CONV_W, D_RNN)),
            _wspec(l, (1, D_RNN)),
            _wspec(l, (LRU_BLOCKS, LRU_BW, 2 * LRU_BW)),
            _wspec(l, (1, D_RNN)),
            _wspec(l, (1, D_RNN)),
            _wspec(l, (1, D_RNN)),
            _wspec(l, (D_RNN, D_MODEL)),
            _wspec(l, (LANES, GLA_DK_TOTAL)),
            _wspec(l, (1, GLA_DK_TOTAL)),
            _wspec(l, (1, GLA_DV_TOTAL)),
            _wspec(l, (GLA_DV_TOTAL, D_MODEL)),
            _wspec(l, (D_MODEL, D_MODEL)),
        ],
        out_specs=[
            seq_spec,
            pl.BlockSpec((1, CONV_W - 1, D_RNN), lambda i, j: (i, 0, 0)),
            pl.BlockSpec((1, 1, D_RNN), lambda i, j: (i, 0, 0)),
            pl.BlockSpec((1, GLA_HEADS, GLA_DK, GLA_DV), lambda i, j: (i, 0, 0, 0)),
        ],
        out_shape=[
            jax.ShapeDtypeStruct((b, t, D_MODEL), F32),
            jax.ShapeDtypeStruct((b, CONV_W - 1, D_RNN), F32),
            jax.ShapeDtypeStruct((b, 1, D_RNN), F32),
            jax.ShapeDtypeStruct((b, GLA_HEADS, GLA_DK, GLA_DV), F32),
        ],
        scratch_shapes=[
            pltpu.VMEM((SUBLANES, D_RNN), F32),
            pltpu.VMEM((SUBLANES, D_RNN), F32),
            pltpu.VMEM((GLA_HEADS, GLA_DK, GLA_DV), F32),
        ],
        compiler_params=_cparams(("arbitrary", "arbitrary")),
        name="prompt_mixer",
    )(x, mod_p, w["g1"], w["win_a"], w["win_b"], w["conv_w"], w["conv_b"], w["wgate"], w["lru_ba"], w["lru_bx"],
      w["lru_lambda"], w["proj_a"], w["wa2"], w["gla_ba"], w["gla_norm_g"], w["proj_b"], w["w_out"])


def _ffn_kernel(x_ref, mod_ref, g2_ref, w1_ref, w2_ref, fg_ref, o_ref, *, final_norm):
    x = x_ref[...]
    m = mod_ref[0]
    h2 = _norm_mod(x, g2_ref[...], _mod_part(m, 4), _mod_part(m, 3))
    f = _dot(h2, w1_ref[...])
    x2 = x + _mod_part(m, 5) * _dot(_silu(f[:, :D_FF]) * f[:, D_FF:], w2_ref[...])
    if final_norm:
        x2 = x2 * _rms(x2) * fg_ref[...]
    o_ref[...] = x2


def _ffn(x2d, mod3, rows_per_mod, tile, w, l, final_g, final_norm):
    mrows = x2d.shape[0]
    r = mod3.shape[1]
    per = rows_per_mod // tile
    row_spec = pl.BlockSpec((tile, D_MODEL), lambda i: (i, 0))
    return pl.pallas_call(
        functools.partial(_ffn_kernel, final_norm=final_norm),
        grid=(mrows // tile,),
        in_specs=[
            row_spec,
            pl.BlockSpec((1, r, N_MOD), lambda i: (i // per, 0, 0)),
            _wspec(l, (1, D_MODEL)),
            _wspec(l, (D_MODEL, 2 * D_FF)),
            _wspec(l, (D_FF, D_MODEL)),
            _full_spec((1, D_MODEL)),
        ],
        out_specs=row_spec,
        out_shape=jax.ShapeDtypeStruct((mrows, D_MODEL), F32),
        compiler_params=_cparams(("arbitrary",)),
        name="ffn_final" if final_norm else "ffn",
    )(x2d, mod3, w["g2"], w["ffn_w1"], w["ffn_w2"], final_g)


def _sample_lru_kernel(x_ref, mod_ref, g1_ref, win_ref, cw_ref, cb_ref, wg_ref, ba_ref, bx_ref, lam_ref, pa_ref,
                       conv0_ref, h0_ref, ya_ref, conv_ref, lru_ref):
    ns = h0_ref.shape[0]
    nt = x_ref.shape[0] // ns
    m = mod_ref[...]
    tile = lambda v: jnp.concatenate([v] * nt, axis=0)
    h = _norm_mod(x_ref[...], g1_ref[...], tile(_mod_part(m, 1)), tile(_mod_part(m, 0)))
    u = _dot(h, win_ref[...])
    ux = jnp.concatenate([u[:, 2 * n * LRU_BW:(2 * n + 1) * LRU_BW] for n in range(LRU_BLOCKS)], axis=1)
    ug = jnp.concatenate([u[:, (2 * n + 1) * LRU_BW:(2 * n + 2) * LRU_BW] for n in range(LRU_BLOCKS)], axis=1)
    xa = [conv0_ref[i] for i in range(CONV_W - 1)] + [ux[i * ns:(i + 1) * ns, :] for i in range(nt)]
    cw = cw_ref[...]
    xcs = []
    for ti in range(nt):
        acc = cb_ref[...]
        for i in range(CONV_W):
            acc = acc + xa[ti + i] * cw[i:i + 1, :]
        xcs.append(acc)
    xc = jnp.concatenate(xcs, axis=0)
    coeffs = [_lru_coeffs_block(xc[:, n * LRU_BW:(n + 1) * LRU_BW], wg_ref[n], ba_ref[:, n * LRU_BW:(n + 1) * LRU_BW],
                                bx_ref[:, n * LRU_BW:(n + 1) * LRU_BW], lam_ref[:, n * LRU_BW:(n + 1) * LRU_BW])
              for n in range(LRU_BLOCKS)]
    a = jnp.concatenate([c[0] for c in coeffs], axis=1)
    bb = jnp.concatenate([c[1] for c in coeffs], axis=1)
    hc = h0_ref[...]
    hs = []
    for ti in range(nt):
        rows = slice(ti * ns, (ti + 1) * ns)
        hc = a[rows, :] * hc + bb[rows, :]
        hs.append(hc)
    ya_ref[...] = _dot(jnp.concatenate(hs, axis=0) * _gelu_tanh(ug), pa_ref[...])
    for i in range(CONV_W - 1):
        conv_ref[i] = xa[nt + i]
    lru_ref[...] = hc


def _sample_lru(x_tm, mod_s, conv0_tm, h0_all, w, l):
    rows = x_tm.shape[0]
    ns = h0_all.shape[1]
    return pl.pallas_call(
        _sample_lru_kernel,
        grid=(1,),
        in_specs=[
            _full_spec((rows, D_MODEL)),
            _full_spec((ns, N_MOD)),
            _wspec(l, (1, D_MODEL)),
            _wspec(l, (D_MODEL, N_IN_A)),
            _wspec(l, (CONV_W, D_RNN)),
            _wspec(l, (1, D_RNN)),
            _wspec(l, (LRU_BLOCKS, LRU_BW, 2 * LRU_BW)),
            _wspec(l, (1, D_RNN)),
            _wspec(l, (1, D_RNN)),
            _wspec(l, (1, D_RNN)),
            _wspec(l, (D_RNN, D_MODEL)),
            _full_spec((CONV_W - 1, ns, D_RNN)),
            _wspec(l, (ns, D_RNN)),
        ],
        out_specs=[
            _full_out((rows, D_MODEL)),
            _full_out((CONV_W - 1, ns, D_RNN)),
            _full_out((ns, D_RNN)),
        ],
        out_shape=[
            jax.ShapeDtypeStruct((rows, D_MODEL), F32),
            jax.ShapeDtypeStruct((CONV_W - 1, ns, D_RNN), F32),
            jax.ShapeDtypeStruct((ns, D_RNN), F32),
        ],
        compiler_params=_cparams(("arbitrary",)),
        name="sample_lru",
    )(x_tm, mod_s, w["g1"], w["win_a"], w["conv_w"], w["conv_b"], w["wgate"], w["lru_ba"], w["lru_bx"],
      w["lru_lambda"], w["proj_a"], conv0_tm, h0_all)


def _sample_inproj_kernel(x_ref, mod_ref, g1_ref, win_ref, u_ref):
    nt = x_ref.shape[0] // mod_ref.shape[0]
    m = mod_ref[...]
    tile = lambda v: jnp.concatenate([v] * nt, axis=0)
    h = _norm_mod(x_ref[...], g1_ref[...], tile(_mod_part(m, 1)), tile(_mod_part(m, 0)))
    u_ref[...] = _dot(h, win_ref[...])


def _sample_inproj(x_tm, mod_s, w, l):
    rows = x_tm.shape[0]
    return pl.pallas_call(
        _sample_inproj_kernel,
        grid=(1,),
        in_specs=[
            _full_spec((rows, D_MODEL)),
            _full_spec(mod_s.shape),
            _wspec(l, (1, D_MODEL)),
            _wspec(l, (D_MODEL, N_IN_B)),
        ],
        out_specs=_full_out((rows, N_IN_B)),
        out_shape=jax.ShapeDtypeStruct((rows, N_IN_B), F32),
        compiler_params=_cparams(("arbitrary",)),
        name="sample_inproj",
    )(x_tm, mod_s, w["g1"], w["win_b"])


def _sample_gla_kernel(u_ref, wa2_ref, gba_ref, gng_ref, s0_ref, *rest, nt, carries_buffer):
    yb_ref, st_ref, o_s, q_s, v_s, kdt_s, bl_s = rest[1:] if carries_buffer else rest
    rows = u_ref.shape[0]
    steps_per_rows = rows // (SAMPLE_STATE_BLOCK * nt)
    sub = pl.program_id(0) % steps_per_rows
    rg = SAMPLE_ROW_GROUP
    seq_per_group = rg // nt

    @pl.when(sub == 0)
    def _():
        u = u_ref[...]
        g = _gla_gate(u[:, _LRO:_LRO + LANES], wa2_ref, gba_ref[...])
        bc = _group_cumsum(g, nt)
        rin = lax.broadcasted_iota(jnp.int32, g.shape, 0) % nt
        sfx = jnp.zeros_like(g)
        for k in range(1, nt):
            sfx = sfx + jnp.where(rin + k < nt, pltpu.roll(g, rows - k, 0), 0.0)
        qt = u[:, _QO:_QO + GLA_DK_TOTAL] * (jnp.exp(bc) * (GLA_DK ** -0.5))
        uk = u[:, _KO:_KO + GLA_DK_TOTAL]
        kt = uk * jnp.exp(-bc)
        kd = uk * jnp.exp(sfx)
        q_s[...] = qt.astype(BF16)
        v_s[...] = u[:, _VO:_VO + GLA_DV_TOTAL].astype(BF16)
        bl_s[...] = bc + sfx
        ri = lax.broadcasted_iota(jnp.int32, (rows, rows), 0)
        ci = lax.broadcasted_iota(jnp.int32, (rows, rows), 1)
        same_seq_causal = (ri // nt == ci // nt) & (ri >= ci)
        for hd in range(GLA_HEADS):
            ks = slice(hd * GLA_DK, (hd + 1) * GLA_DK)
            att = jnp.where(same_seq_causal, _dot_nt(qt[:, ks], kt[:, ks]), 0.0)
            o_s[:, hd * GLA_DV:(hd + 1) * GLA_DV] = _dot(att, v_s[:, hd * GLA_DV:(hd + 1) * GLA_DV])
            kdt_s[hd] = kd[:, ks].T

    lane_seq = lax.broadcasted_iota(jnp.int32, (GLA_DK, rows), 1) // nt
    row_seq = lax.broadcasted_iota(jnp.int32, (rg, GLA_DV), 0) // nt

    def group(gi, carry):
        r0 = pl.multiple_of((sub * (SAMPLE_STATE_BLOCK // seq_per_group) + gi) * rg, rg)
        dec_g = jnp.exp(bl_s[pl.ds(r0, rg), :])
        for hd in range(GLA_HEADS):
            ks = slice(hd * GLA_DK, (hd + 1) * GLA_DK)
            vsl = slice(hd * GLA_DV, (hd + 1) * GLA_DV)
            qg = q_s[pl.ds(r0, rg), ks]
            vh = v_s[:, vsl]
            inter = jnp.zeros((rg, GLA_DV), F32)
            for j in range(seq_per_group):
                s_loc = gi * seq_per_group + j
                s_row = sub * SAMPLE_STATE_BLOCK + s_loc
                s_old = s0_ref[s_loc, hd]
                inter = jnp.where(row_seq == j, _dot(qg, s_old), inter)
                kdt = jnp.where(lane_seq == s_row, kdt_s[hd], 0.0)
                dm = _col_bcast(dec_g[j * nt:j * nt + 1, ks])
                st_ref[s_loc, hd] = s_old * jnp.concatenate([dm, dm], axis=1) + _dot(kdt, vh)
            o_s[pl.ds(r0, rg), vsl] = o_s[pl.ds(r0, rg), vsl] + inter
        return carry

    lax.fori_loop(0, SAMPLE_STATE_BLOCK // seq_per_group, group, 0)

    @pl.when(sub == steps_per_rows - 1)
    def _():
        yb_ref[...] = _head_norm_gate(o_s[...], gng_ref[...], u_ref[:, _RO:_RO + D_MODEL])


def _sample_gla(u_sm, state_all, new_state_all, w, l, nt):
    rows = u_sm.shape[0]
    br = SAMPLE_ROW_BLOCK
    sb = SAMPLE_STATE_BLOCK
    per = br // (sb * nt)
    state_spec = pl.BlockSpec((None, sb, GLA_HEADS, GLA_DK, GLA_DV), lambda i: (l, i, 0, 0, 0))
    in_specs = [
        pl.BlockSpec((br, N_IN_B), lambda i: (i // per, 0)),
        _wspec(l, (LANES, GLA_DK_TOTAL)),
        _wspec(l, (1, GLA_DK_TOTAL)),
        _wspec(l, (1, GLA_DV_TOTAL)),
        state_spec,
    ]
    args = [u_sm, w["wa2"], w["gla_ba"], w["gla_norm_g"], state_all]
    aliases = {}
    if new_state_all is not None:
        in_specs.append(pl.BlockSpec(memory_space=pl.ANY))
        args.append(new_state_all)
        aliases = {len(args) - 1: 1}
    return pl.pallas_call(
        functools.partial(_sample_gla_kernel, nt=nt, carries_buffer=new_state_all is not None),
        grid=(rows // (sb * nt),),
        in_specs=in_specs,
        out_specs=[
            pl.BlockSpec((br, D_MODEL), lambda i: (i // per, 0)),
            state_spec,
        ],
        out_shape=[
            jax.ShapeDtypeStruct((rows, D_MODEL), F32),
            jax.ShapeDtypeStruct(state_all.shape, F32),
        ],
        input_output_aliases=aliases,
        scratch_shapes=[
            pltpu.VMEM((br, GLA_DV_TOTAL), F32),
            pltpu.VMEM((br, GLA_DK_TOTAL), BF16),
            pltpu.VMEM((br, GLA_DV_TOTAL), BF16),
            pltpu.VMEM((GLA_HEADS, GLA_DK, br), F32),
            pltpu.VMEM((br, GLA_DK_TOTAL), F32),
        ],
        compiler_params=_cparams(("arbitrary",)),
        name="sample_gla",
    )(*args)


def _sample_merge_kernel(x_ref, mod_ref, u_ref, ya_ref, yb_ref, pb_ref, wo_ref, x1_ref):
    nt = x_ref.shape[0] // mod_ref.shape[0]
    gt1 = jnp.concatenate([_mod_part(mod_ref[...], 2)] * nt, axis=0)
    x1_ref[...] = _merge_out(x_ref[...], gt1, u_ref[:, _GAO:_GAO + D_MODEL], u_ref[:, _GBO:_GBO + D_MODEL],
                             ya_ref[...], yb_ref[...], pb_ref, wo_ref)


def _sample_merge(x_tm, mod_s, u_tm, ya, yb_tm, w, l):
    rows = x_tm.shape[0]
    return pl.pallas_call(
        _sample_merge_kernel,
        grid=(1,),
        in_specs=[
            _full_spec((rows, D_MODEL)),
            _full_spec(mod_s.shape),
            _full_spec((rows, N_IN_B)),
            _full_spec((rows, D_MODEL)),
            _full_spec((rows, D_MODEL)),
            _wspec(l, (GLA_DV_TOTAL, D_MODEL)),
            _wspec(l, (D_MODEL, D_MODEL)),
        ],
        out_specs=_full_out((rows, D_MODEL)),
        out_shape=jax.ShapeDtypeStruct((rows, D_MODEL), F32),
        compiler_params=_cparams(("arbitrary",)),
        name="sample_merge",
    )(x_tm, mod_s, u_tm, ya, yb_tm, w["proj_b"], w["w_out"])


def _stacked_weights(norm1_g, norm2_g, w_in, conv_w, conv_b, lru_wa, lru_ba, lru_wx, lru_bx, lru_lambda,
                     gla_wa2, gla_ba, gla_norm_g, proj_a, proj_b, w_out, ffn_w1, ffn_w2):
    depth = w_in.shape[0]
    o_q = 2 * D_RNN
    o_lr = o_q + 2 * GLA_DK_TOTAL + 2 * GLA_DV_TOTAL
    o_ga = o_lr + GLA_RANK
    win = w_in.astype(BF16)
    win_b = jnp.concatenate(
        [win[:, :, o_lr:o_ga], jnp.zeros((depth, D_MODEL, LANES - GLA_RANK), BF16),
         win[:, :, o_q:o_lr], win[:, :, o_ga:]], axis=2)
    win_a = jnp.stack([win[:, :, :D_RNN].reshape(depth, D_MODEL, LRU_BLOCKS, LRU_BW),
                       win[:, :, D_RNN:o_q].reshape(depth, D_MODEL, LRU_BLOCKS, LRU_BW)],
                      axis=3).reshape(depth, D_MODEL, N_IN_A)
    row = lambda v: v.reshape(depth, 1, -1)
    return {
        "g1": row(norm1_g), "g2": row(norm2_g),
        "win_a": win_a, "win_b": win_b,
        "conv_w": conv_w, "conv_b": row(conv_b),
        "wgate": jnp.concatenate([lru_wa, lru_wx], axis=-1).astype(BF16),
        "lru_ba": row(lru_ba), "lru_bx": row(lru_bx), "lru_lambda": row(lru_lambda),
        "wa2": jnp.pad(gla_wa2, ((0, 0), (0, LANES - GLA_RANK), (0, 0))).astype(BF16),
        "gla_ba": row(gla_ba), "gla_norm_g": row(gla_norm_g),
        "proj_a": proj_a.astype(BF16), "proj_b": proj_b.astype(BF16), "w_out": w_out.astype(BF16),
        "ffn_w1": ffn_w1.astype(BF16), "ffn_w2": ffn_w2.astype(BF16),
    }


def kernel(x_prompt, x_sample, c_prompt, c_sample, state_conv, state_lru, state_gla, norm1_g, norm2_g, ada_w, ada_b,
           w_in, conv_w, conv_b, lru_wa, lru_ba, lru_wx, lru_bx, lru_lambda, gla_wa2, gla_ba, gla_norm_g, proj_a,
           proj_b, w_out, ffn_w1, ffn_w2, final_g):
    bp, tp, _ = x_prompt.shape
    bs, ts, _ = x_sample.shape
    depth = w_in.shape[0]
    mod = _modulation(jnp.concatenate([c_prompt, c_sample], axis=0), ada_w, ada_b)
    w = _stacked_weights(norm1_g, norm2_g, w_in, conv_w, conv_b, lru_wa, lru_ba, lru_wx, lru_bx, lru_lambda,
                         gla_wa2, gla_ba, gla_norm_g, proj_a, proj_b, w_out, ffn_w1, ffn_w2)
    fg = final_g.reshape(1, D_MODEL)
    xp = x_prompt
    xs = x_sample.transpose(1, 0, 2).reshape(ts * bs, D_MODEL)
    conv_p, lru_p, gla_p, conv_s, lru_s = [], [], [], [], []
    gla_s = None
    for l in range(depth):
        last = l == depth - 1
        mod_p = mod[l, :bp].reshape(bp, 1, N_MOD)
        mod_s = mod[l, bp:]
        x1, cb, ht, st = _prompt_mixer(xp, mod_p, w, l)
        xp = _ffn(x1.reshape(bp * tp, D_MODEL), mod_p, tp, FFN_TILE, w, l, fg, last).reshape(bp, tp, D_MODEL)
        conv_p.append(cb)
        lru_p.append(ht.reshape(bp, D_RNN))
        gla_p.append(st)
        ya_s, cb_s, ht_s = _sample_lru(xs, mod_s, state_conv[l].transpose(1, 0, 2), state_lru, w, l)
        u_tm = _sample_inproj(xs, mod_s, w, l)
        u_sm = u_tm.reshape(ts, bs, N_IN_B).transpose(1, 0, 2).reshape(bs * ts, N_IN_B)
        yb_sm, gla_s = _sample_gla(u_sm, state_gla, gla_s, w, l, ts)
        yb_tm = yb_sm.reshape(bs, ts, D_MODEL).transpose(1, 0, 2).reshape(ts * bs, D_MODEL)
        x1_s = _sample_merge(xs, mod_s, u_tm, ya_s, yb_tm, w, l)
        xs = _ffn(x1_s, mod_s.reshape(1, bs, N_MOD), bs * ts, bs, w, l, fg, last)
        conv_s.append(cb_s.transpose(1, 0, 2))
        lru_s.append(ht_s)
    y_sample = xs.reshape(ts, bs, D_MODEL).transpose(1, 0, 2)
    return (xp, y_sample, jnp.stack(conv_p), jnp.stack(lru_p), jnp.stack(gla_p),
            jnp.stack(conv_s), jnp.stack(lru_s), gla_s)
```

```python
import functools

import jax
import jax.numpy as jnp
from jax import lax
from jax.experimental import pallas as pl
from jax.experimental.pallas import tpu as pltpu

F32 = jnp.float32
BF16 = jnp.bfloat16

D_MODEL = 1024
D_RNN = 1280
LRU_BW = 128
LRU_BLOCKS = D_RNN // LRU_BW
CONV_W = 4
LRU_C = 8.0
GLA_HEADS = 4
GLA_DK = 128
GLA_DV = 256
GLA_DK_TOTAL = GLA_HEADS * GLA_DK
GLA_DV_TOTAL = GLA_HEADS * GLA_DV
GLA_RANK = 16
GLA_TAU = 16.0
D_FF = 2816
EPS = 1e-6
TINY_F32 = 1.1754944e-38
N_MOD = 6 * D_MODEL

LANES = 128
SUBLANES = 8
VMEM_LIMIT_BYTES = 60 * 1024 * 1024

W_X, W_G, W_Q, W_K, W_V, W_R, W_LR, W_GA, W_GB = 0, 1280, 2560, 3072, 3584, 4608, 5632, 5648, 6672
N_IN = W_GB + D_MODEL
_LRO, _QO, _KO, _VO, _RO, _GAO, _GBO = 0, 128, 640, 1152, 2176, 3200, 4224
N_IN_B = _GBO + D_MODEL

PROMPT_TILE = 256
GLA_CHUNK = 128
FFN_TILE = 512
VRG_PIECE = 512
LRU_PAIR = 2 * LRU_BW
SAMPLE_ROW_BLOCK = 128
SAMPLE_STATE_BLOCK = 8
SAMPLE_ROW_GROUP = 16


def _cparams(sem):
    return pltpu.CompilerParams(dimension_semantics=sem, vmem_limit_bytes=VMEM_LIMIT_BYTES)


def _full_spec(shape):
    n = len(shape)
    return pl.BlockSpec(shape, lambda *_: (0,) * n, pipeline_mode=pl.Buffered(1))


def _full_out(shape):
    n = len(shape)
    return pl.BlockSpec(shape, lambda *_: (0,) * n)


def _wspec(l, shape):
    n = len(shape)
    return pl.BlockSpec((None,) + tuple(shape), lambda *_: (l,) + (0,) * n, pipeline_mode=pl.Buffered(1))


def _softplus(y):
    return jnp.maximum(y, 0.0) + jnp.log1p(jnp.exp(-jnp.abs(y)))


def _sigmoid(y):
    return 0.5 * jnp.tanh(0.5 * y) + 0.5


def _silu(y):
    return y * _sigmoid(y)


def _sqrt_nonneg(s):
    return s * lax.rsqrt(jnp.maximum(s, TINY_F32))


def _gelu_tanh(y):
    return 0.5 * y * (1.0 + jnp.tanh(0.7978845608028654 * (y + 0.044715 * (y * y * y))))


def _rms(x):
    return lax.rsqrt(jnp.mean(x * x, axis=-1, keepdims=True) + EPS)


def _norm_mod(x, g, scale, shift):
    return x * _rms(x) * g * (1.0 + scale) + shift


def _mod_part(m, i):
    return m[:, i * D_MODEL:(i + 1) * D_MODEL]


def _dot(a, b):
    return jnp.dot(a.astype(BF16), b.astype(BF16), preferred_element_type=F32)


def _dot_nt(a, b):
    return lax.dot_general(a.astype(BF16), b.astype(BF16), (((1,), (1,)), ((), ())),
                           preferred_element_type=F32)


def _lru_coeffs_block(xb, wg, ba, bx, lam):
    lamc = -LRU_C * _softplus(-lam)
    pre = _dot(xb, wg)
    r = _sigmoid(pre[:, :LRU_BW] + ba)
    i = _sigmoid(pre[:, LRU_BW:] + bx)
    a = jnp.exp(lamc * r)
    return a, _sqrt_nonneg(1.0 - a * a) * (i * xb)


def _gla_gate(ulr, wa2_ref, gba):
    z = _dot(ulr, wa2_ref[...]) + gba
    return (jnp.minimum(z, 0.0) - jnp.log1p(jnp.exp(-jnp.abs(z)))) * (1.0 / GLA_TAU)


def _group_cumsum(g, group):
    rin = lax.broadcasted_iota(jnp.int32, g.shape, 0) % group
    x = g
    k = 1
    while k < group:
        x = x + jnp.where(rin >= k, pltpu.roll(x, k, 0), 0.0)
        k *= 2
    return x


def _col_bcast(row):
    return jnp.broadcast_to(row, (LANES, LANES)).T


def _head_norm_gate(o, gng, ur):
    parts = []
    for hd in range(GLA_HEADS):
        sl = slice(hd * GLA_DV, (hd + 1) * GLA_DV)
        oh = o[:, sl]
        parts.append(oh * _rms(oh) * gng[:, sl])
    return jnp.concatenate(parts, axis=1) * _silu(ur)


def _merge_out(x, gt1, uga, ugb, ya, yb, pb_ref, wo_ref):
    pbv = _dot(yb, pb_ref[...])
    mm = _sigmoid(uga) * ya + _sigmoid(ugb) * pbv
    return x + gt1 * _dot(mm, wo_ref[...])


def _mod_kernel(c_ref, w_ref, b_ref, o_ref):
    o_ref[0] = _dot(_silu(c_ref[...]), w_ref[0]) + b_ref[0]


def _modulation(c_all, ada_w, ada_b):
    depth = ada_w.shape[0]
    rows = c_all.shape[0]
    nblk = N_MOD // D_MODEL
    return pl.pallas_call(
        _mod_kernel,
        grid=(depth, nblk),
        in_specs=[
            pl.BlockSpec((rows, D_MODEL), lambda l, j: (0, 0)),
            pl.BlockSpec((1, D_MODEL, D_MODEL), lambda l, j: (l, 0, j)),
            pl.BlockSpec((1, 1, D_MODEL), lambda l, j: (l, 0, j)),
        ],
        out_specs=pl.BlockSpec((1, rows, D_MODEL), lambda l, j: (l, 0, j)),
        out_shape=jax.ShapeDtypeStruct((depth, rows, N_MOD), F32),
        compiler_params=_cparams(("arbitrary", "arbitrary")),
        name="modulation",
    )(c_all, ada_w, ada_b.reshape(depth, 1, N_MOD))


def _shift_rows_in(blk, first_row):
    top = lax.broadcasted_iota(jnp.int32, blk.shape, 0) == 0
    return jnp.where(top, first_row, pltpu.roll(blk, 1, 0))


def _regroup_matrices(tt):
    ng = tt // SUBLANES
    ri = lax.broadcasted_iota(jnp.int32, (tt, tt), 0)
    ci = lax.broadcasted_iota(jnp.int32, (tt, tt), 1)
    to_blocks = jnp.where(ci == SUBLANES * (ri % ng) + ri // ng, 1.0, 0.0).astype(BF16)
    from_blocks = jnp.where(ci == ng * (ri % SUBLANES) + ri // SUBLANES, 1.0, 0.0).astype(BF16)
    return to_blocks, from_blocks


def _lru_conv(ux, cw, cb, conv_rows):
    ng = ux.shape[0] // SUBLANES
    first_tail = SUBLANES - (CONV_W - 1)
    xs = [ux[j * ng:(j + 1) * ng, :] for j in range(SUBLANES)]
    prev = {k: _shift_rows_in(xs[k], conv_rows[k - first_tail:k - first_tail + 1, :])
            for k in range(first_tail, SUBLANES)}
    xcs = []
    for j in range(SUBLANES):
        acc = cb
        for i in range(CONV_W):
            d = CONV_W - 1 - i
            src = xs[j - d] if j >= d else prev[j - d + SUBLANES]
            acc = acc + src * cw[i:i + 1, :]
        xcs.append(acc)
    tail = jnp.concatenate([xs[k][ng - 1:ng, :] for k in range(first_tail, SUBLANES)], axis=0)
    return jnp.concatenate(xcs, axis=0), tail


def _lru_scan(a, bb, carry):
    ng = a.shape[0] // SUBLANES
    loc = [bb[0:ng, :]]
    cum = [a[0:ng, :]]
    for j in range(1, SUBLANES):
        aj = a[j * ng:(j + 1) * ng, :]
        loc.append(aj * loc[-1] + bb[j * ng:(j + 1) * ng, :])
        cum.append(aj * cum[-1])
    ga, gb = cum[-1], loc[-1]
    rowg = lax.broadcasted_iota(jnp.int32, ga.shape, 0)
    k = 1
    while k < ng:
        ga_sh = jnp.where(rowg >= k, pltpu.roll(ga, k, 0), 1.0)
        gb_sh = jnp.where(rowg >= k, pltpu.roll(gb, k, 0), 0.0)
        gb = ga * gb_sh + gb
        ga = ga * ga_sh
        k *= 2
    leaving = gb + ga * carry
    entering = _shift_rows_in(leaving, carry)
    hs = jnp.concatenate([loc[j] + cum[j] * entering for j in range(SUBLANES)], axis=0)
    return hs, leaving[ng - 1:ng, :]


def _lru_lane_block(n, uan, cw_ref, cb_ref, wg_ref, ba_ref, bx_ref, lam_ref, conv_s, hc_s):
    sl = slice(n * LRU_BW, (n + 1) * LRU_BW)
    xc, tail = _lru_conv(uan[0], cw_ref[:, sl], cb_ref[:, sl], conv_s[0:CONV_W - 1, sl])
    conv_s[0:CONV_W - 1, sl] = tail
    a, bb = _lru_coeffs_block(xc, wg_ref[n], ba_ref[:, sl], bx_ref[:, sl], lam_ref[:, sl])
    hs, last = _lru_scan(a, bb, hc_s[0:1, sl])
    hc_s[0:1, sl] = last
    return (hs * _gelu_tanh(uan[1])).astype(BF16)


def _gla_prep(ulr, uq, uk, wa2_ref, gba):
    g = _gla_gate(ulr, wa2_ref, gba)
    bc = _group_cumsum(g, GLA_CHUNK)
    qt = uq * (jnp.exp(bc) * (GLA_DK ** -0.5))
    kt = uk * jnp.exp(-bc)
    return bc, qt, kt


def _gla_chunks(bc, qt, kt, uk, uv, s_s):
    tt = bc.shape[0]
    c = GLA_CHUNK
    ri = lax.broadcasted_iota(jnp.int32, (c, c), 0)
    ci = lax.broadcasted_iota(jnp.int32, (c, c), 1)
    causal = ri >= ci
    o_rows = []
    for ch in range(tt // c):
        rows = slice(ch * c, (ch + 1) * c)
        bl = bc[ch * c + c - 1:ch * c + c, :]
        kd = uk[rows, :] * jnp.exp(bl - bc[rows, :])
        dec = jnp.exp(bl)
        o_heads = []
        for hd in range(GLA_HEADS):
            ks = slice(hd * GLA_DK, (hd + 1) * GLA_DK)
            qh = qt[rows, ks].astype(BF16)
            vh = uv[rows, hd * GLA_DV:(hd + 1) * GLA_DV].astype(BF16)
            att = jnp.where(causal, _dot_nt(qh, kt[rows, ks]), 0.0)
            s_old = s_s[hd]
            o_heads.append(_dot(att, vh) + _dot(qh, s_old))
            dm = _col_bcast(dec[:, ks])
            s_s[hd] = s_old * jnp.concatenate([dm, dm], axis=1) + _dot(kd[:, ks].T, vh)
        o_rows.append(jnp.concatenate(o_heads, axis=1))
    return jnp.concatenate(o_rows, axis=0)


def _prompt_mixer_kernel(x_ref, mod_ref, g1_ref, wt_ref, cw_ref, cb_ref, wg_ref, ba_ref, bx_ref, lam_ref,
                         pa_ref, wa2_ref, gba_ref, gng_ref, pb_ref, wo_ref,
                         x1_ref, conv_ref, lru_ref, st_ref, conv_s, hc_s, s_s):
    t = pl.program_id(1)

    @pl.when(t == 0)
    def _():
        conv_s[...] = jnp.zeros_like(conv_s)
        hc_s[...] = jnp.zeros_like(hc_s)
        s_s[...] = jnp.zeros_like(s_s)

    x = x_ref[0]
    mod = mod_ref[0]
    hb16 = _norm_mod(x, g1_ref[...], _mod_part(mod, 1), _mod_part(mod, 0)).astype(BF16)
    to_blocks, from_blocks = _regroup_matrices(x.shape[0])
    u_qk = _dot_nt(hb16, wt_ref[W_Q:W_V, :])
    u_lr = _dot_nt(hb16, wt_ref[W_LR:W_LR + LANES, :])
    hb = jnp.dot(to_blocks, hb16, preferred_element_type=F32).astype(BF16)
    uk = u_qk[:, GLA_DK_TOTAL:]
    bc, qt, kt = _gla_prep(u_lr, u_qk[:, :GLA_DK_TOTAL], uk, wa2_ref, gba_ref[...])
    piece_rows = ([W_V + i * VRG_PIECE for i in range((W_LR - W_V) // VRG_PIECE)]
                  + [W_GA + i * VRG_PIECE for i in range((N_IN - W_GA) // VRG_PIECE)])
    per_pair = -(-len(piece_rows) // (LRU_BLOCKS // 2))
    vrg, ya_blocks = [], []
    for m in range(LRU_BLOCKS // 2):
        ux2 = _dot_nt(hb, wt_ref[W_X + m * LRU_PAIR:W_X + (m + 1) * LRU_PAIR, :])
        ug2 = _dot_nt(hb, wt_ref[W_G + m * LRU_PAIR:W_G + (m + 1) * LRU_PAIR, :])
        for r0 in piece_rows[m * per_pair:(m + 1) * per_pair]:
            vrg.append(_dot_nt(hb16, wt_ref[r0:r0 + VRG_PIECE, :]))
        for j in range(2):
            uan = (ux2[:, j * LRU_BW:(j + 1) * LRU_BW], ug2[:, j * LRU_BW:(j + 1) * LRU_BW])
            ya_blocks.append(_lru_lane_block(2 * m + j, uan, cw_ref, cb_ref, wg_ref, ba_ref, bx_ref, lam_ref,
                                             conv_s, hc_s))
    u_vrg = jnp.concatenate(vrg, axis=1)
    o = _gla_chunks(bc, qt, kt, uk, u_vrg[:, :GLA_DV_TOTAL], s_s)
    ya = jnp.dot(from_blocks, jnp.concatenate(ya_blocks, axis=1), preferred_element_type=F32).astype(BF16)
    ya = jnp.dot(ya, pa_ref[...], preferred_element_type=F32)
    yb = _head_norm_gate(o, gng_ref[...], u_vrg[:, GLA_DV_TOTAL:GLA_DV_TOTAL + D_MODEL])
    x1_ref[0] = _merge_out(x, _mod_part(mod, 2), u_vrg[:, 2 * D_MODEL:3 * D_MODEL], u_vrg[:, 3 * D_MODEL:],
                           ya, yb, pb_ref, wo_ref)

    @pl.when(t == pl.num_programs(1) - 1)
    def _():
        conv_ref[0] = conv_s[0:CONV_W - 1, :]
        lru_ref[0] = hc_s[0:1, :]
        st_ref[0] = s_s[...]


def _prompt_mixer(x, mod_p, w, l):
    b, t, _ = x.shape
    tt = PROMPT_TILE
    seq_spec = pl.BlockSpec((1, tt, D_MODEL), lambda i, j: (i, j, 0))
    return pl.pallas_call(
        _prompt_mixer_kernel,
        grid=(b, t // tt),
        in_specs=[
            seq_spec,
            pl.BlockSpec((1, 1, N_MOD), lambda i, j: (i, 0, 0)),
            _wspec(l, (1, D_MODEL)),
            _wspec(l, (N_IN, D_MODEL)),
            _wspec(l, (CONV_W, D_RNN)),
            _wspec(l, (1, D_RNN)),
            _wspec(l, (LRU_BLOCKS, LRU_BW, 2 * LRU_BW)),
            _wspec(l, (1, D_RNN)),
            _wspec(l, (1, D_RNN)),
            _wspec(l, (1, D_RNN)),
            _wspec(l, (D_RNN, D_MODEL)),
            _wspec(l, (LANES, GLA_DK_TOTAL)),
            _wspec(l, (1, GLA_DK_TOTAL)),
            _wspec(l, (1, GLA_DV_TOTAL)),
            _wspec(l, (GLA_DV_TOTAL, D_MODEL)),
            _wspec(l, (D_MODEL, D_MODEL)),
        ],
        out_specs=[
            seq_spec,
            pl.BlockSpec((1, CONV_W - 1, D_RNN), lambda i, j: (i, 0, 0)),
            pl.BlockSpec((1, 1, D_RNN), lambda i, j: (i, 0, 0)),
            pl.BlockSpec((1, GLA_HEADS, GLA_DK, GLA_DV), lambda i, j: (i, 0, 0, 0)),
        ],
        out_shape=[
            jax.ShapeDtypeStruct((b, t, D_MODEL), F32),
            jax.ShapeDtypeStruct((b, CONV_W - 1, D_RNN), F32),
            jax.ShapeDtypeStruct((b, 1, D_RNN), F32),
            jax.ShapeDtypeStruct((b, GLA_HEADS, GLA_DK, GLA_DV), F32),
        ],
        scratch_shapes=[
            pltpu.VMEM((SUBLANES, D_RNN), F32),
            pltpu.VMEM((SUBLANES, D_RNN), F32),
            pltpu.VMEM((GLA_HEADS, GLA_DK, GLA_DV), F32),
        ],
        compiler_params=_cparams(("arbitrary", "arbitrary")),
        name="prompt_mixer",
    )(x, mod_p, w["g1"], w["win_t"], w["conv_w"], w["conv_b"], w["wgate"], w["lru_ba"], w["lru_bx"],
      w["lru_lambda"], w["proj_a"], w["wa2"], w["gla_ba"], w["gla_norm_g"], w["proj_b"], w["w_out"])


def _ffn_kernel(x_ref, mod_ref, g2_ref, w1_ref, w2_ref, fg_ref, o_ref, *, final_norm):
    x = x_ref[...]
    m = mod_ref[0]
    h2 = _norm_mod(x, g2_ref[...], _mod_part(m, 4), _mod_part(m, 3))
    f = _dot(h2, w1_ref[...])
    x2 = x + _mod_part(m, 5) * _dot(_silu(f[:, :D_FF]) * f[:, D_FF:], w2_ref[...])
    if final_norm:
        x2 = x2 * _rms(x2) * fg_ref[...]
    o_ref[...] = x2


def _ffn(x2d, mod3, rows_per_mod, tile, w, l, final_g, final_norm):
    mrows = x2d.shape[0]
    r = mod3.shape[1]
    per = rows_per_mod // tile
    row_spec = pl.BlockSpec((tile, D_MODEL), lambda i: (i, 0))
    return pl.pallas_call(
        functools.partial(_ffn_kernel, final_norm=final_norm),
        grid=(mrows // tile,),
        in_specs=[
            row_spec,
            pl.BlockSpec((1, r, N_MOD), lambda i: (i // per, 0, 0)),
            _wspec(l, (1, D_MODEL)),
            _wspec(l, (D_MODEL, 2 * D_FF)),
            _wspec(l, (D_FF, D_MODEL)),
            _full_spec((1, D_MODEL)),
        ],
        out_specs=row_spec,
        out_shape=jax.ShapeDtypeStruct((mrows, D_MODEL), F32),
        compiler_params=_cparams(("arbitrary",)),
        name="ffn_final" if final_norm else "ffn",
    )(x2d, mod3, w["g2"], w["ffn_w1"], w["ffn_w2"], final_g)


def _sample_lru_kernel(x_ref, mod_ref, g1_ref, win_ref, cw_ref, cb_ref, wg_ref, ba_ref, bx_ref, lam_ref, pa_ref,
                       conv0_ref, h0_ref, ya_ref, conv_ref, lru_ref):
    ns = h0_ref.shape[0]
    nt = x_ref.shape[0] // ns
    m = mod_ref[...]
    tile = lambda v: jnp.concatenate([v] * nt, axis=0)
    h = _norm_mod(x_ref[...], g1_ref[...], tile(_mod_part(m, 1)), tile(_mod_part(m, 0)))
    ux = _dot_nt(h, win_ref[W_X:W_G, :])
    ug = _dot_nt(h, win_ref[W_G:W_Q, :])
    xa = [conv0_ref[i] for i in range(CONV_W - 1)] + [ux[i * ns:(i + 1) * ns, :] for i in range(nt)]
    cw = cw_ref[...]
    xcs = []
    for ti in range(nt):
        acc = cb_ref[...]
        for i in range(CONV_W):
            acc = acc + xa[ti + i] * cw[i:i + 1, :]
        xcs.append(acc)
    xc = jnp.concatenate(xcs, axis=0)
    coeffs = [_lru_coeffs_block(xc[:, n * LRU_BW:(n + 1) * LRU_BW], wg_ref[n], ba_ref[:, n * LRU_BW:(n + 1) * LRU_BW],
                                bx_ref[:, n * LRU_BW:(n + 1) * LRU_BW], lam_ref[:, n * LRU_BW:(n + 1) * LRU_BW])
              for n in range(LRU_BLOCKS)]
    a = jnp.concatenate([c[0] for c in coeffs], axis=1)
    bb = jnp.concatenate([c[1] for c in coeffs], axis=1)
    hc = h0_ref[...]
    hs = []
    for ti in range(nt):
        rows = slice(ti * ns, (ti + 1) * ns)
        hc = a[rows, :] * hc + bb[rows, :]
        hs.append(hc)
    ya_ref[...] = _dot(jnp.concatenate(hs, axis=0) * _gelu_tanh(ug), pa_ref[...])
    for i in range(CONV_W - 1):
        conv_ref[i] = xa[nt + i]
    lru_ref[...] = hc


def _sample_lru(x_tm, mod_s, conv0_tm, h0_all, w, l):
    rows = x_tm.shape[0]
    ns = h0_all.shape[1]
    return pl.pallas_call(
        _sample_lru_kernel,
        grid=(1,),
        in_specs=[
            _full_spec((rows, D_MODEL)),
            _full_spec((ns, N_MOD)),
            _wspec(l, (1, D_MODEL)),
            _wspec(l, (N_IN, D_MODEL)),
            _wspec(l, (CONV_W, D_RNN)),
            _wspec(l, (1, D_RNN)),
            _wspec(l, (LRU_BLOCKS, LRU_BW, 2 * LRU_BW)),
            _wspec(l, (1, D_RNN)),
            _wspec(l, (1, D_RNN)),
            _wspec(l, (1, D_RNN)),
            _wspec(l, (D_RNN, D_MODEL)),
            _full_spec((CONV_W - 1, ns, D_RNN)),
            _wspec(l, (ns, D_RNN)),
        ],
        out_specs=[
            _full_out((rows, D_MODEL)),
            _full_out((CONV_W - 1, ns, D_RNN)),
            _full_out((ns, D_RNN)),
        ],
        out_shape=[
            jax.ShapeDtypeStruct((rows, D_MODEL), F32),
            jax.ShapeDtypeStruct((CONV_W - 1, ns, D_RNN), F32),
            jax.ShapeDtypeStruct((ns, D_RNN), F32),
        ],
        compiler_params=_cparams(("arbitrary",)),
        name="sample_lru",
    )(x_tm, mod_s, w["g1"], w["win_t"], w["conv_w"], w["conv_b"], w["wgate"], w["lru_ba"], w["lru_bx"],
      w["lru_lambda"], w["proj_a"], conv0_tm, h0_all)


def _sample_inproj_kernel(x_ref, mod_ref, g1_ref, win_ref, u_ref):
    nt = x_ref.shape[0] // mod_ref.shape[0]
    m = mod_ref[...]
    tile = lambda v: jnp.concatenate([v] * nt, axis=0)
    h = _norm_mod(x_ref[...], g1_ref[...], tile(_mod_part(m, 1)), tile(_mod_part(m, 0)))
    u_ref[...] = jnp.concatenate([_dot_nt(h, win_ref[W_LR:W_LR + LANES, :]), _dot_nt(h, win_ref[W_Q:W_LR, :]),
                                  _dot_nt(h, win_ref[W_GA:N_IN, :])], axis=1)


def _sample_inproj(x_tm, mod_s, w, l):
    rows = x_tm.shape[0]
    return pl.pallas_call(
        _sample_inproj_kernel,
        grid=(1,),
        in_specs=[
            _full_spec((rows, D_MODEL)),
            _full_spec(mod_s.shape),
            _wspec(l, (1, D_MODEL)),
            _wspec(l, (N_IN, D_MODEL)),
        ],
        out_specs=_full_out((rows, N_IN_B)),
        out_shape=jax.ShapeDtypeStruct((rows, N_IN_B), F32),
        compiler_params=_cparams(("arbitrary",)),
        name="sample_inproj",
    )(x_tm, mod_s, w["g1"], w["win_t"])


def _sample_gla_kernel(u_ref, wa2_ref, gba_ref, gng_ref, s0_ref, *rest, nt, layer, carries_buffer):
    if carries_buffer:
        yb_ref, st_ref, o_s, q_s, v_s, kdt_s, bl_s = rest[1:]
    else:
        yb_ref, st_all_ref, o_s, q_s, v_s, kdt_s, bl_s = rest
        st_ref = st_all_ref.at[layer]
        for other in range(st_all_ref.shape[0]):
            if other != layer:
                st_all_ref[other] = jnp.zeros(st_all_ref.shape[1:], F32)
    rows = u_ref.shape[0]
    steps_per_rows = rows // (SAMPLE_STATE_BLOCK * nt)
    sub = pl.program_id(0) % steps_per_rows
    rg = SAMPLE_ROW_GROUP
    seq_per_group = rg // nt

    @pl.when(sub == 0)
    def _():
        u = u_ref[...]
        g = _gla_gate(u[:, _LRO:_LRO + LANES], wa2_ref, gba_ref[...])
        bc = _group_cumsum(g, nt)
        rin = lax.broadcasted_iota(jnp.int32, g.shape, 0) % nt
        sfx = jnp.zeros_like(g)
        for k in range(1, nt):
            sfx = sfx + jnp.where(rin + k < nt, pltpu.roll(g, rows - k, 0), 0.0)
        qt = u[:, _QO:_QO + GLA_DK_TOTAL] * (jnp.exp(bc) * (GLA_DK ** -0.5))
        uk = u[:, _KO:_KO + GLA_DK_TOTAL]
        kt = uk * jnp.exp(-bc)
        kd = uk * jnp.exp(sfx)
        q_s[...] = qt.astype(BF16)
        v_s[...] = u[:, _VO:_VO + GLA_DV_TOTAL].astype(BF16)
        bl_s[...] = bc + sfx
        ri = lax.broadcasted_iota(jnp.int32, (rows, rows), 0)
        ci = lax.broadcasted_iota(jnp.int32, (rows, rows), 1)
        same_seq_causal = (ri // nt == ci // nt) & (ri >= ci)
        for hd in range(GLA_HEADS):
            ks = slice(hd * GLA_DK, (hd + 1) * GLA_DK)
            att = jnp.where(same_seq_causal, _dot_nt(qt[:, ks], kt[:, ks]), 0.0)
            o_s[:, hd * GLA_DV:(hd + 1) * GLA_DV] = _dot(att, v_s[:, hd * GLA_DV:(hd + 1) * GLA_DV])
            kdt_s[hd] = kd[:, ks].T

    lane_seq = lax.broadcasted_iota(jnp.int32, (GLA_DK, rows), 1) // nt
    row_seq = lax.broadcasted_iota(jnp.int32, (rg, GLA_DV), 0) // nt

    def group(gi, carry):
        r0 = pl.multiple_of((sub * (SAMPLE_STATE_BLOCK // seq_per_group) + gi) * rg, rg)
        dec_g = jnp.exp(bl_s[pl.ds(r0, rg), :])
        for hd in range(GLA_HEADS):
            ks = slice(hd * GLA_DK, (hd + 1) * GLA_DK)
            vsl = slice(hd * GLA_DV, (hd + 1) * GLA_DV)
            qg = q_s[pl.ds(r0, rg), ks]
            vh = v_s[:, vsl]
            inter = jnp.zeros((rg, GLA_DV), F32)
            for j in range(seq_per_group):
                s_loc = gi * seq_per_group + j
                s_row = sub * SAMPLE_STATE_BLOCK + s_loc
                s_old = s0_ref[s_loc, hd]
                inter = jnp.where(row_seq == j, _dot(qg, s_old), inter)
                kdt = jnp.where(lane_seq == s_row, kdt_s[hd], 0.0)
                dm = _col_bcast(dec_g[j * nt:j * nt + 1, ks])
                st_ref[s_loc, hd] = s_old * jnp.concatenate([dm, dm], axis=1) + _dot(kdt, vh)
            o_s[pl.ds(r0, rg), vsl] = o_s[pl.ds(r0, rg), vsl] + inter
        return carry

    lax.fori_loop(0, SAMPLE_STATE_BLOCK // seq_per_group, group, 0)

    @pl.when(sub == steps_per_rows - 1)
    def _():
        yb_ref[...] = _head_norm_gate(o_s[...], gng_ref[...], u_ref[:, _RO:_RO + D_MODEL])


def _sample_gla(u_sm, state_all, new_state_all, w, l, nt):
    rows = u_sm.shape[0]
    br = SAMPLE_ROW_BLOCK
    sb = SAMPLE_STATE_BLOCK
    per = br // (sb * nt)
    state_spec = pl.BlockSpec((None, sb, GLA_HEADS, GLA_DK, GLA_DV), lambda i: (l, i, 0, 0, 0))
    out_state_spec = state_spec
    if new_state_all is None:
        out_state_spec = pl.BlockSpec((state_all.shape[0], sb, GLA_HEADS, GLA_DK, GLA_DV), lambda i: (0, i, 0, 0, 0))
    in_specs = [
        pl.BlockSpec((br, N_IN_B), lambda i: (i // per, 0)),
        _wspec(l, (LANES, GLA_DK_TOTAL)),
        _wspec(l, (1, GLA_DK_TOTAL)),
        _wspec(l, (1, GLA_DV_TOTAL)),
        state_spec,
    ]
    args = [u_sm, w["wa2"], w["gla_ba"], w["gla_norm_g"], state_all]
    aliases = {}
    if new_state_all is not None:
        in_specs.append(pl.BlockSpec(memory_space=pl.ANY))
        args.append(new_state_all)
        aliases = {len(args) - 1: 1}
    return pl.pallas_call(
        functools.partial(_sample_gla_kernel, nt=nt, layer=l, carries_buffer=new_state_all is not None),
        grid=(rows // (sb * nt),),
        in_specs=in_specs,
        out_specs=[
            pl.BlockSpec((br, D_MODEL), lambda i: (i // per, 0)),
            out_state_spec,
        ],
        out_shape=[
            jax.ShapeDtypeStruct((rows, D_MODEL), F32),
            jax.ShapeDtypeStruct(state_all.shape, F32),
        ],
        input_output_aliases=aliases,
        scratch_shapes=[
            pltpu.VMEM((br, GLA_DV_TOTAL), F32),
            pltpu.VMEM((br, GLA_DK_TOTAL), BF16),
            pltpu.VMEM((br, GLA_DV_TOTAL), BF16),
            pltpu.VMEM((GLA_HEADS, GLA_DK, br), F32),
            pltpu.VMEM((br, GLA_DK_TOTAL), F32),
        ],
        compiler_params=_cparams(("arbitrary",)),
        name="sample_gla",
    )(*args)


def _sample_merge_kernel(x_ref, mod_ref, u_ref, ya_ref, yb_ref, pb_ref, wo_ref, x1_ref):
    nt = x_ref.shape[0] // mod_ref.shape[0]
    gt1 = jnp.concatenate([_mod_part(mod_ref[...], 2)] * nt, axis=0)
    x1_ref[...] = _merge_out(x_ref[...], gt1, u_ref[:, _GAO:_GAO + D_MODEL], u_ref[:, _GBO:_GBO + D_MODEL],
                             ya_ref[...], yb_ref[...], pb_ref, wo_ref)


def _sample_merge(x_tm, mod_s, u_tm, ya, yb_tm, w, l):
    rows = x_tm.shape[0]
    return pl.pallas_call(
        _sample_merge_kernel,
        grid=(1,),
        in_specs=[
            _full_spec((rows, D_MODEL)),
            _full_spec(mod_s.shape),
            _full_spec((rows, N_IN_B)),
            _full_spec((rows, D_MODEL)),
            _full_spec((rows, D_MODEL)),
            _wspec(l, (GLA_DV_TOTAL, D_MODEL)),
            _wspec(l, (D_MODEL, D_MODEL)),
        ],
        out_specs=_full_out((rows, D_MODEL)),
        out_shape=jax.ShapeDtypeStruct((rows, D_MODEL), F32),
        compiler_params=_cparams(("arbitrary",)),
        name="sample_merge",
    )(x_tm, mod_s, u_tm, ya, yb_tm, w["proj_b"], w["w_out"])


def _stacked_weights(norm1_g, norm2_g, w_in, conv_w, conv_b, lru_wa, lru_ba, lru_wx, lru_bx, lru_lambda,
                     gla_wa2, gla_ba, gla_norm_g, proj_a, proj_b, w_out, ffn_w1, ffn_w2):
    depth = w_in.shape[0]
    row = lambda v: v.reshape(depth, 1, -1)
    return {
        "g1": row(norm1_g), "g2": row(norm2_g),
        "win_t": jnp.swapaxes(w_in, 1, 2).astype(BF16),
        "conv_w": conv_w, "conv_b": row(conv_b),
        "wgate": jnp.concatenate([lru_wa, lru_wx], axis=-1).astype(BF16),
        "lru_ba": row(lru_ba), "lru_bx": row(lru_bx), "lru_lambda": row(lru_lambda),
        "wa2": jnp.pad(gla_wa2, ((0, 0), (0, LANES - GLA_RANK), (0, 0))).astype(BF16),
        "gla_ba": row(gla_ba), "gla_norm_g": row(gla_norm_g),
        "proj_a": proj_a.astype(BF16), "proj_b": proj_b.astype(BF16), "w_out": w_out.astype(BF16),
        "ffn_w1": ffn_w1.astype(BF16), "ffn_w2": ffn_w2.astype(BF16),
    }


def kernel(x_prompt, x_sample, c_prompt, c_sample, state_conv, state_lru, state_gla, norm1_g, norm2_g, ada_w, ada_b,
           w_in, conv_w, conv_b, lru_wa, lru_ba, lru_wx, lru_bx, lru_lambda, gla_wa2, gla_ba, gla_norm_g, proj_a,
           proj_b, w_out, ffn_w1, ffn_w2, final_g):
    bp, tp, _ = x_prompt.shape
    bs, ts, _ = x_sample.shape
    depth = w_in.shape[0]
    mod = _modulation(jnp.concatenate([c_prompt, c_sample], axis=0), ada_w, ada_b)
    w = _stacked_weights(norm1_g, norm2_g, w_in, conv_w, conv_b, lru_wa, lru_ba, lru_wx, lru_bx, lru_lambda,
                         gla_wa2, gla_ba, gla_norm_g, proj_a, proj_b, w_out, ffn_w1, ffn_w2)
    fg = final_g.reshape(1, D_MODEL)
    xp = x_prompt
    xs = x_sample.transpose(1, 0, 2).reshape(ts * bs, D_MODEL)
    conv_p, lru_p, gla_p, conv_s, lru_s = [], [], [], [], []
    gla_s = None
    for l in range(depth):
        last = l == depth - 1
        mod_p = mod[l, :bp].reshape(bp, 1, N_MOD)
        mod_s = mod[l, bp:]
        x1, cb, ht, st = _prompt_mixer(xp, mod_p, w, l)
        xp = _ffn(x1.reshape(bp * tp, D_MODEL), mod_p, tp, FFN_TILE, w, l, fg, last).reshape(bp, tp, D_MODEL)
        conv_p.append(cb)
        lru_p.append(ht.reshape(bp, D_RNN))
        gla_p.append(st)
        ya_s, cb_s, ht_s = _sample_lru(xs, mod_s, state_conv[l].transpose(1, 0, 2), state_lru, w, l)
        u_tm = _sample_inproj(xs, mod_s, w, l)
        u_sm = u_tm.reshape(ts, bs, N_IN_B).transpose(1, 0, 2).reshape(bs * ts, N_IN_B)
        yb_sm, gla_s = _sample_gla(u_sm, state_gla, gla_s, w, l, ts)
        yb_tm = yb_sm.reshape(bs, ts, D_MODEL).transpose(1, 0, 2).reshape(ts * bs, D_MODEL)
        x1_s = _sample_merge(xs, mod_s, u_tm, ya_s, yb_tm, w, l)
        xs = _ffn(x1_s, mod_s.reshape(1, bs, N_MOD), bs * ts, bs, w, l, fg, last)
        conv_s.append(cb_s.transpose(1, 0, 2))
        lru_s.append(ht_s)
    y_sample = xs.reshape(ts, bs, D_MODEL).transpose(1, 0, 2)
    return (xp, y_sample, jnp.stack(conv_p), jnp.stack(lru_p), jnp.stack(gla_p),
            jnp.stack(conv_s), jnp.stack(lru_s), gla_s)
```

```python
import functools

import jax
import jax.numpy as jnp
from jax import lax
from jax.experimental import pallas as pl
from jax.experimental.pallas import tpu as pltpu

F32 = jnp.float32
BF16 = jnp.bfloat16

D_MODEL = 1024
D_RNN = 1280
LRU_BW = 128
LRU_BLOCKS = D_RNN // LRU_BW
CONV_W = 4
LRU_C = 8.0
GLA_HEADS = 4
GLA_DK = 128
GLA_DV = 256
GLA_DK_TOTAL = GLA_HEADS * GLA_DK
GLA_DV_TOTAL = GLA_HEADS * GLA_DV
GLA_RANK = 16
GLA_TAU = 16.0
D_FF = 2816
EPS = 1e-6
TINY_F32 = 1.1754944e-38
N_MOD = 6 * D_MODEL

LANES = 128
SUBLANES = 8
VMEM_LIMIT_BYTES = 60 * 1024 * 1024

W_X, W_G, W_Q, W_K, W_V, W_R, W_LR, W_GA, W_GB = 0, 1280, 2560, 3072, 3584, 4608, 5632, 5648, 6672
N_IN = W_GB + D_MODEL
_LRO, _QO, _KO, _VO, _RO = 0, 128, 640, 1152, 2176
N_IN_B = _RO + D_MODEL

PROMPT_TILE = 256
GLA_CHUNK = 128
FFN_TILE = 512
VRG_PIECE = 512
LRU_PAIR = 2 * LRU_BW
SAMPLE_ROW_BLOCK = 128
SAMPLE_STATE_BLOCK = 8
SAMPLE_ROW_GROUP = 16


def _cparams(sem):
    return pltpu.CompilerParams(dimension_semantics=sem, vmem_limit_bytes=VMEM_LIMIT_BYTES)


def _full_spec(shape):
    n = len(shape)
    return pl.BlockSpec(shape, lambda *_: (0,) * n, pipeline_mode=pl.Buffered(1))


def _full_out(shape):
    n = len(shape)
    return pl.BlockSpec(shape, lambda *_: (0,) * n)


def _wspec(l, shape):
    n = len(shape)
    return pl.BlockSpec((None,) + tuple(shape), lambda *_: (l,) + (0,) * n, pipeline_mode=pl.Buffered(1))


def _softplus(y):
    return jnp.maximum(y, 0.0) + jnp.log1p(jnp.exp(-jnp.abs(y)))


def _sigmoid(y):
    return 0.5 * jnp.tanh(0.5 * y) + 0.5


def _silu(y):
    t = 0.5 * y
    return t + t * jnp.tanh(t)


def _sqrt_nonneg(s):
    return s * lax.rsqrt(jnp.maximum(s, TINY_F32))


def _gelu_tanh(y):
    return 0.5 * y * (1.0 + jnp.tanh(0.7978845608028654 * (y + 0.044715 * (y * y * y))))


def _rms(x):
    return lax.rsqrt(jnp.mean(x * x, axis=-1, keepdims=True) + EPS)


def _norm_mod(x, g, scale, shift):
    return (x * _rms(x)) * (g * (1.0 + scale)) + shift


def _mod_part(m, i):
    return m[:, i * D_MODEL:(i + 1) * D_MODEL]


def _dot(a, b):
    return jnp.dot(a.astype(BF16), b.astype(BF16), preferred_element_type=F32)


def _dot_nt(a, b):
    return lax.dot_general(a.astype(BF16), b.astype(BF16), (((1,), (1,)), ((), ())),
                           preferred_element_type=F32)


def _lru_coeffs_block(xb, wg, ba, bx, lam):
    lamc = -LRU_C * _softplus(-lam)
    pre = _dot(xb, wg)
    r = _sigmoid(pre[:, :LRU_BW] + ba)
    i = _sigmoid(pre[:, LRU_BW:] + bx)
    a = jnp.exp(lamc * r)
    return a, _sqrt_nonneg(1.0 - a * a) * (i * xb)


def _gla_gate(ulr, wa2_ref, gba):
    z = _dot(ulr, wa2_ref[...]) + gba
    return (jnp.minimum(z, 0.0) - jnp.log(1.0 + jnp.exp(-jnp.abs(z)))) * (1.0 / GLA_TAU)


def _group_cumsum(g, group):
    rin = lax.broadcasted_iota(jnp.int32, g.shape, 0) % group
    x = g
    k = 1
    while k < group:
        x = x + jnp.where(rin >= k, pltpu.roll(x, k, 0), 0.0)
        k *= 2
    return x


def _col_bcast(row):
    return jnp.broadcast_to(row, (LANES, LANES)).T


def _head_norm_gate(o, gng, ur):
    parts = []
    for hd in range(GLA_HEADS):
        sl = slice(hd * GLA_DV, (hd + 1) * GLA_DV)
        oh = o[:, sl]
        parts.append(oh * _rms(oh) * gng[:, sl])
    return jnp.concatenate(parts, axis=1) * _silu(ur)


def _merge_out(x, gt1, uga, ugb, ya, yb, pb_ref, wo_ref):
    pbv = _dot(yb, pb_ref[...])
    mm = _sigmoid(uga) * ya + _sigmoid(ugb) * pbv
    return x + gt1 * _dot(mm, wo_ref[...])


def _mod_kernel(c_ref, w_ref, b_ref, o_ref):
    o_ref[0] = _dot(_silu(c_ref[...]), w_ref[0]) + b_ref[0]


def _modulation(c_all, ada_w, ada_b):
    depth = ada_w.shape[0]
    rows = c_all.shape[0]
    nblk = N_MOD // D_MODEL
    return pl.pallas_call(
        _mod_kernel,
        grid=(depth, nblk),
        in_specs=[
            pl.BlockSpec((rows, D_MODEL), lambda l, j: (0, 0)),
            pl.BlockSpec((1, D_MODEL, D_MODEL), lambda l, j: (l, 0, j)),
            pl.BlockSpec((1, 1, D_MODEL), lambda l, j: (l, 0, j)),
        ],
        out_specs=pl.BlockSpec((1, rows, D_MODEL), lambda l, j: (l, 0, j)),
        out_shape=jax.ShapeDtypeStruct((depth, rows, N_MOD), F32),
        compiler_params=_cparams(("arbitrary", "arbitrary")),
        name="modulation",
    )(c_all, ada_w, ada_b.reshape(depth, 1, N_MOD))


def _shift_rows_in(blk, first_row):
    top = lax.broadcasted_iota(jnp.int32, blk.shape, 0) == 0
    return jnp.where(top, first_row, pltpu.roll(blk, 1, 0))


def _regroup_matrices(tt):
    ng = tt // SUBLANES
    ri = lax.broadcasted_iota(jnp.int32, (tt, tt), 0)
    ci = lax.broadcasted_iota(jnp.int32, (tt, tt), 1)
    to_blocks = jnp.where(ci == SUBLANES * (ri % ng) + ri // ng, 1.0, 0.0).astype(BF16)
    from_blocks = jnp.where(ci == ng * (ri % SUBLANES) + ri // SUBLANES, 1.0, 0.0).astype(BF16)
    return to_blocks, from_blocks


def _lru_conv(ux, cw, cb, conv_rows):
    ng = ux.shape[0] // SUBLANES
    first_tail = SUBLANES - (CONV_W - 1)
    xs = [ux[j * ng:(j + 1) * ng, :] for j in range(SUBLANES)]
    prev = {k: _shift_rows_in(xs[k], conv_rows[k - first_tail:k - first_tail + 1, :])
            for k in range(first_tail, SUBLANES)}
    xcs = []
    for j in range(SUBLANES):
        acc = cb
        for i in range(CONV_W):
            d = CONV_W - 1 - i
            src = xs[j - d] if j >= d else prev[j - d + SUBLANES]
            acc = acc + src * cw[i:i + 1, :]
        xcs.append(acc)
    tail = jnp.concatenate([xs[k][ng - 1:ng, :] for k in range(first_tail, SUBLANES)], axis=0)
    return jnp.concatenate(xcs, axis=0), tail


def _lru_scan(a, bb, carry):
    ng = a.shape[0] // SUBLANES
    loc = [bb[0:ng, :]]
    cum = [a[0:ng, :]]
    for j in range(1, SUBLANES):
        aj = a[j * ng:(j + 1) * ng, :]
        loc.append(aj * loc[-1] + bb[j * ng:(j + 1) * ng, :])
        cum.append(aj * cum[-1])
    ga, gb = cum[-1], loc[-1]
    rowg = lax.broadcasted_iota(jnp.int32, ga.shape, 0)
    k = 1
    while k < ng:
        ga_sh = jnp.where(rowg >= k, pltpu.roll(ga, k, 0), 1.0)
        gb_sh = jnp.where(rowg >= k, pltpu.roll(gb, k, 0), 0.0)
        gb = ga * gb_sh + gb
        ga = ga * ga_sh
        k *= 2
    leaving = gb + ga * carry
    entering = _shift_rows_in(leaving, carry)
    hs = jnp.concatenate([loc[j] + cum[j] * entering for j in range(SUBLANES)], axis=0)
    return hs, leaving[ng - 1:ng, :]


def _lru_lane_block(n, uan, cw_ref, cb_ref, wg_ref, ba_ref, bx_ref, lam_ref, conv_s, hc_s):
    sl = slice(n * LRU_BW, (n + 1) * LRU_BW)
    xc, tail = _lru_conv(uan[0], cw_ref[:, sl], cb_ref[:, sl], conv_s[0:CONV_W - 1, sl])
    conv_s[0:CONV_W - 1, sl] = tail
    a, bb = _lru_coeffs_block(xc, wg_ref[n], ba_ref[:, sl], bx_ref[:, sl], lam_ref[:, sl])
    hs, last = _lru_scan(a, bb, hc_s[0:1, sl])
    hc_s[0:1, sl] = last
    return (hs * _gelu_tanh(uan[1])).astype(BF16)


def _gla_prep(ulr, uq, uk, wa2_ref, gba):
    g = _gla_gate(ulr, wa2_ref, gba)
    bc = _group_cumsum(g, GLA_CHUNK)
    qt = uq * (jnp.exp(bc) * (GLA_DK ** -0.5))
    kt = uk * jnp.exp(-bc)
    return bc, qt, kt


def _gla_chunks(bc, qt, kt, uk, uv, s_s):
    tt = bc.shape[0]
    c = GLA_CHUNK
    ri = lax.broadcasted_iota(jnp.int32, (c, c), 0)
    ci = lax.broadcasted_iota(jnp.int32, (c, c), 1)
    causal = ri >= ci
    o_rows = []
    for ch in range(tt // c):
        rows = slice(ch * c, (ch + 1) * c)
        bl = bc[ch * c + c - 1:ch * c + c, :]
        kd = uk[rows, :] * jnp.exp(bl - bc[rows, :])
        dec = jnp.exp(bl)
        o_heads = []
        for hd in range(GLA_HEADS):
            ks = slice(hd * GLA_DK, (hd + 1) * GLA_DK)
            qh = qt[rows, ks].astype(BF16)
            vh = uv[rows, hd * GLA_DV:(hd + 1) * GLA_DV].astype(BF16)
            att = jnp.where(causal, _dot_nt(qh, kt[rows, ks]), 0.0)
            s_old = s_s[hd]
            o_heads.append(_dot(att, vh) + _dot(qh, s_old))
            dm = _col_bcast(dec[:, ks])
            s_s[hd] = s_old * jnp.concatenate([dm, dm], axis=1) + _dot(kd[:, ks].T, vh)
        o_rows.append(jnp.concatenate(o_heads, axis=1))
    return jnp.concatenate(o_rows, axis=0)


def _prompt_mixer_kernel(x_ref, mod_ref, g1_ref, wt_ref, cw_ref, cb_ref, wg_ref, ba_ref, bx_ref, lam_ref,
                         pa_ref, wa2_ref, gba_ref, gng_ref, pb_ref, wo_ref,
                         x1_ref, conv_ref, lru_ref, st_ref, conv_s, hc_s, s_s):
    t = pl.program_id(1)

    @pl.when(t == 0)
    def _():
        conv_s[...] = jnp.zeros_like(conv_s)
        hc_s[...] = jnp.zeros_like(hc_s)
        s_s[...] = jnp.zeros_like(s_s)

    x = x_ref[0]
    mod = mod_ref[0]
    hb16 = _norm_mod(x, g1_ref[...], _mod_part(mod, 1), _mod_part(mod, 0)).astype(BF16)
    to_blocks, from_blocks = _regroup_matrices(x.shape[0])
    u_qk = _dot_nt(hb16, wt_ref[W_Q:W_V, :])
    u_lr = _dot_nt(hb16, wt_ref[W_LR:W_LR + LANES, :])
    hb = jnp.dot(to_blocks, hb16, preferred_element_type=F32).astype(BF16)
    uk = u_qk[:, GLA_DK_TOTAL:]
    bc, qt, kt = _gla_prep(u_lr, u_qk[:, :GLA_DK_TOTAL], uk, wa2_ref, gba_ref[...])
    piece_rows = ([W_V + i * VRG_PIECE for i in range((W_LR - W_V) // VRG_PIECE)]
                  + [W_GA + i * VRG_PIECE for i in range((N_IN - W_GA) // VRG_PIECE)])
    per_pair = -(-len(piece_rows) // (LRU_BLOCKS // 2))
    vrg, ya_blocks = [], []
    for m in range(LRU_BLOCKS // 2):
        ux2 = _dot_nt(hb, wt_ref[W_X + m * LRU_PAIR:W_X + (m + 1) * LRU_PAIR, :])
        ug2 = _dot_nt(hb, wt_ref[W_G + m * LRU_PAIR:W_G + (m + 1) * LRU_PAIR, :])
        for r0 in piece_rows[m * per_pair:(m + 1) * per_pair]:
            vrg.append(_dot_nt(hb16, wt_ref[r0:r0 + VRG_PIECE, :]))
        for j in range(2):
            uan = (ux2[:, j * LRU_BW:(j + 1) * LRU_BW], ug2[:, j * LRU_BW:(j + 1) * LRU_BW])
            ya_blocks.append(_lru_lane_block(2 * m + j, uan, cw_ref, cb_ref, wg_ref, ba_ref, bx_ref, lam_ref,
                                             conv_s, hc_s))
    u_vrg = jnp.concatenate(vrg, axis=1)
    o = _gla_chunks(bc, qt, kt, uk, u_vrg[:, :GLA_DV_TOTAL], s_s)
    ya = jnp.dot(from_blocks, jnp.concatenate(ya_blocks, axis=1), preferred_element_type=F32).astype(BF16)
    ya = jnp.dot(ya, pa_ref[...], preferred_element_type=F32)
    yb = _head_norm_gate(o, gng_ref[...], u_vrg[:, GLA_DV_TOTAL:GLA_DV_TOTAL + D_MODEL])
    x1_ref[0] = _merge_out(x, _mod_part(mod, 2), u_vrg[:, 2 * D_MODEL:3 * D_MODEL], u_vrg[:, 3 * D_MODEL:],
                           ya, yb, pb_ref, wo_ref)

    @pl.when(t == pl.num_programs(1) - 1)
    def _():
        conv_ref[0] = conv_s[0:CONV_W - 1, :]
        lru_ref[0] = hc_s[0:1, :]
        st_ref[0] = s_s[...]


def _prompt_mixer(x, mod_p, w, l):
    b, t, _ = x.shape
    tt = PROMPT_TILE
    per_seq = t // tt
    seq_spec = pl.BlockSpec((1, tt, D_MODEL), lambda i, j: (i, j, 0))
    return pl.pallas_call(
        _prompt_mixer_kernel,
        grid=(b, per_seq),
        in_specs=[
            seq_spec,
            pl.BlockSpec((1, 1, N_MOD), lambda i, j: (i, 0, 0)),
            _wspec(l, (1, D_MODEL)),
            _wspec(l, (N_IN, D_MODEL)),
            _wspec(l, (CONV_W, D_RNN)),
            _wspec(l, (1, D_RNN)),
            _wspec(l, (LRU_BLOCKS, LRU_BW, 2 * LRU_BW)),
            _wspec(l, (1, D_RNN)),
            _wspec(l, (1, D_RNN)),
            _wspec(l, (1, D_RNN)),
            _wspec(l, (D_RNN, D_MODEL)),
            _wspec(l, (LANES, GLA_DK_TOTAL)),
            _wspec(l, (1, GLA_DK_TOTAL)),
            _wspec(l, (1, GLA_DV_TOTAL)),
            _wspec(l, (GLA_DV_TOTAL, D_MODEL)),
            _wspec(l, (D_MODEL, D_MODEL)),
        ],
        out_specs=[
            seq_spec,
            pl.BlockSpec((1, CONV_W - 1, D_RNN), lambda i, j: (i, 0, 0)),
            pl.BlockSpec((1, 1, D_RNN), lambda i, j: (i, 0, 0)),
            pl.BlockSpec((1, GLA_HEADS, GLA_DK, GLA_DV), lambda i, j: (i, 0, 0, 0)),
        ],
        out_shape=[
            jax.ShapeDtypeStruct((b, t, D_MODEL), F32),
            jax.ShapeDtypeStruct((b, CONV_W - 1, D_RNN), F32),
            jax.ShapeDtypeStruct((b, 1, D_RNN), F32),
            jax.ShapeDtypeStruct((b, GLA_HEADS, GLA_DK, GLA_DV), F32),
        ],
        scratch_shapes=[
            pltpu.VMEM((SUBLANES, D_RNN), F32),
            pltpu.VMEM((SUBLANES, D_RNN), F32),
            pltpu.VMEM((GLA_HEADS, GLA_DK, GLA_DV), F32),
        ],
        compiler_params=_cparams(("arbitrary", "arbitrary")),
        name="prompt_mixer",
    )(x, mod_p, w["g1"], w["win_t"], w["conv_w"], w["conv_b"], w["wgate"], w["lru_ba"], w["lru_bx"],
      w["lru_lambda"], w["proj_a"], w["wa2"], w["gla_ba"], w["gla_norm_g"], w["proj_b"], w["w_out"])


def _ffn_kernel(x_ref, mod_ref, g2_ref, w1_ref, w2_ref, fg_ref, o_ref, *, final_norm):
    x = x_ref[...]
    m = mod_ref[0]
    h2 = _norm_mod(x, g2_ref[...], _mod_part(m, 4), _mod_part(m, 3))
    f = _dot(h2, w1_ref[...])
    x2 = x + _mod_part(m, 5) * _dot(_silu(f[:, :D_FF]) * f[:, D_FF:], w2_ref[...])
    if final_norm:
        x2 = x2 * _rms(x2) * fg_ref[...]
    o_ref[...] = x2


def _ffn(x2d, mod3, rows_per_mod, tile, w, l, final_g, final_norm):
    mrows = x2d.shape[0]
    r = mod3.shape[1]
    per = rows_per_mod // tile
    row_spec = pl.BlockSpec((tile, D_MODEL), lambda i: (i, 0))
    return pl.pallas_call(
        functools.partial(_ffn_kernel, final_norm=final_norm),
        grid=(mrows // tile,),
        in_specs=[
            row_spec,
            pl.BlockSpec((1, r, N_MOD), lambda i: (i // per, 0, 0)),
            _wspec(l, (1, D_MODEL)),
            _wspec(l, (D_MODEL, 2 * D_FF)),
            _wspec(l, (D_FF, D_MODEL)),
            _full_spec((1, D_MODEL)),
        ],
        out_specs=row_spec,
        out_shape=jax.ShapeDtypeStruct((mrows, D_MODEL), F32),
        compiler_params=_cparams(("arbitrary",)),
        name="ffn_final" if final_norm else "ffn",
    )(x2d, mod3, w["g2"], w["ffn_w1"], w["ffn_w2"], final_g)


def _sample_lru_kernel(x_ref, mod_ref, g1_ref, win_ref, cw_ref, cb_ref, wg_ref, ba_ref, bx_ref, lam_ref, pa_ref,
                       conv0_ref, h0_ref, ya_ref, conv_ref, lru_ref):
    ns = h0_ref.shape[0]
    nt = x_ref.shape[0] // ns
    m = mod_ref[...]
    tile = lambda v: jnp.concatenate([v] * nt, axis=0)
    h = _norm_mod(x_ref[...], g1_ref[...], tile(_mod_part(m, 1)), tile(_mod_part(m, 0)))
    ux = _dot_nt(h, win_ref[W_X:W_G, :])
    ug = _dot_nt(h, win_ref[W_G:W_Q, :])
    xa = [conv0_ref[i] for i in range(CONV_W - 1)] + [ux[i * ns:(i + 1) * ns, :] for i in range(nt)]
    cw = cw_ref[...]
    xcs = []
    for ti in range(nt):
        acc = cb_ref[...]
        for i in range(CONV_W):
            acc = acc + xa[ti + i] * cw[i:i + 1, :]
        xcs.append(acc)
    xc = jnp.concatenate(xcs, axis=0)
    coeffs = [_lru_coeffs_block(xc[:, n * LRU_BW:(n + 1) * LRU_BW], wg_ref[n], ba_ref[:, n * LRU_BW:(n + 1) * LRU_BW],
                                bx_ref[:, n * LRU_BW:(n + 1) * LRU_BW], lam_ref[:, n * LRU_BW:(n + 1) * LRU_BW])
              for n in range(LRU_BLOCKS)]
    a = jnp.concatenate([c[0] for c in coeffs], axis=1)
    bb = jnp.concatenate([c[1] for c in coeffs], axis=1)
    hc = h0_ref[...]
    hs = []
    for ti in range(nt):
        rows = slice(ti * ns, (ti + 1) * ns)
        hc = a[rows, :] * hc + bb[rows, :]
        hs.append(hc)
    ya_ref[...] = _dot(jnp.concatenate(hs, axis=0) * _gelu_tanh(ug), pa_ref[...])
    for i in range(CONV_W - 1):
        conv_ref[i] = xa[nt + i]
    lru_ref[...] = hc


def _sample_lru(x_tm, mod_s, conv0_tm, h0_all, w, l):
    rows = x_tm.shape[0]
    ns = h0_all.shape[1]
    return pl.pallas_call(
        _sample_lru_kernel,
        grid=(1,),
        in_specs=[
            _full_spec((rows, D_MODEL)),
            _full_spec((ns, N_MOD)),
            _wspec(l, (1, D_MODEL)),
            _wspec(l, (W_Q, D_MODEL)),
            _wspec(l, (CONV_W, D_RNN)),
            _wspec(l, (1, D_RNN)),
            _wspec(l, (LRU_BLOCKS, LRU_BW, 2 * LRU_BW)),
            _wspec(l, (1, D_RNN)),
            _wspec(l, (1, D_RNN)),
            _wspec(l, (1, D_RNN)),
            _wspec(l, (D_RNN, D_MODEL)),
            _full_spec((CONV_W - 1, ns, D_RNN)),
            _wspec(l, (ns, D_RNN)),
        ],
        out_specs=[
            _full_out((rows, D_MODEL)),
            _full_out((CONV_W - 1, ns, D_RNN)),
            _full_out((ns, D_RNN)),
        ],
        out_shape=[
            jax.ShapeDtypeStruct((rows, D_MODEL), F32),
            jax.ShapeDtypeStruct((CONV_W - 1, ns, D_RNN), F32),
            jax.ShapeDtypeStruct((ns, D_RNN), F32),
        ],
        compiler_params=_cparams(("arbitrary",)),
        name="sample_lru",
    )(x_tm, mod_s, w["g1"], w["win_t"], w["conv_w"], w["conv_b"], w["wgate"], w["lru_ba"], w["lru_bx"],
      w["lru_lambda"], w["proj_a"], conv0_tm, h0_all)


def _row_regroup(rows, inner, outer):
    ri = lax.broadcasted_iota(jnp.int32, (rows, rows), 0)
    ci = lax.broadcasted_iota(jnp.int32, (rows, rows), 1)
    return jnp.where(ci == (ri % inner) * outer + ri // inner, 1.0, 0.0).astype(BF16)


def _sample_inproj_kernel(x_ref, mod_ref, g1_ref, win_ref, u_ref, ug_ref):
    rows = x_ref.shape[0]
    ns = mod_ref.shape[0]
    nt = rows // ns
    m = mod_ref[...]
    tile = lambda v: jnp.concatenate([v] * nt, axis=0)
    h = _norm_mod(x_ref[...], g1_ref[...], tile(_mod_part(m, 1)), tile(_mod_part(m, 0))).astype(BF16)
    ug_ref[...] = _dot_nt(h, win_ref[W_GA:N_IN, :])
    hs = jnp.dot(_row_regroup(rows, nt, ns), h, preferred_element_type=F32).astype(BF16)
    u_ref[...] = jnp.concatenate([_dot_nt(hs, win_ref[W_LR:W_LR + LANES, :]), _dot_nt(hs, win_ref[W_Q:W_LR, :])],
                                 axis=1)


def _sample_inproj(x_tm, mod_s, w, l):
    rows = x_tm.shape[0]
    return pl.pallas_call(
        _sample_inproj_kernel,
        grid=(1,),
        in_specs=[
            _full_spec((rows, D_MODEL)),
            _full_spec(mod_s.shape),
            _wspec(l, (1, D_MODEL)),
            _wspec(l, (N_IN, D_MODEL)),
        ],
        out_specs=[_full_out((rows, N_IN_B)), _full_out((rows, 2 * D_MODEL))],
        out_shape=[jax.ShapeDtypeStruct((rows, N_IN_B), F32), jax.ShapeDtypeStruct((rows, 2 * D_MODEL), F32)],
        compiler_params=_cparams(("arbitrary",)),
        name="sample_inproj",
    )(x_tm, mod_s, w["g1"], w["win_t"])


def _sample_gla_kernel(u_ref, wa2_ref, gba_ref, gng_ref, s0_ref, *rest, nt, layer, carries_buffer):
    if carries_buffer:
        yb_ref, st_ref, o_s, q_s, v_s, kdt_s, bl_s = rest[1:]
    else:
        yb_ref, st_all_ref, o_s, q_s, v_s, kdt_s, bl_s = rest
        st_ref = st_all_ref.at[layer]
        for other in range(st_all_ref.shape[0]):
            if other != layer:
                st_all_ref[other] = jnp.zeros(st_all_ref.shape[1:], F32)
    rows = u_ref.shape[0]
    steps_per_rows = rows // (SAMPLE_STATE_BLOCK * nt)
    sub = pl.program_id(0) % steps_per_rows
    rg = SAMPLE_ROW_GROUP
    seq_per_group = rg // nt

    @pl.when(sub == 0)
    def _():
        u = u_ref[...]
        g = _gla_gate(u[:, _LRO:_LRO + LANES], wa2_ref, gba_ref[...])
        bc = _group_cumsum(g, nt)
        rin = lax.broadcasted_iota(jnp.int32, g.shape, 0) % nt
        sfx = jnp.zeros_like(g)
        for k in range(1, nt):
            sfx = sfx + jnp.where(rin + k < nt, pltpu.roll(g, rows - k, 0), 0.0)
        qt = u[:, _QO:_QO + GLA_DK_TOTAL] * (jnp.exp(bc) * (GLA_DK ** -0.5))
        uk = u[:, _KO:_KO + GLA_DK_TOTAL]
        kt = uk * jnp.exp(-bc)
        kd = uk * jnp.exp(sfx)
        q_s[...] = qt.astype(BF16)
        v_s[...] = u[:, _VO:_VO + GLA_DV_TOTAL].astype(BF16)
        bl_s[...] = bc + sfx
        ri = lax.broadcasted_iota(jnp.int32, (rows, rows), 0)
        ci = lax.broadcasted_iota(jnp.int32, (rows, rows), 1)
        same_seq_causal = (ri // nt == ci // nt) & (ri >= ci)
        for hd in range(GLA_HEADS):
            ks = slice(hd * GLA_DK, (hd + 1) * GLA_DK)
            att = jnp.where(same_seq_causal, _dot_nt(qt[:, ks], kt[:, ks]), 0.0)
            o_s[:, hd * GLA_DV:(hd + 1) * GLA_DV] = _dot(att, v_s[:, hd * GLA_DV:(hd + 1) * GLA_DV])
            kdt_s[hd] = kd[:, ks].T

    lane_seq = lax.broadcasted_iota(jnp.int32, (GLA_DK, rows), 1) // nt
    row_seq = lax.broadcasted_iota(jnp.int32, (rg, GLA_DV), 0) // nt

    def group(gi, carry):
        r0 = pl.multiple_of((sub * (SAMPLE_STATE_BLOCK // seq_per_group) + gi) * rg, rg)
        dec_g = jnp.exp(bl_s[pl.ds(r0, rg), :])
        for hd in range(GLA_HEADS):
            ks = slice(hd * GLA_DK, (hd + 1) * GLA_DK)
            vsl = slice(hd * GLA_DV, (hd + 1) * GLA_DV)
            qg = q_s[pl.ds(r0, rg), ks]
            vh = v_s[:, vsl]
            inter = jnp.zeros((rg, GLA_DV), F32)
            for j in range(seq_per_group):
                s_loc = gi * seq_per_group + j
                s_row = sub * SAMPLE_STATE_BLOCK + s_loc
                s_old = s0_ref[s_loc, hd]
                inter = jnp.where(row_seq == j, _dot(qg, s_old), inter)
                kdt = jnp.where(lane_seq == s_row, kdt_s[hd], 0.0)
                dm = _col_bcast(dec_g[j * nt:j * nt + 1, ks])
                st_ref[s_loc, hd] = s_old * jnp.concatenate([dm, dm], axis=1) + _dot(kdt, vh)
            o_s[pl.ds(r0, rg), vsl] = o_s[pl.ds(r0, rg), vsl] + inter
        return carry

    lax.fori_loop(0, SAMPLE_STATE_BLOCK // seq_per_group, group, 0)

    @pl.when(sub == steps_per_rows - 1)
    def _():
        yb_ref[...] = _head_norm_gate(o_s[...], gng_ref[...], u_ref[:, _RO:_RO + D_MODEL])


def _sample_gla(u_sm, state_all, new_state_all, w, l, nt):
    rows = u_sm.shape[0]
    br = SAMPLE_ROW_BLOCK
    sb = SAMPLE_STATE_BLOCK
    per = br // (sb * nt)
    state_spec = pl.BlockSpec((None, sb, GLA_HEADS, GLA_DK, GLA_DV), lambda i: (l, i, 0, 0, 0))
    out_state_spec = state_spec
    if new_state_all is None:
        out_state_spec = pl.BlockSpec((state_all.shape[0], sb, GLA_HEADS, GLA_DK, GLA_DV), lambda i: (0, i, 0, 0, 0))
    in_specs = [
        pl.BlockSpec((br, N_IN_B), lambda i: (i // per, 0)),
        _wspec(l, (LANES, GLA_DK_TOTAL)),
        _wspec(l, (1, GLA_DK_TOTAL)),
        _wspec(l, (1, GLA_DV_TOTAL)),
        state_spec,
    ]
    args = [u_sm, w["wa2"], w["gla_ba"], w["gla_norm_g"], state_all]
    aliases = {}
    if new_state_all is not None:
        in_specs.append(pl.BlockSpec(memory_space=pl.ANY))
        args.append(new_state_all)
        aliases = {len(args) - 1: 1}
    return pl.pallas_call(
        functools.partial(_sample_gla_kernel, nt=nt, layer=l, carries_buffer=new_state_all is not None),
        grid=(rows // (sb * nt),),
        in_specs=in_specs,
        out_specs=[
            pl.BlockSpec((br, D_MODEL), lambda i: (i // per, 0)),
            out_state_spec,
        ],
        out_shape=[
            jax.ShapeDtypeStruct((rows, D_MODEL), F32),
            jax.ShapeDtypeStruct(state_all.shape, F32),
        ],
        input_output_aliases=aliases,
        scratch_shapes=[
            pltpu.VMEM((br, GLA_DV_TOTAL), F32),
            pltpu.VMEM((br, GLA_DK_TOTAL), BF16),
            pltpu.VMEM((br, GLA_DV_TOTAL), BF16),
            pltpu.VMEM((GLA_HEADS, GLA_DK, br), F32),
            pltpu.VMEM((br, GLA_DK_TOTAL), F32),
        ],
        compiler_params=_cparams(("arbitrary",)),
        name="sample_gla",
    )(*args)


def _sample_merge_kernel(x_ref, mod_ref, ug_ref, ya_ref, yb_ref, pb_ref, wo_ref, x1_ref):
    rows = x_ref.shape[0]
    ns = mod_ref.shape[0]
    nt = rows // ns
    gt1 = jnp.concatenate([_mod_part(mod_ref[...], 2)] * nt, axis=0)
    yb = jnp.dot(_row_regroup(rows, ns, nt), yb_ref[...].astype(BF16), preferred_element_type=F32).astype(BF16)
    x1_ref[...] = _merge_out(x_ref[...], gt1, ug_ref[:, :D_MODEL], ug_ref[:, D_MODEL:], ya_ref[...], yb, pb_ref, wo_ref)


def _sample_merge(x_tm, mod_s, ug_tm, ya, yb_sm, w, l):
    rows = x_tm.shape[0]
    return pl.pallas_call(
        _sample_merge_kernel,
        grid=(1,),
        in_specs=[
            _full_spec((rows, D_MODEL)),
            _full_spec(mod_s.shape),
            _full_spec((rows, 2 * D_MODEL)),
            _full_spec((rows, D_MODEL)),
            _full_spec((rows, D_MODEL)),
            _wspec(l, (GLA_DV_TOTAL, D_MODEL)),
            _wspec(l, (D_MODEL, D_MODEL)),
        ],
        out_specs=_full_out((rows, D_MODEL)),
        out_shape=jax.ShapeDtypeStruct((rows, D_MODEL), F32),
        compiler_params=_cparams(("arbitrary",)),
        name="sample_merge",
    )(x_tm, mod_s, ug_tm, ya, yb_sm, w["proj_b"], w["w_out"])


def _stacked_weights(norm1_g, norm2_g, w_in, conv_w, conv_b, lru_wa, lru_ba, lru_wx, lru_bx, lru_lambda,
                     gla_wa2, gla_ba, gla_norm_g, proj_a, proj_b, w_out, ffn_w1, ffn_w2):
    depth = w_in.shape[0]
    row = lambda v: v.reshape(depth, 1, -1)
    return {
        "g1": row(norm1_g), "g2": row(norm2_g),
        "win_t": jnp.swapaxes(w_in, 1, 2).astype(BF16),
        "conv_w": conv_w, "conv_b": row(conv_b),
        "wgate": jnp.concatenate([lru_wa, lru_wx], axis=-1).astype(BF16),
        "lru_ba": row(lru_ba), "lru_bx": row(lru_bx), "lru_lambda": row(lru_lambda),
        "wa2": jnp.pad(gla_wa2, ((0, 0), (0, LANES - GLA_RANK), (0, 0))).astype(BF16),
        "gla_ba": row(gla_ba), "gla_norm_g": row(gla_norm_g),
        "proj_a": proj_a.astype(BF16), "proj_b": proj_b.astype(BF16), "w_out": w_out.astype(BF16),
        "ffn_w1": ffn_w1.astype(BF16), "ffn_w2": ffn_w2.astype(BF16),
    }


def kernel(x_prompt, x_sample, c_prompt, c_sample, state_conv, state_lru, state_gla, norm1_g, norm2_g, ada_w, ada_b,
           w_in, conv_w, conv_b, lru_wa, lru_ba, lru_wx, lru_bx, lru_lambda, gla_wa2, gla_ba, gla_norm_g, proj_a,
           proj_b, w_out, ffn_w1, ffn_w2, final_g):
    bp, tp, _ = x_prompt.shape
    bs, ts, _ = x_sample.shape
    depth = w_in.shape[0]
    mod = _modulation(jnp.concatenate([c_prompt, c_sample], axis=0), ada_w, ada_b)
    w = _stacked_weights(norm1_g, norm2_g, w_in, conv_w, conv_b, lru_wa, lru_ba, lru_wx, lru_bx, lru_lambda,
                         gla_wa2, gla_ba, gla_norm_g, proj_a, proj_b, w_out, ffn_w1, ffn_w2)
    fg = final_g.reshape(1, D_MODEL)
    xp = x_prompt
    xs = x_sample.transpose(1, 0, 2).reshape(ts * bs, D_MODEL)
    conv_p, lru_p, gla_p, conv_s, lru_s = [], [], [], [], []
    gla_s = None
    for l in range(depth):
        last = l == depth - 1
        mod_p = mod[l, :bp].reshape(bp, 1, N_MOD)
        mod_s = mod[l, bp:]
        x1, cb, ht, st = _prompt_mixer(xp, mod_p, w, l)
        xp = _ffn(x1.reshape(bp * tp, D_MODEL), mod_p, tp, FFN_TILE, w, l, fg, last).reshape(bp, tp, D_MODEL)
        conv_p.append(cb)
        lru_p.append(ht.reshape(bp, D_RNN))
        gla_p.append(st)
        ya_s, cb_s, ht_s = _sample_lru(xs, mod_s, state_conv[l].transpose(1, 0, 2), state_lru, w, l)
        u_sm, ug_tm = _sample_inproj(xs, mod_s, w, l)
        yb_sm, gla_s = _sample_gla(u_sm, state_gla, gla_s, w, l, ts)
        x1_s = _sample_merge(xs, mod_s, ug_tm, ya_s, yb_sm, w, l)
        xs = _ffn(x1_s, mod_s.reshape(1, bs, N_MOD), bs * ts, bs, w, l, fg, last)
        conv_s.append(cb_s.transpose(1, 0, 2))
        lru_s.append(ht_s)
    y_sample = xs.reshape(ts, bs, D_MODEL).transpose(1, 0, 2)
    return (xp, y_sample, jnp.stack(conv_p), jnp.stack(lru_p), jnp.stack(gla_p),
            jnp.stack(conv_s), jnp.stack(lru_s), gla_s)
```

```python
import functools

import jax
import jax.numpy as jnp
from jax import lax
from jax.experimental import pallas as pl
from jax.experimental.pallas import tpu as pltpu

F32 = jnp.float32
BF16 = jnp.bfloat16

D_MODEL = 1024
D_RNN = 1280
LRU_BW = 128
LRU_BLOCKS = D_RNN // LRU_BW
CONV_W = 4
LRU_C = 8.0
GLA_HEADS = 4
GLA_DK = 128
GLA_DV = 256
GLA_DK_TOTAL = GLA_HEADS * GLA_DK
GLA_DV_TOTAL = GLA_HEADS * GLA_DV
GLA_RANK = 16
GLA_TAU = 16.0
D_FF = 2816
EPS = 1e-6
TINY_F32 = 1.1754944e-38
N_MOD = 6 * D_MODEL

LANES = 128
SUBLANES = 8
VMEM_LIMIT_BYTES = 60 * 1024 * 1024

W_X, W_G, W_Q, W_K, W_V, W_R, W_LR, W_GA, W_GB = 0, 1280, 2560, 3072, 3584, 4608, 5632, 5648, 6672
N_IN = W_GB + D_MODEL
_LRO, _QO, _KO, _VO, _RO = 0, 128, 640, 1152, 2176
N_IN_B = _RO + D_MODEL

PROMPT_TILE = 512
GLA_CHUNK = 128
FFN_TILE = 512
VRG_PIECE = 512
LRU_PAIR = 2 * LRU_BW
SAMPLE_ROW_BLOCK = 128
SAMPLE_STATE_BLOCK = 8
SAMPLE_ROW_GROUP = 16


def _cparams(sem):
    return pltpu.CompilerParams(dimension_semantics=sem, vmem_limit_bytes=VMEM_LIMIT_BYTES)


def _full_spec(shape):
    n = len(shape)
    return pl.BlockSpec(shape, lambda *_: (0,) * n, pipeline_mode=pl.Buffered(1))


def _full_out(shape):
    n = len(shape)
    return pl.BlockSpec(shape, lambda *_: (0,) * n)


def _wspec(l, shape):
    n = len(shape)
    return pl.BlockSpec((None,) + tuple(shape), lambda *_: (l,) + (0,) * n, pipeline_mode=pl.Buffered(1))


def _softplus(y):
    return jnp.maximum(y, 0.0) + jnp.log1p(jnp.exp(-jnp.abs(y)))


def _sigmoid(y):
    return 0.5 * jnp.tanh(0.5 * y) + 0.5


def _silu(y):
    t = 0.5 * y
    return t + t * jnp.tanh(t)


def _sqrt_nonneg(s):
    return s * lax.rsqrt(jnp.maximum(s, TINY_F32))


def _gelu_tanh(y):
    return 0.5 * y * (1.0 + jnp.tanh(0.7978845608028654 * (y + 0.044715 * (y * y * y))))


def _rms(x):
    return lax.rsqrt(jnp.mean(x * x, axis=-1, keepdims=True) + EPS)


def _norm_mod(x, g, scale, shift):
    return (x * _rms(x)) * (g * (1.0 + scale)) + shift


def _mod_part(m, i):
    return m[:, i * D_MODEL:(i + 1) * D_MODEL]


def _dot(a, b):
    return jnp.dot(a.astype(BF16), b.astype(BF16), preferred_element_type=F32)


def _dot_nt(a, b):
    return lax.dot_general(a.astype(BF16), b.astype(BF16), (((1,), (1,)), ((), ())),
                           preferred_element_type=F32)


def _lru_coeffs_block(xb, wg, ba, bx, lam):
    lamc = -LRU_C * _softplus(-lam)
    pre = _dot(xb, wg)
    r = _sigmoid(pre[:, :LRU_BW] + ba)
    i = _sigmoid(pre[:, LRU_BW:] + bx)
    a = jnp.exp(lamc * r)
    return a, _sqrt_nonneg(1.0 - a * a) * (i * xb)


def _gla_gate(ulr, wa2_ref, gba):
    z = _dot(ulr, wa2_ref[...]) + gba
    return (jnp.minimum(z, 0.0) - jnp.log(1.0 + jnp.exp(-jnp.abs(z)))) * (1.0 / GLA_TAU)


def _group_cumsum(g, group):
    rin = lax.broadcasted_iota(jnp.int32, g.shape, 0) % group
    x = g
    k = 1
    while k < group:
        x = x + jnp.where(rin >= k, pltpu.roll(x, k, 0), 0.0)
        k *= 2
    return x


def _col_bcast(row):
    return jnp.broadcast_to(row, (LANES, LANES)).T


def _head_norm_gate(o, gng, ur):
    parts = []
    for hd in range(GLA_HEADS):
        sl = slice(hd * GLA_DV, (hd + 1) * GLA_DV)
        oh = o[:, sl]
        parts.append(oh * _rms(oh) * gng[:, sl])
    return jnp.concatenate(parts, axis=1) * _silu(ur)


def _merge_out(x, gt1, uga, ugb, ya, yb, pb_ref, wo_ref):
    pbv = _dot(yb, pb_ref[...])
    mm = _sigmoid(uga) * ya + _sigmoid(ugb) * pbv
    return x + gt1 * _dot(mm, wo_ref[...])


def _mod_kernel(c_ref, w_ref, b_ref, o_ref):
    o_ref[0] = _dot(_silu(c_ref[...]), w_ref[0]) + b_ref[0]


def _modulation(c_all, ada_w, ada_b):
    depth = ada_w.shape[0]
    rows = c_all.shape[0]
    nblk = N_MOD // D_MODEL
    return pl.pallas_call(
        _mod_kernel,
        grid=(depth, nblk),
        in_specs=[
            pl.BlockSpec((rows, D_MODEL), lambda l, j: (0, 0)),
            pl.BlockSpec((1, D_MODEL, D_MODEL), lambda l, j: (l, 0, j)),
            pl.BlockSpec((1, 1, D_MODEL), lambda l, j: (l, 0, j)),
        ],
        out_specs=pl.BlockSpec((1, rows, D_MODEL), lambda l, j: (l, 0, j)),
        out_shape=jax.ShapeDtypeStruct((depth, rows, N_MOD), F32),
        compiler_params=_cparams(("arbitrary", "arbitrary")),
        name="modulation",
    )(c_all, ada_w, ada_b.reshape(depth, 1, N_MOD))


def _shift_rows_in(blk, first_row):
    top = lax.broadcasted_iota(jnp.int32, blk.shape, 0) == 0
    return jnp.where(top, first_row, pltpu.roll(blk, 1, 0))


def _regroup_matrices(tt):
    ng = tt // SUBLANES
    ri = lax.broadcasted_iota(jnp.int32, (tt, tt), 0)
    ci = lax.broadcasted_iota(jnp.int32, (tt, tt), 1)
    to_blocks = jnp.where(ci == SUBLANES * (ri % ng) + ri // ng, 1.0, 0.0).astype(BF16)
    from_blocks = jnp.where(ci == ng * (ri % SUBLANES) + ri // SUBLANES, 1.0, 0.0).astype(BF16)
    return to_blocks, from_blocks


def _lru_conv(ux, cw, cb, conv_rows):
    ng = ux.shape[0] // SUBLANES
    first_tail = SUBLANES - (CONV_W - 1)
    xs = [ux[j * ng:(j + 1) * ng, :] for j in range(SUBLANES)]
    prev = {k: _shift_rows_in(xs[k], conv_rows[k - first_tail:k - first_tail + 1, :])
            for k in range(first_tail, SUBLANES)}
    xcs = []
    for j in range(SUBLANES):
        acc = cb
        for i in range(CONV_W):
            d = CONV_W - 1 - i
            src = xs[j - d] if j >= d else prev[j - d + SUBLANES]
            acc = acc + src * cw[i:i + 1, :]
        xcs.append(acc)
    tail = jnp.concatenate([xs[k][ng - 1:ng, :] for k in range(first_tail, SUBLANES)], axis=0)
    return jnp.concatenate(xcs, axis=0), tail


def _lru_scan(a, bb, carry):
    ng = a.shape[0] // SUBLANES
    loc = [bb[0:ng, :]]
    cum = [a[0:ng, :]]
    for j in range(1, SUBLANES):
        aj = a[j * ng:(j + 1) * ng, :]
        loc.append(aj * loc[-1] + bb[j * ng:(j + 1) * ng, :])
        cum.append(aj * cum[-1])
    ga, gb = cum[-1], loc[-1]
    rowg = lax.broadcasted_iota(jnp.int32, ga.shape, 0)
    k = 1
    while k < ng:
        ga_sh = jnp.where(rowg >= k, pltpu.roll(ga, k, 0), 1.0)
        gb_sh = jnp.where(rowg >= k, pltpu.roll(gb, k, 0), 0.0)
        gb = ga * gb_sh + gb
        ga = ga * ga_sh
        k *= 2
    leaving = gb + ga * carry
    entering = _shift_rows_in(leaving, carry)
    hs = jnp.concatenate([loc[j] + cum[j] * entering for j in range(SUBLANES)], axis=0)
    return hs, leaving[ng - 1:ng, :]


def _lru_lane_block(n, uan, cw_ref, cb_ref, wg_ref, ba_ref, bx_ref, lam_ref, conv_s, hc_s):
    sl = slice(n * LRU_BW, (n + 1) * LRU_BW)
    xc, tail = _lru_conv(uan[0], cw_ref[:, sl], cb_ref[:, sl], conv_s[0:CONV_W - 1, sl])
    conv_s[0:CONV_W - 1, sl] = tail
    a, bb = _lru_coeffs_block(xc, wg_ref[n], ba_ref[:, sl], bx_ref[:, sl], lam_ref[:, sl])
    hs, last = _lru_scan(a, bb, hc_s[0:1, sl])
    hc_s[0:1, sl] = last
    return (hs * _gelu_tanh(uan[1])).astype(BF16)


def _gla_prep(ulr, uq, uk, wa2_ref, gba):
    g = _gla_gate(ulr, wa2_ref, gba)
    bc = _group_cumsum(g, GLA_CHUNK)
    qt = uq * (jnp.exp(bc) * (GLA_DK ** -0.5))
    kt = uk * jnp.exp(-bc)
    return bc, qt, kt


def _gla_chunks(bc, qt, kt, uk, uv, s_s):
    tt = bc.shape[0]
    c = GLA_CHUNK
    ri = lax.broadcasted_iota(jnp.int32, (c, c), 0)
    ci = lax.broadcasted_iota(jnp.int32, (c, c), 1)
    causal = ri >= ci
    o_rows = []
    for ch in range(tt // c):
        rows = slice(ch * c, (ch + 1) * c)
        bl = bc[ch * c + c - 1:ch * c + c, :]
        kd = uk[rows, :] * jnp.exp(bl - bc[rows, :])
        dec = jnp.exp(bl)
        o_heads = []
        for hd in range(GLA_HEADS):
            ks = slice(hd * GLA_DK, (hd + 1) * GLA_DK)
            qh = qt[rows, ks].astype(BF16)
            vh = uv[rows, hd * GLA_DV:(hd + 1) * GLA_DV].astype(BF16)
            att = jnp.where(causal, _dot_nt(qh, kt[rows, ks]), 0.0)
            s_old = s_s[hd]
            o_heads.append(_dot(att, vh) + _dot(qh, s_old))
            dm = _col_bcast(dec[:, ks])
            s_s[hd] = s_old * jnp.concatenate([dm, dm], axis=1) + _dot(kd[:, ks].T, vh)
        o_rows.append(jnp.concatenate(o_heads, axis=1))
    return jnp.concatenate(o_rows, axis=0)


def _prompt_mixer_kernel(x_ref, mod_ref, g1_ref, wt_ref, cw_ref, cb_ref, wg_ref, ba_ref, bx_ref, lam_ref,
                         pa_ref, wa2_ref, gba_ref, gng_ref, pb_ref, wo_ref,
                         x1_ref, conv_ref, lru_ref, st_ref, conv_s, hc_s, s_s, bc_s, qt_s, kt_s, uk_s, uvrg_s):
    t = pl.program_id(1)

    @pl.when(t == 0)
    def _():
        conv_s[...] = jnp.zeros_like(conv_s)
        hc_s[...] = jnp.zeros_like(hc_s)
        s_s[...] = jnp.zeros_like(s_s)

    x = x_ref[0]
    mod = mod_ref[0]
    hb16 = _norm_mod(x, g1_ref[...], _mod_part(mod, 1), _mod_part(mod, 0)).astype(BF16)
    to_blocks, from_blocks = _regroup_matrices(x.shape[0])
    u_qk = _dot_nt(hb16, wt_ref[W_Q:W_V, :])
    u_lr = _dot_nt(hb16, wt_ref[W_LR:W_LR + LANES, :])
    hb = jnp.dot(to_blocks, hb16, preferred_element_type=F32).astype(BF16)
    uk_s[...] = u_qk[:, GLA_DK_TOTAL:]
    bc_s[...], qt_s[...], kt_s[...] = _gla_prep(u_lr, u_qk[:, :GLA_DK_TOTAL], u_qk[:, GLA_DK_TOTAL:], wa2_ref,
                                                gba_ref[...])
    piece_rows = ([W_V + i * VRG_PIECE for i in range((W_LR - W_V) // VRG_PIECE)]
                  + [W_GA + i * VRG_PIECE for i in range((N_IN - W_GA) // VRG_PIECE)])
    per_pair = -(-len(piece_rows) // (LRU_BLOCKS // 2))
    ya_blocks = []
    for m in range(LRU_BLOCKS // 2):
        ux2 = _dot_nt(hb, wt_ref[W_X + m * LRU_PAIR:W_X + (m + 1) * LRU_PAIR, :])
        ug2 = _dot_nt(hb, wt_ref[W_G + m * LRU_PAIR:W_G + (m + 1) * LRU_PAIR, :])
        for i in range(m * per_pair, min((m + 1) * per_pair, len(piece_rows))):
            uvrg_s[:, i * VRG_PIECE:(i + 1) * VRG_PIECE] = _dot_nt(hb16, wt_ref[piece_rows[i]:piece_rows[i] + VRG_PIECE, :])
        for j in range(2):
            uan = (ux2[:, j * LRU_BW:(j + 1) * LRU_BW], ug2[:, j * LRU_BW:(j + 1) * LRU_BW])
            ya_blocks.append(_lru_lane_block(2 * m + j, uan, cw_ref, cb_ref, wg_ref, ba_ref, bx_ref, lam_ref,
                                             conv_s, hc_s))
    o = _gla_chunks(bc_s, qt_s, kt_s, uk_s, uvrg_s, s_s)
    ya = jnp.dot(from_blocks, jnp.concatenate(ya_blocks, axis=1), preferred_element_type=F32).astype(BF16)
    ya = jnp.dot(ya, pa_ref[...], preferred_element_type=F32)
    yb = _head_norm_gate(o, gng_ref[...], uvrg_s[:, GLA_DV_TOTAL:GLA_DV_TOTAL + D_MODEL])
    x1_ref[0] = _merge_out(x, _mod_part(mod, 2), uvrg_s[:, 2 * D_MODEL:3 * D_MODEL], uvrg_s[:, 3 * D_MODEL:],
                           ya, yb, pb_ref, wo_ref)

    @pl.when(t == pl.num_programs(1) - 1)
    def _():
        conv_ref[0] = conv_s[0:CONV_W - 1, :]
        lru_ref[0] = hc_s[0:1, :]
        st_ref[0] = s_s[...]


def _prompt_mixer(x, mod_p, w, l):
    b, t, _ = x.shape
    tt = PROMPT_TILE
    per_seq = t // tt
    seq_spec = pl.BlockSpec((1, tt, D_MODEL), lambda i, j: (i, j, 0))
    return pl.pallas_call(
        _prompt_mixer_kernel,
        grid=(b, per_seq),
        in_specs=[
            seq_spec,
            pl.BlockSpec((1, 1, N_MOD), lambda i, j: (i, 0, 0)),
            _wspec(l, (1, D_MODEL)),
            _wspec(l, (N_IN, D_MODEL)),
            _wspec(l, (CONV_W, D_RNN)),
            _wspec(l, (1, D_RNN)),
            _wspec(l, (LRU_BLOCKS, LRU_BW, 2 * LRU_BW)),
            _wspec(l, (1, D_RNN)),
            _wspec(l, (1, D_RNN)),
            _wspec(l, (1, D_RNN)),
            _wspec(l, (D_RNN, D_MODEL)),
            _wspec(l, (LANES, GLA_DK_TOTAL)),
            _wspec(l, (1, GLA_DK_TOTAL)),
            _wspec(l, (1, GLA_DV_TOTAL)),
            _wspec(l, (GLA_DV_TOTAL, D_MODEL)),
            _wspec(l, (D_MODEL, D_MODEL)),
        ],
        out_specs=[
            seq_spec,
            pl.BlockSpec((1, CONV_W - 1, D_RNN), lambda i, j: (i, 0, 0)),
            pl.BlockSpec((1, 1, D_RNN), lambda i, j: (i, 0, 0)),
            pl.BlockSpec((1, GLA_HEADS, GLA_DK, GLA_DV), lambda i, j: (i, 0, 0, 0)),
        ],
        out_shape=[
            jax.ShapeDtypeStruct((b, t, D_MODEL), F32),
            jax.ShapeDtypeStruct((b, CONV_W - 1, D_RNN), F32),
            jax.ShapeDtypeStruct((b, 1, D_RNN), F32),
            jax.ShapeDtypeStruct((b, GLA_HEADS, GLA_DK, GLA_DV), F32),
        ],
        scratch_shapes=[
            pltpu.VMEM((SUBLANES, D_RNN), F32),
            pltpu.VMEM((SUBLANES, D_RNN), F32),
            pltpu.VMEM((GLA_HEADS, GLA_DK, GLA_DV), F32),
            pltpu.VMEM((tt, GLA_DK_TOTAL), F32),
            pltpu.VMEM((tt, GLA_DK_TOTAL), F32),
            pltpu.VMEM((tt, GLA_DK_TOTAL), F32),
            pltpu.VMEM((tt, GLA_DK_TOTAL), F32),
            pltpu.VMEM((tt, 2 * GLA_DV_TOTAL + 2 * D_MODEL), F32),
        ],
        compiler_params=_cparams(("arbitrary", "arbitrary")),
        name="prompt_mixer",
    )(x, mod_p, w["g1"], w["win_t"], w["conv_w"], w["conv_b"], w["wgate"], w["lru_ba"], w["lru_bx"],
      w["lru_lambda"], w["proj_a"], w["wa2"], w["gla_ba"], w["gla_norm_g"], w["proj_b"], w["w_out"])


def _ffn_kernel(x_ref, mod_ref, g2_ref, w1_ref, w2_ref, fg_ref, o_ref, *, final_norm):
    x = x_ref[...]
    m = mod_ref[0]
    h2 = _norm_mod(x, g2_ref[...], _mod_part(m, 4), _mod_part(m, 3))
    f = _dot(h2, w1_ref[...])
    x2 = x + _mod_part(m, 5) * _dot(_silu(f[:, :D_FF]) * f[:, D_FF:], w2_ref[...])
    if final_norm:
        x2 = x2 * _rms(x2) * fg_ref[...]
    o_ref[...] = x2


def _ffn(x2d, mod3, rows_per_mod, tile, w, l, final_g, final_norm):
    mrows = x2d.shape[0]
    r = mod3.shape[1]
    per = rows_per_mod // tile
    row_spec = pl.BlockSpec((tile, D_MODEL), lambda i: (i, 0))
    return pl.pallas_call(
        functools.partial(_ffn_kernel, final_norm=final_norm),
        grid=(mrows // tile,),
        in_specs=[
            row_spec,
            pl.BlockSpec((1, r, N_MOD), lambda i: (i // per, 0, 0)),
            _wspec(l, (1, D_MODEL)),
            _wspec(l, (D_MODEL, 2 * D_FF)),
            _wspec(l, (D_FF, D_MODEL)),
            _full_spec((1, D_MODEL)),
        ],
        out_specs=row_spec,
        out_shape=jax.ShapeDtypeStruct((mrows, D_MODEL), F32),
        compiler_params=_cparams(("arbitrary",)),
        name="ffn_final" if final_norm else "ffn",
    )(x2d, mod3, w["g2"], w["ffn_w1"], w["ffn_w2"], final_g)


def _sample_lru_kernel(x_ref, mod_ref, g1_ref, win_ref, cw_ref, cb_ref, wg_ref, ba_ref, bx_ref, lam_ref, pa_ref,
                       conv0_ref, h0_ref, ya_ref, conv_ref, lru_ref):
    ns = h0_ref.shape[0]
    nt = x_ref.shape[0] // ns
    m = mod_ref[...]
    tile = lambda v: jnp.concatenate([v] * nt, axis=0)
    h = _norm_mod(x_ref[...], g1_ref[...], tile(_mod_part(m, 1)), tile(_mod_part(m, 0)))
    ux = _dot_nt(h, win_ref[W_X:W_G, :])
    ug = _dot_nt(h, win_ref[W_G:W_Q, :])
    xa = [conv0_ref[i] for i in range(CONV_W - 1)] + [ux[i * ns:(i + 1) * ns, :] for i in range(nt)]
    cw = cw_ref[...]
    xcs = []
    for ti in range(nt):
        acc = cb_ref[...]
        for i in range(CONV_W):
            acc = acc + xa[ti + i] * cw[i:i + 1, :]
        xcs.append(acc)
    xc = jnp.concatenate(xcs, axis=0)
    coeffs = [_lru_coeffs_block(xc[:, n * LRU_BW:(n + 1) * LRU_BW], wg_ref[n], ba_ref[:, n * LRU_BW:(n + 1) * LRU_BW],
                                bx_ref[:, n * LRU_BW:(n + 1) * LRU_BW], lam_ref[:, n * LRU_BW:(n + 1) * LRU_BW])
              for n in range(LRU_BLOCKS)]
    a = jnp.concatenate([c[0] for c in coeffs], axis=1)
    bb = jnp.concatenate([c[1] for c in coeffs], axis=1)
    hc = h0_ref[...]
    hs = []
    for ti in range(nt):
        rows = slice(ti * ns, (ti + 1) * ns)
        hc = a[rows, :] * hc + bb[rows, :]
        hs.append(hc)
    ya_ref[...] = _dot(jnp.concatenate(hs, axis=0) * _gelu_tanh(ug), pa_ref[...])
    for i in range(CONV_W - 1):
        conv_ref[i] = xa[nt + i]
    lru_ref[...] = hc


def _sample_lru(x_tm, mod_s, conv0_tm, h0_all, w, l):
    rows = x_tm.shape[0]
    ns = h0_all.shape[1]
    return pl.pallas_call(
        _sample_lru_kernel,
        grid=(1,),
        in_specs=[
            _full_spec((rows, D_MODEL)),
            _full_spec((ns, N_MOD)),
            _wspec(l, (1, D_MODEL)),
            _wspec(l, (W_Q, D_MODEL)),
            _wspec(l, (CONV_W, D_RNN)),
            _wspec(l, (1, D_RNN)),
            _wspec(l, (LRU_BLOCKS, LRU_BW, 2 * LRU_BW)),
            _wspec(l, (1, D_RNN)),
            _wspec(l, (1, D_RNN)),
            _wspec(l, (1, D_RNN)),
            _wspec(l, (D_RNN, D_MODEL)),
            _full_spec((CONV_W - 1, ns, D_RNN)),
            _wspec(l, (ns, D_RNN)),
        ],
        out_specs=[
            _full_out((rows, D_MODEL)),
            _full_out((CONV_W - 1, ns, D_RNN)),
            _full_out((ns, D_RNN)),
        ],
        out_shape=[
            jax.ShapeDtypeStruct((rows, D_MODEL), F32),
            jax.ShapeDtypeStruct((CONV_W - 1, ns, D_RNN), F32),
            jax.ShapeDtypeStruct((ns, D_RNN), F32),
        ],
        compiler_params=_cparams(("arbitrary",)),
        name="sample_lru",
    )(x_tm, mod_s, w["g1"], w["win_t"], w["conv_w"], w["conv_b"], w["wgate"], w["lru_ba"], w["lru_bx"],
      w["lru_lambda"], w["proj_a"], conv0_tm, h0_all)


def _row_regroup(rows, inner, outer):
    ri = lax.broadcasted_iota(jnp.int32, (rows, rows), 0)
    ci = lax.broadcasted_iota(jnp.int32, (rows, rows), 1)
    return jnp.where(ci == (ri % inner) * outer + ri // inner, 1.0, 0.0).astype(BF16)


def _sample_inproj_kernel(x_ref, mod_ref, g1_ref, win_ref, u_ref, ug_ref):
    rows = x_ref.shape[0]
    ns = mod_ref.shape[0]
    nt = rows // ns
    m = mod_ref[...]
    tile = lambda v: jnp.concatenate([v] * nt, axis=0)
    h = _norm_mod(x_ref[...], g1_ref[...], tile(_mod_part(m, 1)), tile(_mod_part(m, 0))).astype(BF16)
    ug_ref[...] = _dot_nt(h, win_ref[W_GA:N_IN, :])
    hs = jnp.dot(_row_regroup(rows, nt, ns), h, preferred_element_type=F32).astype(BF16)
    u_ref[...] = jnp.concatenate([_dot_nt(hs, win_ref[W_LR:W_LR + LANES, :]), _dot_nt(hs, win_ref[W_Q:W_LR, :])],
                                 axis=1)


def _sample_inproj(x_tm, mod_s, w, l):
    rows = x_tm.shape[0]
    return pl.pallas_call(
        _sample_inproj_kernel,
        grid=(1,),
        in_specs=[
            _full_spec((rows, D_MODEL)),
            _full_spec(mod_s.shape),
            _wspec(l, (1, D_MODEL)),
            _wspec(l, (N_IN, D_MODEL)),
        ],
        out_specs=[_full_out((rows, N_IN_B)), _full_out((rows, 2 * D_MODEL))],
        out_shape=[jax.ShapeDtypeStruct((rows, N_IN_B), F32), jax.ShapeDtypeStruct((rows, 2 * D_MODEL), F32)],
        compiler_params=_cparams(("arbitrary",)),
        name="sample_inproj",
    )(x_tm, mod_s, w["g1"], w["win_t"])


def _sample_gla_kernel(u_ref, wa2_ref, gba_ref, gng_ref, s0_ref, *rest, nt, layer, carries_buffer):
    if carries_buffer:
        yb_ref, st_ref, o_s, q_s, v_s, kdt_s, bl_s = rest[1:]
    else:
        yb_ref, st_all_ref, o_s, q_s, v_s, kdt_s, bl_s = rest
        st_ref = st_all_ref.at[layer]
        for other in range(st_all_ref.shape[0]):
            if other != layer:
                st_all_ref[other] = jnp.zeros(st_all_ref.shape[1:], F32)
    rows = u_ref.shape[0]
    steps_per_rows = rows // (SAMPLE_STATE_BLOCK * nt)
    sub = pl.program_id(0) % steps_per_rows
    rg = SAMPLE_ROW_GROUP
    seq_per_group = rg // nt

    @pl.when(sub == 0)
    def _():
        u = u_ref[...]
        g = _gla_gate(u[:, _LRO:_LRO + LANES], wa2_ref, gba_ref[...])
        bc = _group_cumsum(g, nt)
        rin = lax.broadcasted_iota(jnp.int32, g.shape, 0) % nt
        sfx = jnp.zeros_like(g)
        for k in range(1, nt):
            sfx = sfx + jnp.where(rin + k < nt, pltpu.roll(g, rows - k, 0), 0.0)
        qt = u[:, _QO:_QO + GLA_DK_TOTAL] * (jnp.exp(bc) * (GLA_DK ** -0.5))
        uk = u[:, _KO:_KO + GLA_DK_TOTAL]
        kt = uk * jnp.exp(-bc)
        kd = uk * jnp.exp(sfx)
        q_s[...] = qt.astype(BF16)
        v_s[...] = u[:, _VO:_VO + GLA_DV_TOTAL].astype(BF16)
        bl_s[...] = bc + sfx
        ri = lax.broadcasted_iota(jnp.int32, (rows, rows), 0)
        ci = lax.broadcasted_iota(jnp.int32, (rows, rows), 1)
        same_seq_causal = (ri // nt == ci // nt) & (ri >= ci)
        for hd in range(GLA_HEADS):
            ks = slice(hd * GLA_DK, (hd + 1) * GLA_DK)
            att = jnp.where(same_seq_causal, _dot_nt(qt[:, ks], kt[:, ks]), 0.0)
            o_s[:, hd * GLA_DV:(hd + 1) * GLA_DV] = _dot(att, v_s[:, hd * GLA_DV:(hd + 1) * GLA_DV])
            kdt_s[hd] = kd[:, ks].T

    lane_seq = lax.broadcasted_iota(jnp.int32, (GLA_DK, rows), 1) // nt
    row_seq = lax.broadcasted_iota(jnp.int32, (rg, GLA_DV), 0) // nt

    def group(gi, carry):
        r0 = pl.multiple_of((sub * (SAMPLE_STATE_BLOCK // seq_per_group) + gi) * rg, rg)
        dec_g = jnp.exp(bl_s[pl.ds(r0, rg), :])
        for hd in range(GLA_HEADS):
            ks = slice(hd * GLA_DK, (hd + 1) * GLA_DK)
            vsl = slice(hd * GLA_DV, (hd + 1) * GLA_DV)
            qg = q_s[pl.ds(r0, rg), ks]
            vh = v_s[:, vsl]
            inter = jnp.zeros((rg, GLA_DV), F32)
            for j in range(seq_per_group):
                s_loc = gi * seq_per_group + j
                s_row = sub * SAMPLE_STATE_BLOCK + s_loc
                s_old = s0_ref[s_loc, hd]
                inter = jnp.where(row_seq == j, _dot(qg, s_old), inter)
                kdt = jnp.where(lane_seq == s_row, kdt_s[hd], 0.0)
                dm = _col_bcast(dec_g[j * nt:j * nt + 1, ks])
                st_ref[s_loc, hd] = s_old * jnp.concatenate([dm, dm], axis=1) + _dot(kdt, vh)
            o_s[pl.ds(r0, rg), vsl] = o_s[pl.ds(r0, rg), vsl] + inter
        return carry

    lax.fori_loop(0, SAMPLE_STATE_BLOCK // seq_per_group, group, 0)

    @pl.when(sub == steps_per_rows - 1)
    def _():
        yb_ref[...] = _head_norm_gate(o_s[...], gng_ref[...], u_ref[:, _RO:_RO + D_MODEL])


def _sample_gla(u_sm, state_all, new_state_all, w, l, nt):
    rows = u_sm.shape[0]
    br = SAMPLE_ROW_BLOCK
    sb = SAMPLE_STATE_BLOCK
    per = br // (sb * nt)
    state_spec = pl.BlockSpec((None, sb, GLA_HEADS, GLA_DK, GLA_DV), lambda i: (l, i, 0, 0, 0))
    out_state_spec = state_spec
    if new_state_all is None:
        out_state_spec = pl.BlockSpec((state_all.shape[0], sb, GLA_HEADS, GLA_DK, GLA_DV), lambda i: (0, i, 0, 0, 0))
    in_specs = [
        pl.BlockSpec((br, N_IN_B), lambda i: (i // per, 0)),
        _wspec(l, (LANES, GLA_DK_TOTAL)),
        _wspec(l, (1, GLA_DK_TOTAL)),
        _wspec(l, (1, GLA_DV_TOTAL)),
        state_spec,
    ]
    args = [u_sm, w["wa2"], w["gla_ba"], w["gla_norm_g"], state_all]
    aliases = {}
    if new_state_all is not None:
        in_specs.append(pl.BlockSpec(memory_space=pl.ANY))
        args.append(new_state_all)
        aliases = {len(args) - 1: 1}
    return pl.pallas_call(
        functools.partial(_sample_gla_kernel, nt=nt, layer=l, carries_buffer=new_state_all is not None),
        grid=(rows // (sb * nt),),
        in_specs=in_specs,
        out_specs=[
            pl.BlockSpec((br, D_MODEL), lambda i: (i // per, 0)),
            out_state_spec,
        ],
        out_shape=[
            jax.ShapeDtypeStruct((rows, D_MODEL), F32),
            jax.ShapeDtypeStruct(state_all.shape, F32),
        ],
        input_output_aliases=aliases,
        scratch_shapes=[
            pltpu.VMEM((br, GLA_DV_TOTAL), F32),
            pltpu.VMEM((br, GLA_DK_TOTAL), BF16),
            pltpu.VMEM((br, GLA_DV_TOTAL), BF16),
            pltpu.VMEM((GLA_HEADS, GLA_DK, br), F32),
            pltpu.VMEM((br, GLA_DK_TOTAL), F32),
        ],
        compiler_params=_cparams(("arbitrary",)),
        name="sample_gla",
    )(*args)


def _sample_merge_kernel(x_ref, mod_ref, ug_ref, ya_ref, yb_ref, pb_ref, wo_ref, x1_ref):
    rows = x_ref.shape[0]
    ns = mod_ref.shape[0]
    nt = rows // ns
    gt1 = jnp.concatenate([_mod_part(mod_ref[...], 2)] * nt, axis=0)
    yb = jnp.dot(_row_regroup(rows, ns, nt), yb_ref[...].astype(BF16), preferred_element_type=F32).astype(BF16)
    x1_ref[...] = _merge_out(x_ref[...], gt1, ug_ref[:, :D_MODEL], ug_ref[:, D_MODEL:], ya_ref[...], yb, pb_ref, wo_ref)


def _sample_merge(x_tm, mod_s, ug_tm, ya, yb_sm, w, l):
    rows = x_tm.shape[0]
    return pl.pallas_call(
        _sample_merge_kernel,
        grid=(1,),
        in_specs=[
            _full_spec((rows, D_MODEL)),
            _full_spec(mod_s.shape),
            _full_spec((rows, 2 * D_MODEL)),
            _full_spec((rows, D_MODEL)),
            _full_spec((rows, D_MODEL)),
            _wspec(l, (GLA_DV_TOTAL, D_MODEL)),
            _wspec(l, (D_MODEL, D_MODEL)),
        ],
        out_specs=_full_out((rows, D_MODEL)),
        out_shape=jax.ShapeDtypeStruct((rows, D_MODEL), F32),
        compiler_params=_cparams(("arbitrary",)),
        name="sample_merge",
    )(x_tm, mod_s, ug_tm, ya, yb_sm, w["proj_b"], w["w_out"])


def _stacked_weights(norm1_g, norm2_g, w_in, conv_w, conv_b, lru_wa, lru_ba, lru_wx, lru_bx, lru_lambda,
                     gla_wa2, gla_ba, gla_norm_g, proj_a, proj_b, w_out, ffn_w1, ffn_w2):
    depth = w_in.shape[0]
    row = lambda v: v.reshape(depth, 1, -1)
    return {
        "g1": row(norm1_g), "g2": row(norm2_g),
        "win_t": jnp.swapaxes(w_in, 1, 2).astype(BF16),
        "conv_w": conv_w, "conv_b": row(conv_b),
        "wgate": jnp.concatenate([lru_wa, lru_wx], axis=-1).astype(BF16),
        "lru_ba": row(lru_ba), "lru_bx": row(lru_bx), "lru_lambda": row(lru_lambda),
        "wa2": jnp.pad(gla_wa2, ((0, 0), (0, LANES - GLA_RANK), (0, 0))).astype(BF16),
        "gla_ba": row(gla_ba), "gla_norm_g": row(gla_norm_g),
        "proj_a": proj_a.astype(BF16), "proj_b": proj_b.astype(BF16), "w_out": w_out.astype(BF16),
        "ffn_w1": ffn_w1.astype(BF16), "ffn_w2": ffn_w2.astype(BF16),
    }


def kernel(x_prompt, x_sample, c_prompt, c_sample, state_conv, state_lru, state_gla, norm1_g, norm2_g, ada_w, ada_b,
           w_in, conv_w, conv_b, lru_wa, lru_ba, lru_wx, lru_bx, lru_lambda, gla_wa2, gla_ba, gla_norm_g, proj_a,
           proj_b, w_out, ffn_w1, ffn_w2, final_g):
    bp, tp, _ = x_prompt.shape
    bs, ts, _ = x_sample.shape
    depth = w_in.shape[0]
    mod = _modulation(jnp.concatenate([c_prompt, c_sample], axis=0), ada_w, ada_b)
    w = _stacked_weights(norm1_g, norm2_g, w_in, conv_w, conv_b, lru_wa, lru_ba, lru_wx, lru_bx, lru_lambda,
                         gla_wa2, gla_ba, gla_norm_g, proj_a, proj_b, w_out, ffn_w1, ffn_w2)
    fg = final_g.reshape(1, D_MODEL)
    xp = x_prompt
    xs = x_sample.transpose(1, 0, 2).reshape(ts * bs, D_MODEL)
    conv_p, lru_p, gla_p, conv_s, lru_s = [], [], [], [], []
    gla_s = None
    for l in range(depth):
        last = l == depth - 1
        mod_p = mod[l, :bp].reshape(bp, 1, N_MOD)
        mod_s = mod[l, bp:]
        x1, cb, ht, st = _prompt_mixer(xp, mod_p, w, l)
        xp = _ffn(x1.reshape(bp * tp, D_MODEL), mod_p, tp, FFN_TILE, w, l, fg, last).reshape(bp, tp, D_MODEL)
        conv_p.append(cb)
        lru_p.append(ht.reshape(bp, D_RNN))
        gla_p.append(st)
        ya_s, cb_s, ht_s = _sample_lru(xs, mod_s, state_conv[l].transpose(1, 0, 2), state_lru, w, l)
        u_sm, ug_tm = _sample_inproj(xs, mod_s, w, l)
        yb_sm, gla_s = _sample_gla(u_sm, state_gla, gla_s, w, l, ts)
        x1_s = _sample_merge(xs, mod_s, ug_tm, ya_s, yb_sm, w, l)
        xs = _ffn(x1_s, mod_s.reshape(1, bs, N_MOD), bs * ts, bs, w, l, fg, last)
        conv_s.append(cb_s.transpose(1, 0, 2))
        lru_s.append(ht_s)
    y_sample = xs.reshape(ts, bs, D_MODEL).transpose(1, 0, 2)
    return (xp, y_sample, jnp.stack(conv_p), jnp.stack(lru_p), jnp.stack(gla_p),
            jnp.stack(conv_s), jnp.stack(lru_s), gla_s)
```

```python
import functools

import jax
import jax.numpy as jnp
from jax import lax
from jax.experimental import pallas as pl
from jax.experimental.pallas import tpu as pltpu

F32 = jnp.float32
BF16 = jnp.bfloat16

D_MODEL = 1024
D_RNN = 1280
LRU_BW = 128
LRU_BLOCKS = D_RNN // LRU_BW
CONV_W = 4
LRU_C = 8.0
GLA_HEADS = 4
GLA_DK = 128
GLA_DV = 256
GLA_DK_TOTAL = GLA_HEADS * GLA_DK
GLA_DV_TOTAL = GLA_HEADS * GLA_DV
GLA_RANK = 16
GLA_TAU = 16.0
D_FF = 2816
EPS = 1e-6
TINY_F32 = 1.1754944e-38
N_MOD = 6 * D_MODEL

LANES = 128
SUBLANES = 8
VMEM_LIMIT_BYTES = 60 * 1024 * 1024

W_X, W_G, W_Q, W_K, W_V, W_R, W_LR, W_GA, W_GB = 0, 1280, 2560, 3072, 3584, 4608, 5632, 5648, 6672
N_IN = W_GB + D_MODEL
_LRO, _QO, _KO, _VO, _RO = 0, 128, 640, 1152, 2176
N_IN_B = _RO + D_MODEL

PROMPT_TILE = 512
GLA_CHUNK = 128
GLA_SPLITS = (64, 32)
GLA_FINE = 32
FFN_TILE = 512
VRG_PIECE = 512
LRU_PAIR = 2 * LRU_BW
SAMPLE_ROW_BLOCK = 128
SAMPLE_STATE_BLOCK = 8
SAMPLE_ROW_GROUP = 16


def _cparams(sem):
    return pltpu.CompilerParams(dimension_semantics=sem, vmem_limit_bytes=VMEM_LIMIT_BYTES)


def _full_spec(shape):
    n = len(shape)
    return pl.BlockSpec(shape, lambda *_: (0,) * n, pipeline_mode=pl.Buffered(1))


def _full_out(shape):
    n = len(shape)
    return pl.BlockSpec(shape, lambda *_: (0,) * n)


def _wspec(l, shape):
    n = len(shape)
    return pl.BlockSpec((None,) + tuple(shape), lambda *_: (l,) + (0,) * n, pipeline_mode=pl.Buffered(1))


def _softplus(y):
    return jnp.maximum(y, 0.0) + jnp.log1p(jnp.exp(-jnp.abs(y)))


def _sigmoid(y):
    return 0.5 * jnp.tanh(0.5 * y) + 0.5


def _silu(y):
    t = 0.5 * y
    return t + t * jnp.tanh(t)


def _sqrt_nonneg(s):
    return s * lax.rsqrt(jnp.maximum(s, TINY_F32))


def _gelu_tanh(y):
    return 0.5 * y * (1.0 + jnp.tanh(0.7978845608028654 * (y + 0.044715 * (y * y * y))))


def _rms(x):
    return lax.rsqrt(jnp.mean(x * x, axis=-1, keepdims=True) + EPS)


def _norm_mod(x, g, scale, shift):
    return (x * _rms(x)) * (g * (1.0 + scale)) + shift


def _mod_part(m, i):
    return m[:, i * D_MODEL:(i + 1) * D_MODEL]


def _dot(a, b):
    return jnp.dot(a.astype(BF16), b.astype(BF16), preferred_element_type=F32)


def _dot_nt(a, b):
    return lax.dot_general(a.astype(BF16), b.astype(BF16), (((1,), (1,)), ((), ())),
                           preferred_element_type=F32)


def _lru_coeffs_block(xb, wg, ba, bx, lam):
    lamc = -LRU_C * _softplus(-lam)
    pre = _dot(xb, wg)
    r = _sigmoid(pre[:, :LRU_BW] + ba)
    i = _sigmoid(pre[:, LRU_BW:] + bx)
    a = jnp.exp(lamc * r)
    return a, _sqrt_nonneg(1.0 - a * a) * (i * xb)


def _gla_gate(ulr, wa2_ref, gba):
    z = _dot(ulr, wa2_ref[...]) + gba
    return (jnp.minimum(z, 0.0) - jnp.log(1.0 + jnp.exp(-jnp.abs(z)))) * (1.0 / GLA_TAU)


def _group_cumsum(g, group):
    rin = lax.broadcasted_iota(jnp.int32, g.shape, 0) % group
    x = g
    k = 1
    while k < group:
        x = x + jnp.where(rin >= k, pltpu.roll(x, k, 0), 0.0)
        k *= 2
    return x


def _col_bcast(row):
    return jnp.broadcast_to(row, (LANES, LANES)).T


def _head_norm_gate(o, gng, ur):
    parts = []
    for hd in range(GLA_HEADS):
        sl = slice(hd * GLA_DV, (hd + 1) * GLA_DV)
        oh = o[:, sl]
        parts.append(oh * _rms(oh) * gng[:, sl])
    return jnp.concatenate(parts, axis=1) * _silu(ur)


def _merge_out(x, gt1, uga, ugb, ya, yb, pb_ref, wo_ref):
    pbv = _dot(yb, pb_ref[...])
    mm = _sigmoid(uga) * ya + _sigmoid(ugb) * pbv
    return x + gt1 * _dot(mm, wo_ref[...])


def _mod_kernel(c_ref, w_ref, b_ref, o_ref):
    o_ref[0] = _dot(_silu(c_ref[...]), w_ref[0]) + b_ref[0]


def _modulation(c_all, ada_w, ada_b):
    depth = ada_w.shape[0]
    rows = c_all.shape[0]
    nblk = N_MOD // D_MODEL
    return pl.pallas_call(
        _mod_kernel,
        grid=(depth, nblk),
        in_specs=[
            pl.BlockSpec((rows, D_MODEL), lambda l, j: (0, 0)),
            pl.BlockSpec((1, D_MODEL, D_MODEL), lambda l, j: (l, 0, j)),
            pl.BlockSpec((1, 1, D_MODEL), lambda l, j: (l, 0, j)),
        ],
        out_specs=pl.BlockSpec((1, rows, D_MODEL), lambda l, j: (l, 0, j)),
        out_shape=jax.ShapeDtypeStruct((depth, rows, N_MOD), F32),
        compiler_params=_cparams(("arbitrary", "arbitrary")),
        name="modulation",
    )(c_all, ada_w, ada_b.reshape(depth, 1, N_MOD))


def _shift_rows_in(blk, first_row):
    top = lax.broadcasted_iota(jnp.int32, blk.shape, 0) == 0
    return jnp.where(top, first_row, pltpu.roll(blk, 1, 0))


def _regroup_matrices(tt):
    ng = tt // SUBLANES
    ri = lax.broadcasted_iota(jnp.int32, (tt, tt), 0)
    ci = lax.broadcasted_iota(jnp.int32, (tt, tt), 1)
    to_blocks = jnp.where(ci == SUBLANES * (ri % ng) + ri // ng, 1.0, 0.0).astype(BF16)
    from_blocks = jnp.where(ci == ng * (ri % SUBLANES) + ri // SUBLANES, 1.0, 0.0).astype(BF16)
    return to_blocks, from_blocks


def _lru_conv(ux, cw, cb, conv_rows):
    ng = ux.shape[0] // SUBLANES
    first_tail = SUBLANES - (CONV_W - 1)
    xs = [ux[j * ng:(j + 1) * ng, :] for j in range(SUBLANES)]
    prev = {k: _shift_rows_in(xs[k], conv_rows[k - first_tail:k - first_tail + 1, :])
            for k in range(first_tail, SUBLANES)}
    xcs = []
    for j in range(SUBLANES):
        acc = cb
        for i in range(CONV_W):
            d = CONV_W - 1 - i
            src = xs[j - d] if j >= d else prev[j - d + SUBLANES]
            acc = acc + src * cw[i:i + 1, :]
        xcs.append(acc)
    tail = jnp.concatenate([xs[k][ng - 1:ng, :] for k in range(first_tail, SUBLANES)], axis=0)
    return jnp.concatenate(xcs, axis=0), tail


def _lru_scan(a, bb, carry):
    ng = a.shape[0] // SUBLANES
    loc = [bb[0:ng, :]]
    cum = [a[0:ng, :]]
    for j in range(1, SUBLANES):
        aj = a[j * ng:(j + 1) * ng, :]
        loc.append(aj * loc[-1] + bb[j * ng:(j + 1) * ng, :])
        cum.append(aj * cum[-1])
    ga, gb = cum[-1], loc[-1]
    rowg = lax.broadcasted_iota(jnp.int32, ga.shape, 0)
    k = 1
    while k < ng:
        ga_sh = jnp.where(rowg >= k, pltpu.roll(ga, k, 0), 1.0)
        gb_sh = jnp.where(rowg >= k, pltpu.roll(gb, k, 0), 0.0)
        gb = ga * gb_sh + gb
        ga = ga * ga_sh
        k *= 2
    leaving = gb + ga * carry
    entering = _shift_rows_in(leaving, carry)
    hs = jnp.concatenate([loc[j] + cum[j] * entering for j in range(SUBLANES)], axis=0)
    return hs, leaving[ng - 1:ng, :]


def _lru_lane_block(n, uan, cw_ref, cb_ref, wg_ref, ba_ref, bx_ref, lam_ref, conv_s, hc_s):
    sl = slice(n * LRU_BW, (n + 1) * LRU_BW)
    xc, tail = _lru_conv(uan[0], cw_ref[:, sl], cb_ref[:, sl], conv_s[0:CONV_W - 1, sl])
    conv_s[0:CONV_W - 1, sl] = tail
    a, bb = _lru_coeffs_block(xc, wg_ref[n], ba_ref[:, sl], bx_ref[:, sl], lam_ref[:, sl])
    hs, last = _lru_scan(a, bb, hc_s[0:1, sl])
    hc_s[0:1, sl] = last
    return (hs * _gelu_tanh(uan[1])).astype(BF16)


def _rows_per_block(x, size, pick):
    parts = []
    for b in range(x.shape[0] // size):
        r = pick(b)
        row = jnp.zeros((1, x.shape[1]), x.dtype) if r is None else x[r:r + 1, :]
        parts.append(jnp.broadcast_to(row, (size, x.shape[1])))
    return jnp.concatenate(parts, axis=0)


def _gla_prep(ulr, uq, uk, wa2_ref, gba):
    c = GLA_CHUNK
    g = _gla_gate(ulr, wa2_ref, gba)
    bc = _group_cumsum(g, c)
    qs = uq * (GLA_DK ** -0.5)
    factors = []
    for size in GLA_SPLITS:
        before = _rows_per_block(bc, size, lambda b: None if (b * size) % c == 0 else b * size - 1)
        last = _rows_per_block(bc, size, lambda b: b * size + size - 1)
        factors.append((qs * jnp.exp(bc - before), uk * jnp.exp(last - bc)))
    mid = _rows_per_block(bc, GLA_FINE, lambda b: b * GLA_FINE + GLA_FINE // 2 - 1)
    factors.append((qs * jnp.exp(bc - mid), uk * jnp.exp(mid - bc)))
    return bc, qs * jnp.exp(bc), factors


def _gla_level_masks():
    c = GLA_CHUNK
    ri = lax.broadcasted_iota(jnp.int32, (c, c), 0)
    ci = lax.broadcasted_iota(jnp.int32, (c, c), 1)
    masks = [(ri // (2 * size) == ci // (2 * size)) & ((ri // size) % 2 == 1) & ((ci // size) % 2 == 0)
             for size in GLA_SPLITS]
    masks.append((ri // GLA_FINE == ci // GLA_FINE) & (ri >= ci))
    return masks


def _gla_chunks(bc, qt, factors, uk, uv, s_s):
    tt = bc.shape[0]
    c = GLA_CHUNK
    masks = _gla_level_masks()
    o_rows = []
    for ch in range(tt // c):
        rows = slice(ch * c, (ch + 1) * c)
        bl = bc[ch * c + c - 1:ch * c + c, :]
        kd = uk[rows, :] * jnp.exp(bl - bc[rows, :])
        dec = jnp.exp(bl)
        o_heads = []
        for hd in range(GLA_HEADS):
            ks = slice(hd * GLA_DK, (hd + 1) * GLA_DK)
            qh = qt[rows, ks].astype(BF16)
            vh = uv[rows, hd * GLA_DV:(hd + 1) * GLA_DV].astype(BF16)
            att = jnp.zeros((c, c), F32)
            for (qf, kf), mask in zip(factors, masks):
                att = jnp.where(mask, _dot_nt(qf[rows, ks], kf[rows, ks]), att)
            s_old = s_s[hd]
            o_heads.append(_dot(att, vh) + _dot(qh, s_old))
            dm = _col_bcast(dec[:, ks])
            s_s[hd] = s_old * jnp.concatenate([dm, dm], axis=1) + _dot(kd[:, ks].T, vh)
        o_rows.append(jnp.concatenate(o_heads, axis=1))
    return jnp.concatenate(o_rows, axis=0)


def _prompt_mixer_kernel(x_ref, mod_ref, g1_ref, wt_ref, cw_ref, cb_ref, wg_ref, ba_ref, bx_ref, lam_ref,
                         pa_ref, wa2_ref, gba_ref, gng_ref, pb_ref, wo_ref,
                         x1_ref, conv_ref, lru_ref, st_ref, conv_s, hc_s, s_s, uvrg_s):
    t = pl.program_id(1)

    @pl.when(t == 0)
    def _():
        conv_s[...] = jnp.zeros_like(conv_s)
        hc_s[...] = jnp.zeros_like(hc_s)
        s_s[...] = jnp.zeros_like(s_s)

    x = x_ref[0]
    mod = mod_ref[0]
    hb16 = _norm_mod(x, g1_ref[...], _mod_part(mod, 1), _mod_part(mod, 0)).astype(BF16)
    to_blocks, from_blocks = _regroup_matrices(x.shape[0])
    u_qk = _dot_nt(hb16, wt_ref[W_Q:W_V, :])
    u_lr = _dot_nt(hb16, wt_ref[W_LR:W_LR + LANES, :])
    hb = jnp.dot(to_blocks, hb16, preferred_element_type=F32).astype(BF16)
    uk = u_qk[:, GLA_DK_TOTAL:]
    bc, qt, factors = _gla_prep(u_lr, u_qk[:, :GLA_DK_TOTAL], uk, wa2_ref, gba_ref[...])
    piece_rows = ([W_V + i * VRG_PIECE for i in range((W_LR - W_V) // VRG_PIECE)]
                  + [W_GA + i * VRG_PIECE for i in range((N_IN - W_GA) // VRG_PIECE)])
    per_pair = -(-len(piece_rows) // (LRU_BLOCKS // 2))
    ya_blocks = []
    for m in range(LRU_BLOCKS // 2):
        ux2 = _dot_nt(hb, wt_ref[W_X + m * LRU_PAIR:W_X + (m + 1) * LRU_PAIR, :])
        ug2 = _dot_nt(hb, wt_ref[W_G + m * LRU_PAIR:W_G + (m + 1) * LRU_PAIR, :])
        for i in range(m * per_pair, min((m + 1) * per_pair, len(piece_rows))):
            uvrg_s[:, i * VRG_PIECE:(i + 1) * VRG_PIECE] = _dot_nt(hb16, wt_ref[piece_rows[i]:piece_rows[i] + VRG_PIECE, :])
        for j in range(2):
            uan = (ux2[:, j * LRU_BW:(j + 1) * LRU_BW], ug2[:, j * LRU_BW:(j + 1) * LRU_BW])
            ya_blocks.append(_lru_lane_block(2 * m + j, uan, cw_ref, cb_ref, wg_ref, ba_ref, bx_ref, lam_ref,
                                             conv_s, hc_s))
    o = _gla_chunks(bc, qt, factors, uk, uvrg_s, s_s)
    ya = jnp.dot(from_blocks, jnp.concatenate(ya_blocks, axis=1), preferred_element_type=F32).astype(BF16)
    ya = jnp.dot(ya, pa_ref[...], preferred_element_type=F32)
    yb = _head_norm_gate(o, gng_ref[...], uvrg_s[:, GLA_DV_TOTAL:GLA_DV_TOTAL + D_MODEL])
    x1_ref[0] = _merge_out(x, _mod_part(mod, 2), uvrg_s[:, 2 * D_MODEL:3 * D_MODEL], uvrg_s[:, 3 * D_MODEL:],
                           ya, yb, pb_ref, wo_ref)

    @pl.when(t == pl.num_programs(1) - 1)
    def _():
        conv_ref[0] = conv_s[0:CONV_W - 1, :]
        lru_ref[0] = hc_s[0:1, :]
        st_ref[0] = s_s[...]


def _prompt_mixer(x, mod_p, w, l):
    b, t, _ = x.shape
    tt = PROMPT_TILE
    per_seq = t // tt
    seq_spec = pl.BlockSpec((1, tt, D_MODEL), lambda i, j: (i, j, 0))
    return pl.pallas_call(
        _prompt_mixer_kernel,
        grid=(b, per_seq),
        in_specs=[
            seq_spec,
            pl.BlockSpec((1, 1, N_MOD), lambda i, j: (i, 0, 0)),
            _wspec(l, (1, D_MODEL)),
            _wspec(l, (N_IN, D_MODEL)),
            _wspec(l, (CONV_W, D_RNN)),
            _wspec(l, (1, D_RNN)),
            _wspec(l, (LRU_BLOCKS, LRU_BW, 2 * LRU_BW)),
            _wspec(l, (1, D_RNN)),
            _wspec(l, (1, D_RNN)),
            _wspec(l, (1, D_RNN)),
            _wspec(l, (D_RNN, D_MODEL)),
            _wspec(l, (LANES, GLA_DK_TOTAL)),
            _wspec(l, (1, GLA_DK_TOTAL)),
            _wspec(l, (1, GLA_DV_TOTAL)),
            _wspec(l, (GLA_DV_TOTAL, D_MODEL)),
            _wspec(l, (D_MODEL, D_MODEL)),
        ],
        out_specs=[
            seq_spec,
            pl.BlockSpec((1, CONV_W - 1, D_RNN), lambda i, j: (i, 0, 0)),
            pl.BlockSpec((1, 1, D_RNN), lambda i, j: (i, 0, 0)),
            pl.BlockSpec((1, GLA_HEADS, GLA_DK, GLA_DV), lambda i, j: (i, 0, 0, 0)),
        ],
        out_shape=[
            jax.ShapeDtypeStruct((b, t, D_MODEL), F32),
            jax.ShapeDtypeStruct((b, CONV_W - 1, D_RNN), F32),
            jax.ShapeDtypeStruct((b, 1, D_RNN), F32),
            jax.ShapeDtypeStruct((b, GLA_HEADS, GLA_DK, GLA_DV), F32),
        ],
        scratch_shapes=[
            pltpu.VMEM((SUBLANES, D_RNN), F32),
            pltpu.VMEM((SUBLANES, D_RNN), F32),
            pltpu.VMEM((GLA_HEADS, GLA_DK, GLA_DV), F32),
            pltpu.VMEM((tt, 2 * GLA_DV_TOTAL + 2 * D_MODEL), F32),
        ],
        compiler_params=_cparams(("arbitrary", "arbitrary")),
        name="prompt_mixer",
    )(x, mod_p, w["g1"], w["win_t"], w["conv_w"], w["conv_b"], w["wgate"], w["lru_ba"], w["lru_bx"],
      w["lru_lambda"], w["proj_a"], w["wa2"], w["gla_ba"], w["gla_norm_g"], w["proj_b"], w["w_out"])


def _ffn_kernel(x_ref, mod_ref, g2_ref, w1_ref, w2_ref, fg_ref, o_ref, *, final_norm):
    x = x_ref[...]
    m = mod_ref[0]
    h2 = _norm_mod(x, g2_ref[...], _mod_part(m, 4), _mod_part(m, 3))
    f = _dot(h2, w1_ref[...])
    x2 = x + _mod_part(m, 5) * _dot(_silu(f[:, :D_FF]) * f[:, D_FF:], w2_ref[...])
    if final_norm:
        x2 = x2 * _rms(x2) * fg_ref[...]
    o_ref[...] = x2


def _ffn(x2d, mod3, rows_per_mod, tile, w, l, final_g, final_norm):
    mrows = x2d.shape[0]
    r = mod3.shape[1]
    per = rows_per_mod // tile
    row_spec = pl.BlockSpec((tile, D_MODEL), lambda i: (i, 0))
    return pl.pallas_call(
        functools.partial(_ffn_kernel, final_norm=final_norm),
        grid=(mrows // tile,),
        in_specs=[
            row_spec,
            pl.BlockSpec((1, r, N_MOD), lambda i: (i // per, 0, 0)),
            _wspec(l, (1, D_MODEL)),
            _wspec(l, (D_MODEL, 2 * D_FF)),
            _wspec(l, (D_FF, D_MODEL)),
            _full_spec((1, D_MODEL)),
        ],
        out_specs=row_spec,
        out_shape=jax.ShapeDtypeStruct((mrows, D_MODEL), F32),
        compiler_params=_cparams(("arbitrary",)),
        name="ffn_final" if final_norm else "ffn",
    )(x2d, mod3, w["g2"], w["ffn_w1"], w["ffn_w2"], final_g)


def _sample_lru_kernel(x_ref, mod_ref, g1_ref, win_ref, cw_ref, cb_ref, wg_ref, ba_ref, bx_ref, lam_ref, pa_ref,
                       conv0_ref, h0_ref, ya_ref, conv_ref, lru_ref):
    ns = h0_ref.shape[0]
    nt = x_ref.shape[0] // ns
    m = mod_ref[...]
    tile = lambda v: jnp.concatenate([v] * nt, axis=0)
    h = _norm_mod(x_ref[...], g1_ref[...], tile(_mod_part(m, 1)), tile(_mod_part(m, 0)))
    ux = _dot_nt(h, win_ref[W_X:W_G, :])
    ug = _dot_nt(h, win_ref[W_G:W_Q, :])
    xa = [conv0_ref[i] for i in range(CONV_W - 1)] + [ux[i * ns:(i + 1) * ns, :] for i in range(nt)]
    cw = cw_ref[...]
    xcs = []
    for ti in range(nt):
        acc = cb_ref[...]
        for i in range(CONV_W):
            acc = acc + xa[ti + i] * cw[i:i + 1, :]
        xcs.append(acc)
    xc = jnp.concatenate(xcs, axis=0)
    coeffs = [_lru_coeffs_block(xc[:, n * LRU_BW:(n + 1) * LRU_BW], wg_ref[n], ba_ref[:, n * LRU_BW:(n + 1) * LRU_BW],
                                bx_ref[:, n * LRU_BW:(n + 1) * LRU_BW], lam_ref[:, n * LRU_BW:(n + 1) * LRU_BW])
              for n in range(LRU_BLOCKS)]
    a = jnp.concatenate([c[0] for c in coeffs], axis=1)
    bb = jnp.concatenate([c[1] for c in coeffs], axis=1)
    hc = h0_ref[...]
    hs = []
    for ti in range(nt):
        rows = slice(ti * ns, (ti + 1) * ns)
        hc = a[rows, :] * hc + bb[rows, :]
        hs.append(hc)
    ya_ref[...] = _dot(jnp.concatenate(hs, axis=0) * _gelu_tanh(ug), pa_ref[...])
    for i in range(CONV_W - 1):
        conv_ref[i] = xa[nt + i]
    lru_ref[...] = hc


def _sample_lru(x_tm, mod_s, conv0_tm, h0_all, w, l):
    rows = x_tm.shape[0]
    ns = h0_all.shape[1]
    return pl.pallas_call(
        _sample_lru_kernel,
        grid=(1,),
        in_specs=[
            _full_spec((rows, D_MODEL)),
            _full_spec((ns, N_MOD)),
            _wspec(l, (1, D_MODEL)),
            _wspec(l, (W_Q, D_MODEL)),
            _wspec(l, (CONV_W, D_RNN)),
            _wspec(l, (1, D_RNN)),
            _wspec(l, (LRU_BLOCKS, LRU_BW, 2 * LRU_BW)),
            _wspec(l, (1, D_RNN)),
            _wspec(l, (1, D_RNN)),
            _wspec(l, (1, D_RNN)),
            _wspec(l, (D_RNN, D_MODEL)),
            _full_spec((CONV_W - 1, ns, D_RNN)),
            _wspec(l, (ns, D_RNN)),
        ],
        out_specs=[
            _full_out((rows, D_MODEL)),
            _full_out((CONV_W - 1, ns, D_RNN)),
            _full_out((ns, D_RNN)),
        ],
        out_shape=[
            jax.ShapeDtypeStruct((rows, D_MODEL), F32),
            jax.ShapeDtypeStruct((CONV_W - 1, ns, D_RNN), F32),
            jax.ShapeDtypeStruct((ns, D_RNN), F32),
        ],
        compiler_params=_cparams(("arbitrary",)),
        name="sample_lru",
    )(x_tm, mod_s, w["g1"], w["win_t"], w["conv_w"], w["conv_b"], w["wgate"], w["lru_ba"], w["lru_bx"],
      w["lru_lambda"], w["proj_a"], conv0_tm, h0_all)


def _row_regroup(rows, inner, outer):
    ri = lax.broadcasted_iota(jnp.int32, (rows, rows), 0)
    ci = lax.broadcasted_iota(jnp.int32, (rows, rows), 1)
    return jnp.where(ci == (ri % inner) * outer + ri // inner, 1.0, 0.0).astype(BF16)


def _sample_inproj_kernel(x_ref, mod_ref, g1_ref, win_ref, u_ref, ug_ref):
    rows = x_ref.shape[0]
    ns = mod_ref.shape[0]
    nt = rows // ns
    m = mod_ref[...]
    tile = lambda v: jnp.concatenate([v] * nt, axis=0)
    h = _norm_mod(x_ref[...], g1_ref[...], tile(_mod_part(m, 1)), tile(_mod_part(m, 0))).astype(BF16)
    ug_ref[...] = _dot_nt(h, win_ref[W_GA:N_IN, :])
    hs = jnp.dot(_row_regroup(rows, nt, ns), h, preferred_element_type=F32).astype(BF16)
    u_ref[...] = jnp.concatenate([_dot_nt(hs, win_ref[W_LR:W_LR + LANES, :]), _dot_nt(hs, win_ref[W_Q:W_LR, :])],
                                 axis=1)


def _sample_inproj(x_tm, mod_s, w, l):
    rows = x_tm.shape[0]
    return pl.pallas_call(
        _sample_inproj_kernel,
        grid=(1,),
        in_specs=[
            _full_spec((rows, D_MODEL)),
            _full_spec(mod_s.shape),
            _wspec(l, (1, D_MODEL)),
            _wspec(l, (N_IN, D_MODEL)),
        ],
        out_specs=[_full_out((rows, N_IN_B)), _full_out((rows, 2 * D_MODEL))],
        out_shape=[jax.ShapeDtypeStruct((rows, N_IN_B), F32), jax.ShapeDtypeStruct((rows, 2 * D_MODEL), F32)],
        compiler_params=_cparams(("arbitrary",)),
        name="sample_inproj",
    )(x_tm, mod_s, w["g1"], w["win_t"])


def _sample_gla_kernel(u_ref, wa2_ref, gba_ref, gng_ref, s0_ref, *rest, nt, layer, carries_buffer):
    if carries_buffer:
        yb_ref, st_ref, o_s, q_s, v_s, kdt_s, bl_s = rest[1:]
    else:
        yb_ref, st_all_ref, o_s, q_s, v_s, kdt_s, bl_s = rest
        st_ref = st_all_ref.at[layer]
        for other in range(st_all_ref.shape[0]):
            if other != layer:
                st_all_ref[other] = jnp.zeros(st_all_ref.shape[1:], F32)
    rows = u_ref.shape[0]
    steps_per_rows = rows // (SAMPLE_STATE_BLOCK * nt)
    sub = pl.program_id(0) % steps_per_rows
    rg = SAMPLE_ROW_GROUP
    seq_per_group = rg // nt

    @pl.when(sub == 0)
    def _():
        u = u_ref[...]
        g = _gla_gate(u[:, _LRO:_LRO + LANES], wa2_ref, gba_ref[...])
        bc = _group_cumsum(g, nt)
        rin = lax.broadcasted_iota(jnp.int32, g.shape, 0) % nt
        sfx = jnp.zeros_like(g)
        for k in range(1, nt):
            sfx = sfx + jnp.where(rin + k < nt, pltpu.roll(g, rows - k, 0), 0.0)
        qt = u[:, _QO:_QO + GLA_DK_TOTAL] * (jnp.exp(bc) * (GLA_DK ** -0.5))
        uk = u[:, _KO:_KO + GLA_DK_TOTAL]
        kt = uk * jnp.exp(-bc)
        kd = uk * jnp.exp(sfx)
        q_s[...] = qt.astype(BF16)
        v_s[...] = u[:, _VO:_VO + GLA_DV_TOTAL].astype(BF16)
        bl_s[...] = bc + sfx
        ri = lax.broadcasted_iota(jnp.int32, (rows, rows), 0)
        ci = lax.broadcasted_iota(jnp.int32, (rows, rows), 1)
        same_seq_causal = (ri // nt == ci // nt) & (ri >= ci)
        for hd in range(GLA_HEADS):
            ks = slice(hd * GLA_DK, (hd + 1) * GLA_DK)
            att = jnp.where(same_seq_causal, _dot_nt(qt[:, ks], kt[:, ks]), 0.0)
            o_s[:, hd * GLA_DV:(hd + 1) * GLA_DV] = _dot(att, v_s[:, hd * GLA_DV:(hd + 1) * GLA_DV])
            kdt_s[hd] = kd[:, ks].T

    lane_seq = lax.broadcasted_iota(jnp.int32, (GLA_DK, rows), 1) // nt
    row_seq = lax.broadcasted_iota(jnp.int32, (rg, GLA_DV), 0) // nt

    def group(gi, carry):
        r0 = pl.multiple_of((sub * (SAMPLE_STATE_BLOCK // seq_per_group) + gi) * rg, rg)
        dec_g = jnp.exp(bl_s[pl.ds(r0, rg), :])
        for hd in range(GLA_HEADS):
            ks = slice(hd * GLA_DK, (hd + 1) * GLA_DK)
            vsl = slice(hd * GLA_DV, (hd + 1) * GLA_DV)
            qg = q_s[pl.ds(r0, rg), ks]
            vh = v_s[:, vsl]
            inter = jnp.zeros((rg, GLA_DV), F32)
            for j in range(seq_per_group):
                s_loc = gi * seq_per_group + j
                s_row = sub * SAMPLE_STATE_BLOCK + s_loc
                s_old = s0_ref[s_loc, hd]
                inter = jnp.where(row_seq == j, _dot(qg, s_old), inter)
                kdt = jnp.where(lane_seq == s_row, kdt_s[hd], 0.0)
                dm = _col_bcast(dec_g[j * nt:j * nt + 1, ks])
                st_ref[s_loc, hd] = s_old * jnp.concatenate([dm, dm], axis=1) + _dot(kdt, vh)
            o_s[pl.ds(r0, rg), vsl] = o_s[pl.ds(r0, rg), vsl] + inter
        return carry

    lax.fori_loop(0, SAMPLE_STATE_BLOCK // seq_per_group, group, 0)

    @pl.when(sub == steps_per_rows - 1)
    def _():
        yb_ref[...] = _head_norm_gate(o_s[...], gng_ref[...], u_ref[:, _RO:_RO + D_MODEL])


def _sample_gla(u_sm, state_all, new_state_all, w, l, nt):
    rows = u_sm.shape[0]
    br = SAMPLE_ROW_BLOCK
    sb = SAMPLE_STATE_BLOCK
    per = br // (sb * nt)
    state_spec = pl.BlockSpec((None, sb, GLA_HEADS, GLA_DK, GLA_DV), lambda i: (l, i, 0, 0, 0))
    out_state_spec = state_spec
    if new_state_all is None:
        out_state_spec = pl.BlockSpec((state_all.shape[0], sb, GLA_HEADS, GLA_DK, GLA_DV), lambda i: (0, i, 0, 0, 0))
    in_specs = [
        pl.BlockSpec((br, N_IN_B), lambda i: (i // per, 0)),
        _wspec(l, (LANES, GLA_DK_TOTAL)),
        _wspec(l, (1, GLA_DK_TOTAL)),
        _wspec(l, (1, GLA_DV_TOTAL)),
        state_spec,
    ]
    args = [u_sm, w["wa2"], w["gla_ba"], w["gla_norm_g"], state_all]
    aliases = {}
    if new_state_all is not None:
        in_specs.append(pl.BlockSpec(memory_space=pl.ANY))
        args.append(new_state_all)
        aliases = {len(args) - 1: 1}
    return pl.pallas_call(
        functools.partial(_sample_gla_kernel, nt=nt, layer=l, carries_buffer=new_state_all is not None),
        grid=(rows // (sb * nt),),
        in_specs=in_specs,
        out_specs=[
            pl.BlockSpec((br, D_MODEL), lambda i: (i // per, 0)),
            out_state_spec,
        ],
        out_shape=[
            jax.ShapeDtypeStruct((rows, D_MODEL), F32),
            jax.ShapeDtypeStruct(state_all.shape, F32),
        ],
        input_output_aliases=aliases,
        scratch_shapes=[
            pltpu.VMEM((br, GLA_DV_TOTAL), F32),
            pltpu.VMEM((br, GLA_DK_TOTAL), BF16),
            pltpu.VMEM((br, GLA_DV_TOTAL), BF16),
            pltpu.VMEM((GLA_HEADS, GLA_DK, br), F32),
            pltpu.VMEM((br, GLA_DK_TOTAL), F32),
        ],
        compiler_params=_cparams(("arbitrary",)),
        name="sample_gla",
    )(*args)


def _sample_merge_kernel(x_ref, mod_ref, ug_ref, ya_ref, yb_ref, pb_ref, wo_ref, x1_ref):
    rows = x_ref.shape[0]
    ns = mod_ref.shape[0]
    nt = rows // ns
    gt1 = jnp.concatenate([_mod_part(mod_ref[...], 2)] * nt, axis=0)
    yb = jnp.dot(_row_regroup(rows, ns, nt), yb_ref[...].astype(BF16), preferred_element_type=F32).astype(BF16)
    x1_ref[...] = _merge_out(x_ref[...], gt1, ug_ref[:, :D_MODEL], ug_ref[:, D_MODEL:], ya_ref[...], yb, pb_ref, wo_ref)


def _sample_merge(x_tm, mod_s, ug_tm, ya, yb_sm, w, l):
    rows = x_tm.shape[0]
    return pl.pallas_call(
        _sample_merge_kernel,
        grid=(1,),
        in_specs=[
            _full_spec((rows, D_MODEL)),
            _full_spec(mod_s.shape),
            _full_spec((rows, 2 * D_MODEL)),
            _full_spec((rows, D_MODEL)),
            _full_spec((rows, D_MODEL)),
            _wspec(l, (GLA_DV_TOTAL, D_MODEL)),
            _wspec(l, (D_MODEL, D_MODEL)),
        ],
        out_specs=_full_out((rows, D_MODEL)),
        out_shape=jax.ShapeDtypeStruct((rows, D_MODEL), F32),
        compiler_params=_cparams(("arbitrary",)),
        name="sample_merge",
    )(x_tm, mod_s, ug_tm, ya, yb_sm, w["proj_b"], w["w_out"])


def _stacked_weights(norm1_g, norm2_g, w_in, conv_w, conv_b, lru_wa, lru_ba, lru_wx, lru_bx, lru_lambda,
                     gla_wa2, gla_ba, gla_norm_g, proj_a, proj_b, w_out, ffn_w1, ffn_w2):
    depth = w_in.shape[0]
    row = lambda v: v.reshape(depth, 1, -1)
    return {
        "g1": row(norm1_g), "g2": row(norm2_g),
        "win_t": jnp.swapaxes(w_in, 1, 2).astype(BF16),
        "conv_w": conv_w, "conv_b": row(conv_b),
        "wgate": jnp.concatenate([lru_wa, lru_wx], axis=-1).astype(BF16),
        "lru_ba": row(lru_ba), "lru_bx": row(lru_bx), "lru_lambda": row(lru_lambda),
        "wa2": jnp.pad(gla_wa2, ((0, 0), (0, LANES - GLA_RANK), (0, 0))).astype(BF16),
        "gla_ba": row(gla_ba), "gla_norm_g": row(gla_norm_g),
        "proj_a": proj_a.astype(BF16), "proj_b": proj_b.astype(BF16), "w_out": w_out.astype(BF16),
        "ffn_w1": ffn_w1.astype(BF16), "ffn_w2": ffn_w2.astype(BF16),
    }


def kernel(x_prompt, x_sample, c_prompt, c_sample, state_conv, state_lru, state_gla, norm1_g, norm2_g, ada_w, ada_b,
           w_in, conv_w, conv_b, lru_wa, lru_ba, lru_wx, lru_bx, lru_lambda, gla_wa2, gla_ba, gla_norm_g, proj_a,
           proj_b, w_out, ffn_w1, ffn_w2, final_g):
    bp, tp, _ = x_prompt.shape
    bs, ts, _ = x_sample.shape
    depth = w_in.shape[0]
    mod = _modulation(jnp.concatenate([c_prompt, c_sample], axis=0), ada_w, ada_b)
    w = _stacked_weights(norm1_g, norm2_g, w_in, conv_w, conv_b, lru_wa, lru_ba, lru_wx, lru_bx, lru_lambda,
                         gla_wa2, gla_ba, gla_norm_g, proj_a, proj_b, w_out, ffn_w1, ffn_w2)
    fg = final_g.reshape(1, D_MODEL)
    xp = x_prompt
    xs = x_sample.transpose(1, 0, 2).reshape(ts * bs, D_MODEL)
    conv_p, lru_p, gla_p, conv_s, lru_s = [], [], [], [], []
    gla_s = None
    for l in range(depth):
        last = l == depth - 1
        mod_p = mod[l, :bp].reshape(bp, 1, N_MOD)
        mod_s = mod[l, bp:]
        x1, cb, ht, st = _prompt_mixer(xp, mod_p, w, l)
        xp = _ffn(x1.reshape(bp * tp, D_MODEL), mod_p, tp, FFN_TILE, w, l, fg, last).reshape(bp, tp, D_MODEL)
        conv_p.append(cb)
        lru_p.append(ht.reshape(bp, D_RNN))
        gla_p.append(st)
        ya_s, cb_s, ht_s = _sample_lru(xs, mod_s, state_conv[l].transpose(1, 0, 2), state_lru, w, l)
        u_sm, ug_tm = _sample_inproj(xs, mod_s, w, l)
        yb_sm, gla_s = _sample_gla(u_sm, state_gla, gla_s, w, l, ts)
        x1_s = _sample_merge(xs, mod_s, ug_tm, ya_s, yb_sm, w, l)
        xs = _ffn(x1_s, mod_s.reshape(1, bs, N_MOD), bs * ts, bs, w, l, fg, last)
        conv_s.append(cb_s.transpose(1, 0, 2))
        lru_s.append(ht_s)
    y_sample = xs.reshape(ts, bs, D_MODEL).transpose(1, 0, 2)
    return (xp, y_sample, jnp.stack(conv_p), jnp.stack(lru_p), jnp.stack(gla_p),
            jnp.stack(conv_s), jnp.stack(lru_s), gla_s)
```

```python
import functools

import jax
import jax.numpy as jnp
from jax import lax
from jax.experimental import pallas as pl
from jax.experimental.pallas import tpu as pltpu

F32 = jnp.float32
BF16 = jnp.bfloat16

D_MODEL = 1024
D_RNN = 1280
LRU_BW = 128
LRU_BLOCKS = D_RNN // LRU_BW
CONV_W = 4
LRU_C = 8.0
GLA_HEADS = 4
GLA_DK = 128
GLA_DV = 256
GLA_DK_TOTAL = GLA_HEADS * GLA_DK
GLA_DV_TOTAL = GLA_HEADS * GLA_DV
GLA_RANK = 16
GLA_TAU = 16.0
D_FF = 2816
EPS = 1e-6
TINY_F32 = 1.1754944e-38
N_MOD = 6 * D_MODEL

LANES = 128
SUBLANES = 8
VMEM_LIMIT_BYTES = 60 * 1024 * 1024

W_X, W_G, W_Q, W_K, W_V, W_R, W_LR, W_GA, W_GB = 0, 1280, 2560, 3072, 3584, 4608, 5632, 5648, 6672
N_IN = W_GB + D_MODEL
_LRO, _QO, _KO, _VO, _RO = 0, 128, 640, 1152, 2176
N_IN_B = _RO + D_MODEL

PROMPT_TILE = 512
GLA_CHUNK = 128
GLA_SPLITS = (64, 32)
GLA_FINE = 32
FFN_TILE = 512
VRG_PIECE = 512
LRU_PAIR = 2 * LRU_BW
SAMPLE_ROW_BLOCK = 128
SAMPLE_STATE_BLOCK = 8
SAMPLE_ROW_GROUP = 16


def _cparams(sem):
    return pltpu.CompilerParams(dimension_semantics=sem, vmem_limit_bytes=VMEM_LIMIT_BYTES)


def _full_spec(shape):
    n = len(shape)
    return pl.BlockSpec(shape, lambda *_: (0,) * n, pipeline_mode=pl.Buffered(1))


def _full_out(shape):
    n = len(shape)
    return pl.BlockSpec(shape, lambda *_: (0,) * n)


def _wspec(l, shape):
    n = len(shape)
    return pl.BlockSpec((None,) + tuple(shape), lambda *_: (l,) + (0,) * n, pipeline_mode=pl.Buffered(1))


def _softplus(y):
    return jnp.maximum(y, 0.0) + jnp.log1p(jnp.exp(-jnp.abs(y)))


def _sigmoid(y):
    return 0.5 * jnp.tanh(0.5 * y) + 0.5


def _silu(y):
    t = 0.5 * y
    return t + t * jnp.tanh(t)


def _sqrt_nonneg(s):
    return s * lax.rsqrt(jnp.maximum(s, TINY_F32))


def _gelu_tanh(y):
    return 0.5 * y * (1.0 + jnp.tanh(0.7978845608028654 * (y + 0.044715 * (y * y * y))))


def _rms(x):
    return lax.rsqrt(jnp.mean(x * x, axis=-1, keepdims=True) + EPS)


def _norm_mod(x, g, scale, shift):
    return (x * _rms(x)) * (g * (1.0 + scale)) + shift


def _mod_part(m, i):
    return m[:, i * D_MODEL:(i + 1) * D_MODEL]


def _dot(a, b):
    return jnp.dot(a.astype(BF16), b.astype(BF16), preferred_element_type=F32)


def _dot_nt(a, b):
    return lax.dot_general(a.astype(BF16), b.astype(BF16), (((1,), (1,)), ((), ())),
                           preferred_element_type=F32)


def _lru_coeffs_block(xb, wg, ba, bx, lam):
    lamc = -LRU_C * _softplus(-lam)
    pre = _dot(xb, wg)
    r = _sigmoid(pre[:, :LRU_BW] + ba)
    i = _sigmoid(pre[:, LRU_BW:] + bx)
    a = jnp.exp(lamc * r)
    return a, _sqrt_nonneg(1.0 - a * a) * (i * xb)


def _gla_gate(ulr, wa2_ref, gba):
    z = _dot(ulr, wa2_ref[...]) + gba
    return (jnp.minimum(z, 0.0) - jnp.log(1.0 + jnp.exp(-jnp.abs(z)))) * (1.0 / GLA_TAU)


def _group_cumsum(g, group):
    rin = lax.broadcasted_iota(jnp.int32, g.shape, 0) % group
    x = g
    k = 1
    while k < group:
        x = x + jnp.where(rin >= k, pltpu.roll(x, k, 0), 0.0)
        k *= 2
    return x


def _col_bcast(row):
    return jnp.broadcast_to(row, (LANES, LANES)).T


def _head_norm_gate(o, gng, ur):
    parts = []
    for hd in range(GLA_HEADS):
        sl = slice(hd * GLA_DV, (hd + 1) * GLA_DV)
        oh = o[:, sl]
        parts.append(oh * _rms(oh) * gng[:, sl])
    return jnp.concatenate(parts, axis=1) * _silu(ur)


def _merge_out(x, gt1, uga, ugb, ya, yb, pb_ref, wo_ref):
    pbv = _dot(yb, pb_ref[...])
    mm = _sigmoid(uga) * ya + _sigmoid(ugb) * pbv
    return x + gt1 * _dot(mm, wo_ref[...])


def _mod_kernel(c_ref, w_ref, b_ref, o_ref):
    o_ref[0] = _dot(_silu(c_ref[...]), w_ref[0]) + b_ref[0]


def _modulation(c_all, ada_w, ada_b):
    depth = ada_w.shape[0]
    rows = c_all.shape[0]
    nblk = N_MOD // D_MODEL
    return pl.pallas_call(
        _mod_kernel,
        grid=(depth, nblk),
        in_specs=[
            pl.BlockSpec((rows, D_MODEL), lambda l, j: (0, 0)),
            pl.BlockSpec((1, D_MODEL, D_MODEL), lambda l, j: (l, 0, j)),
            pl.BlockSpec((1, 1, D_MODEL), lambda l, j: (l, 0, j)),
        ],
        out_specs=pl.BlockSpec((1, rows, D_MODEL), lambda l, j: (l, 0, j)),
        out_shape=jax.ShapeDtypeStruct((depth, rows, N_MOD), F32),
        compiler_params=_cparams(("arbitrary", "arbitrary")),
        name="modulation",
    )(c_all, ada_w, ada_b.reshape(depth, 1, N_MOD))


def _shift_rows_in(blk, first_row):
    top = lax.broadcasted_iota(jnp.int32, blk.shape, 0) == 0
    return jnp.where(top, first_row, pltpu.roll(blk, 1, 0))


def _regroup_matrices(tt):
    ng = tt // SUBLANES
    ri = lax.broadcasted_iota(jnp.int32, (tt, tt), 0)
    ci = lax.broadcasted_iota(jnp.int32, (tt, tt), 1)
    to_blocks = jnp.where(ci == SUBLANES * (ri % ng) + ri // ng, 1.0, 0.0).astype(BF16)
    from_blocks = jnp.where(ci == ng * (ri % SUBLANES) + ri // SUBLANES, 1.0, 0.0).astype(BF16)
    return to_blocks, from_blocks


def _lru_conv(ux, cw, cb, conv_rows):
    ng = ux.shape[0] // SUBLANES
    first_tail = SUBLANES - (CONV_W - 1)
    xs = [ux[j * ng:(j + 1) * ng, :] for j in range(SUBLANES)]
    prev = {k: _shift_rows_in(xs[k], conv_rows[k - first_tail:k - first_tail + 1, :])
            for k in range(first_tail, SUBLANES)}
    xcs = []
    for j in range(SUBLANES):
        acc = cb
        for i in range(CONV_W):
            d = CONV_W - 1 - i
            src = xs[j - d] if j >= d else prev[j - d + SUBLANES]
            acc = acc + src * cw[i:i + 1, :]
        xcs.append(acc)
    tail = jnp.concatenate([xs[k][ng - 1:ng, :] for k in range(first_tail, SUBLANES)], axis=0)
    return jnp.concatenate(xcs, axis=0), tail


def _lru_scan(a, bb, carry):
    ng = a.shape[0] // SUBLANES
    loc = [bb[0:ng, :]]
    cum = [a[0:ng, :]]
    for j in range(1, SUBLANES):
        aj = a[j * ng:(j + 1) * ng, :]
        loc.append(aj * loc[-1] + bb[j * ng:(j + 1) * ng, :])
        cum.append(aj * cum[-1])
    ga, gb = cum[-1], loc[-1]
    rowg = lax.broadcasted_iota(jnp.int32, ga.shape, 0)
    k = 1
    while k < ng:
        ga_sh = jnp.where(rowg >= k, pltpu.roll(ga, k, 0), 1.0)
        gb_sh = jnp.where(rowg >= k, pltpu.roll(gb, k, 0), 0.0)
        gb = ga * gb_sh + gb
        ga = ga * ga_sh
        k *= 2
    leaving = gb + ga * carry
    entering = _shift_rows_in(leaving, carry)
    hs = jnp.concatenate([loc[j] + cum[j] * entering for j in range(SUBLANES)], axis=0)
    return hs, leaving[ng - 1:ng, :]


def _lru_lane_block(n, uan, cw_ref, cb_ref, wg_ref, ba_ref, bx_ref, lam_ref, conv_s, hc_s):
    sl = slice(n * LRU_BW, (n + 1) * LRU_BW)
    xc, tail = _lru_conv(uan[0], cw_ref[:, sl], cb_ref[:, sl], conv_s[0:CONV_W - 1, sl])
    conv_s[0:CONV_W - 1, sl] = tail
    a, bb = _lru_coeffs_block(xc, wg_ref[n], ba_ref[:, sl], bx_ref[:, sl], lam_ref[:, sl])
    hs, last = _lru_scan(a, bb, hc_s[0:1, sl])
    hc_s[0:1, sl] = last
    return (hs * _gelu_tanh(uan[1])).astype(BF16)


def _rows_per_block(x, size, pick):
    parts = []
    for b in range(x.shape[0] // size):
        r = pick(b)
        row = jnp.zeros((1, x.shape[1]), x.dtype) if r is None else x[r:r + 1, :]
        parts.append(jnp.broadcast_to(row, (size, x.shape[1])))
    return jnp.concatenate(parts, axis=0)


def _gla_prep(ulr, uq, uk, wa2_ref, gba):
    c = GLA_CHUNK
    g = _gla_gate(ulr, wa2_ref, gba)
    bc = _group_cumsum(g, c)
    qs = uq * (GLA_DK ** -0.5)
    factors = []
    for size in GLA_SPLITS:
        before = _rows_per_block(bc, size, lambda b: None if (b * size) % c == 0 else b * size - 1)
        last = _rows_per_block(bc, size, lambda b: b * size + size - 1)
        factors.append((qs * jnp.exp(bc - before), uk * jnp.exp(last - bc)))
    mid = _rows_per_block(bc, GLA_FINE, lambda b: b * GLA_FINE + GLA_FINE // 2 - 1)
    factors.append((qs * jnp.exp(bc - mid), uk * jnp.exp(mid - bc)))
    return bc, qs * jnp.exp(bc), factors


def _gla_level_masks():
    c = GLA_CHUNK
    ri = lax.broadcasted_iota(jnp.int32, (c, c), 0)
    ci = lax.broadcasted_iota(jnp.int32, (c, c), 1)
    masks = [(ri // (2 * size) == ci // (2 * size)) & ((ri // size) % 2 == 1) & ((ci // size) % 2 == 0)
             for size in GLA_SPLITS]
    masks.append((ri // GLA_FINE == ci // GLA_FINE) & (ri >= ci))
    return masks


def _gla_chunks(bc, qt, factors, uk, uv, s_s):
    tt = bc.shape[0]
    c = GLA_CHUNK
    masks = _gla_level_masks()
    o_rows = []
    for ch in range(tt // c):
        rows = slice(ch * c, (ch + 1) * c)
        bl = bc[ch * c + c - 1:ch * c + c, :]
        kd = uk[rows, :] * jnp.exp(bl - bc[rows, :])
        dec = jnp.exp(bl)
        o_heads = []
        for hd in range(GLA_HEADS):
            ks = slice(hd * GLA_DK, (hd + 1) * GLA_DK)
            qh = qt[rows, ks].astype(BF16)
            vh = uv[rows, hd * GLA_DV:(hd + 1) * GLA_DV].astype(BF16)
            att = jnp.zeros((c, c), F32)
            for (qf, kf), mask in zip(factors, masks):
                att = jnp.where(mask, _dot_nt(qf[rows, ks], kf[rows, ks]), att)
            s_old = s_s[hd]
            o_heads.append(_dot(att, vh) + _dot(qh, s_old))
            dm = _col_bcast(dec[:, ks])
            s_s[hd] = s_old * jnp.concatenate([dm, dm], axis=1) + _dot(kd[:, ks].T, vh)
        o_rows.append(jnp.concatenate(o_heads, axis=1))
    return jnp.concatenate(o_rows, axis=0)


def _prompt_mixer_kernel(x_ref, mod_ref, g1_ref, wt_ref, cw_ref, cb_ref, wg_ref, ba_ref, bx_ref, lam_ref,
                         pa_ref, wa2_ref, gba_ref, gng_ref, pb_ref, wo_ref,
                         x1_ref, conv_ref, lru_ref, st_ref, conv_s, hc_s, s_s, uvrg_s):
    t = pl.program_id(1)

    @pl.when(t == 0)
    def _():
        conv_s[...] = jnp.zeros_like(conv_s)
        hc_s[...] = jnp.zeros_like(hc_s)
        s_s[...] = jnp.zeros_like(s_s)

    x = x_ref[0]
    mod = mod_ref[0]
    hb16 = _norm_mod(x, g1_ref[...], _mod_part(mod, 1), _mod_part(mod, 0)).astype(BF16)
    to_blocks, from_blocks = _regroup_matrices(x.shape[0])
    u_qk = _dot_nt(hb16, wt_ref[W_Q:W_V, :])
    u_lr = _dot_nt(hb16, wt_ref[W_LR:W_LR + LANES, :])
    hb = jnp.dot(to_blocks, hb16, preferred_element_type=F32).astype(BF16)
    uk = u_qk[:, GLA_DK_TOTAL:]
    bc, qt, factors = _gla_prep(u_lr, u_qk[:, :GLA_DK_TOTAL], uk, wa2_ref, gba_ref[...])
    piece_rows = ([W_V + i * VRG_PIECE for i in range((W_LR - W_V) // VRG_PIECE)]
                  + [W_GA + i * VRG_PIECE for i in range((N_IN - W_GA) // VRG_PIECE)])
    per_pair = -(-len(piece_rows) // (LRU_BLOCKS // 2))
    ya_blocks = []
    for m in range(LRU_BLOCKS // 2):
        ux2 = _dot_nt(hb, wt_ref[W_X + m * LRU_PAIR:W_X + (m + 1) * LRU_PAIR, :])
        ug2 = _dot_nt(hb, wt_ref[W_G + m * LRU_PAIR:W_G + (m + 1) * LRU_PAIR, :])
        for i in range(m * per_pair, min((m + 1) * per_pair, len(piece_rows))):
            uvrg_s[:, i * VRG_PIECE:(i + 1) * VRG_PIECE] = _dot_nt(hb16, wt_ref[piece_rows[i]:piece_rows[i] + VRG_PIECE, :])
        for j in range(2):
            uan = (ux2[:, j * LRU_BW:(j + 1) * LRU_BW], ug2[:, j * LRU_BW:(j + 1) * LRU_BW])
            ya_blocks.append(_lru_lane_block(2 * m + j, uan, cw_ref, cb_ref, wg_ref, ba_ref, bx_ref, lam_ref,
                                             conv_s, hc_s))
    o = _gla_chunks(bc, qt, factors, uk, uvrg_s, s_s)
    ya = jnp.dot(from_blocks, jnp.concatenate(ya_blocks, axis=1), preferred_element_type=F32).astype(BF16)
    ya = jnp.dot(ya, pa_ref[...], preferred_element_type=F32)
    yb = _head_norm_gate(o, gng_ref[...], uvrg_s[:, GLA_DV_TOTAL:GLA_DV_TOTAL + D_MODEL])
    x1_ref[0] = _merge_out(x, _mod_part(mod, 2), uvrg_s[:, 2 * D_MODEL:3 * D_MODEL], uvrg_s[:, 3 * D_MODEL:],
                           ya, yb, pb_ref, wo_ref)

    @pl.when(t == pl.num_programs(1) - 1)
    def _():
        conv_ref[0] = conv_s[0:CONV_W - 1, :]
        lru_ref[0] = hc_s[0:1, :]
        st_ref[0] = s_s[...]


def _prompt_mixer(x, mod_p, w, l):
    b, t, _ = x.shape
    tt = PROMPT_TILE
    per_seq = t // tt
    seq_spec = pl.BlockSpec((1, tt, D_MODEL), lambda i, j: (i, j, 0))
    return pl.pallas_call(
        _prompt_mixer_kernel,
        grid=(b, per_seq),
        in_specs=[
            seq_spec,
            pl.BlockSpec((1, 1, N_MOD), lambda i, j: (i, 0, 0)),
            _wspec(l, (1, D_MODEL)),
            _wspec(l, (N_IN, D_MODEL)),
            _wspec(l, (CONV_W, D_RNN)),
            _wspec(l, (1, D_RNN)),
            _wspec(l, (LRU_BLOCKS, LRU_BW, 2 * LRU_BW)),
            _wspec(l, (1, D_RNN)),
            _wspec(l, (1, D_RNN)),
            _wspec(l, (1, D_RNN)),
            _wspec(l, (D_RNN, D_MODEL)),
            _wspec(l, (LANES, GLA_DK_TOTAL)),
            _wspec(l, (1, GLA_DK_TOTAL)),
            _wspec(l, (1, GLA_DV_TOTAL)),
            _wspec(l, (GLA_DV_TOTAL, D_MODEL)),
            _wspec(l, (D_MODEL, D_MODEL)),
        ],
        out_specs=[
            seq_spec,
            pl.BlockSpec((1, CONV_W - 1, D_RNN), lambda i, j: (i, 0, 0)),
            pl.BlockSpec((1, 1, D_RNN), lambda i, j: (i, 0, 0)),
            pl.BlockSpec((1, GLA_HEADS, GLA_DK, GLA_DV), lambda i, j: (i, 0, 0, 0)),
        ],
        out_shape=[
            jax.ShapeDtypeStruct((b, t, D_MODEL), F32),
            jax.ShapeDtypeStruct((b, CONV_W - 1, D_RNN), F32),
            jax.ShapeDtypeStruct((b, 1, D_RNN), F32),
            jax.ShapeDtypeStruct((b, GLA_HEADS, GLA_DK, GLA_DV), F32),
        ],
        scratch_shapes=[
            pltpu.VMEM((SUBLANES, D_RNN), F32),
            pltpu.VMEM((SUBLANES, D_RNN), F32),
            pltpu.VMEM((GLA_HEADS, GLA_DK, GLA_DV), F32),
            pltpu.VMEM((tt, 2 * GLA_DV_TOTAL + 2 * D_MODEL), F32),
        ],
        compiler_params=_cparams(("arbitrary", "arbitrary")),
        name="prompt_mixer",
    )(x, mod_p, w["g1"], w["win_t"], w["conv_w"], w["conv_b"], w["wgate"], w["lru_ba"], w["lru_bx"],
      w["lru_lambda"], w["proj_a"], w["wa2"], w["gla_ba"], w["gla_norm_g"], w["proj_b"], w["w_out"])


def _ffn_rows(x, shift, scale, gate, g2, w1_ref, w2_ref, fg, final_norm):
    h2 = _norm_mod(x, g2, scale, shift)
    f = _dot(h2, w1_ref[...])
    x2 = x + gate * _dot(_silu(f[:, :D_FF]) * f[:, D_FF:], w2_ref[...])
    if final_norm:
        x2 = x2 * _rms(x2) * fg
    return x2


def _ffn_kernel(x_ref, mod_ref, g2_ref, w1_ref, w2_ref, fg_ref, o_ref, *, final_norm):
    m = mod_ref[0]
    o_ref[...] = _ffn_rows(x_ref[...], _mod_part(m, 3), _mod_part(m, 4), _mod_part(m, 5), g2_ref[...], w1_ref, w2_ref,
                           fg_ref[...], final_norm)


def _ffn(x2d, mod3, rows_per_mod, tile, w, l, final_g, final_norm):
    mrows = x2d.shape[0]
    r = mod3.shape[1]
    per = rows_per_mod // tile
    row_spec = pl.BlockSpec((tile, D_MODEL), lambda i: (i, 0))
    return pl.pallas_call(
        functools.partial(_ffn_kernel, final_norm=final_norm),
        grid=(mrows // tile,),
        in_specs=[
            row_spec,
            pl.BlockSpec((1, r, N_MOD), lambda i: (i // per, 0, 0)),
            _wspec(l, (1, D_MODEL)),
            _wspec(l, (D_MODEL, 2 * D_FF)),
            _wspec(l, (D_FF, D_MODEL)),
            _full_spec((1, D_MODEL)),
        ],
        out_specs=row_spec,
        out_shape=jax.ShapeDtypeStruct((mrows, D_MODEL), F32),
        compiler_params=_cparams(("arbitrary",)),
        name="ffn_final" if final_norm else "ffn",
    )(x2d, mod3, w["g2"], w["ffn_w1"], w["ffn_w2"], final_g)


def _row_regroup(rows, inner, outer):
    ri = lax.broadcasted_iota(jnp.int32, (rows, rows), 0)
    ci = lax.broadcasted_iota(jnp.int32, (rows, rows), 1)
    return jnp.where(ci == (ri % inner) * outer + ri // inner, 1.0, 0.0).astype(BF16)


def _sample_in_kernel(x_ref, mod_ref, g1_ref, win_ref, cw_ref, cb_ref, wg_ref, ba_ref, bx_ref, lam_ref, pa_ref,
                      conv0_ref, h0_ref, ya_ref, conv_ref, lru_ref, u_ref, ug_ref):
    rows = x_ref.shape[0]
    ns = h0_ref.shape[0]
    nt = rows // ns
    m = mod_ref[...]
    tile = lambda v: jnp.concatenate([v] * nt, axis=0)
    h = _norm_mod(x_ref[...], g1_ref[...], tile(_mod_part(m, 1)), tile(_mod_part(m, 0))).astype(BF16)
    ux = _dot_nt(h, win_ref[W_X:W_G, :])
    ug = _dot_nt(h, win_ref[W_G:W_Q, :])
    xa = [conv0_ref[i] for i in range(CONV_W - 1)] + [ux[i * ns:(i + 1) * ns, :] for i in range(nt)]
    cw = cw_ref[...]
    xcs = []
    for ti in range(nt):
        acc = cb_ref[...]
        for i in range(CONV_W):
            acc = acc + xa[ti + i] * cw[i:i + 1, :]
        xcs.append(acc)
    xc = jnp.concatenate(xcs, axis=0)
    coeffs = [_lru_coeffs_block(xc[:, n * LRU_BW:(n + 1) * LRU_BW], wg_ref[n], ba_ref[:, n * LRU_BW:(n + 1) * LRU_BW],
                                bx_ref[:, n * LRU_BW:(n + 1) * LRU_BW], lam_ref[:, n * LRU_BW:(n + 1) * LRU_BW])
              for n in range(LRU_BLOCKS)]
    a = jnp.concatenate([c[0] for c in coeffs], axis=1)
    bb = jnp.concatenate([c[1] for c in coeffs], axis=1)
    hc = h0_ref[...]
    hs = []
    for ti in range(nt):
        sl = slice(ti * ns, (ti + 1) * ns)
        hc = a[sl, :] * hc + bb[sl, :]
        hs.append(hc)
    ya_ref[...] = _dot(jnp.concatenate(hs, axis=0) * _gelu_tanh(ug), pa_ref[...])
    for i in range(CONV_W - 1):
        conv_ref[i] = xa[nt + i]
    lru_ref[...] = hc
    ug_ref[...] = _dot_nt(h, win_ref[W_GA:N_IN, :])
    hs_major = jnp.dot(_row_regroup(rows, nt, ns), h, preferred_element_type=F32).astype(BF16)
    u_ref[...] = jnp.concatenate([_dot_nt(hs_major, win_ref[W_LR:W_LR + LANES, :]),
                                  _dot_nt(hs_major, win_ref[W_Q:W_LR, :])], axis=1)


def _sample_in(x_tm, mod_s, conv0_tm, h0_all, w, l):
    rows = x_tm.shape[0]
    ns = h0_all.shape[1]
    return pl.pallas_call(
        _sample_in_kernel,
        grid=(1,),
        in_specs=[
            _full_spec((rows, D_MODEL)),
            _full_spec((ns, N_MOD)),
            _wspec(l, (1, D_MODEL)),
            _wspec(l, (N_IN, D_MODEL)),
            _wspec(l, (CONV_W, D_RNN)),
            _wspec(l, (1, D_RNN)),
            _wspec(l, (LRU_BLOCKS, LRU_BW, 2 * LRU_BW)),
            _wspec(l, (1, D_RNN)),
            _wspec(l, (1, D_RNN)),
            _wspec(l, (1, D_RNN)),
            _wspec(l, (D_RNN, D_MODEL)),
            _full_spec((CONV_W - 1, ns, D_RNN)),
            _wspec(l, (ns, D_RNN)),
        ],
        out_specs=[
            _full_out((rows, D_MODEL)),
            _full_out((CONV_W - 1, ns, D_RNN)),
            _full_out((ns, D_RNN)),
            _full_out((rows, N_IN_B)),
            _full_out((rows, 2 * D_MODEL)),
        ],
        out_shape=[
            jax.ShapeDtypeStruct((rows, D_MODEL), F32),
            jax.ShapeDtypeStruct((CONV_W - 1, ns, D_RNN), F32),
            jax.ShapeDtypeStruct((ns, D_RNN), F32),
            jax.ShapeDtypeStruct((rows, N_IN_B), F32),
            jax.ShapeDtypeStruct((rows, 2 * D_MODEL), F32),
        ],
        compiler_params=_cparams(("arbitrary",)),
        name="sample_in",
    )(x_tm, mod_s, w["g1"], w["win_t"], w["conv_w"], w["conv_b"], w["wgate"], w["lru_ba"], w["lru_bx"],
      w["lru_lambda"], w["proj_a"], conv0_tm, h0_all)


def _sample_gla_kernel(u_ref, wa2_ref, gba_ref, gng_ref, s0_ref, *rest, nt, layer, carries_buffer):
    if carries_buffer:
        yb_ref, st_ref, o_s, q_s, v_s, kdt_s, bl_s = rest[1:]
    else:
        yb_ref, st_all_ref, o_s, q_s, v_s, kdt_s, bl_s = rest
        st_ref = st_all_ref.at[layer]
        for other in range(st_all_ref.shape[0]):
            if other != layer:
                st_all_ref[other] = jnp.zeros(st_all_ref.shape[1:], F32)
    rows = u_ref.shape[0]
    steps_per_rows = rows // (SAMPLE_STATE_BLOCK * nt)
    sub = pl.program_id(0) % steps_per_rows
    rg = SAMPLE_ROW_GROUP
    seq_per_group = rg // nt

    @pl.when(sub == 0)
    def _():
        u = u_ref[...]
        g = _gla_gate(u[:, _LRO:_LRO + LANES], wa2_ref, gba_ref[...])
        bc = _group_cumsum(g, nt)
        rin = lax.broadcasted_iota(jnp.int32, g.shape, 0) % nt
        sfx = jnp.zeros_like(g)
        for k in range(1, nt):
            sfx = sfx + jnp.where(rin + k < nt, pltpu.roll(g, rows - k, 0), 0.0)
        qt = u[:, _QO:_QO + GLA_DK_TOTAL] * (jnp.exp(bc) * (GLA_DK ** -0.5))
        uk = u[:, _KO:_KO + GLA_DK_TOTAL]
        kt = uk * jnp.exp(-bc)
        kd = uk * jnp.exp(sfx)
        q_s[...] = qt.astype(BF16)
        v_s[...] = u[:, _VO:_VO + GLA_DV_TOTAL].astype(BF16)
        bl_s[...] = bc + sfx
        ri = lax.broadcasted_iota(jnp.int32, (rows, rows), 0)
        ci = lax.broadcasted_iota(jnp.int32, (rows, rows), 1)
        same_seq_causal = (ri // nt == ci // nt) & (ri >= ci)
        for hd in range(GLA_HEADS):
            ks = slice(hd * GLA_DK, (hd + 1) * GLA_DK)
            att = jnp.where(same_seq_causal, _dot_nt(qt[:, ks], kt[:, ks]), 0.0)
            o_s[:, hd * GLA_DV:(hd + 1) * GLA_DV] = _dot(att, v_s[:, hd * GLA_DV:(hd + 1) * GLA_DV])
            kdt_s[hd] = kd[:, ks].T

    lane_seq = lax.broadcasted_iota(jnp.int32, (GLA_DK, rows), 1) // nt
    row_seq = lax.broadcasted_iota(jnp.int32, (rg, GLA_DV), 0) // nt

    def group(gi, carry):
        r0 = pl.multiple_of((sub * (SAMPLE_STATE_BLOCK // seq_per_group) + gi) * rg, rg)
        dec_g = jnp.exp(bl_s[pl.ds(r0, rg), :])
        for hd in range(GLA_HEADS):
            ks = slice(hd * GLA_DK, (hd + 1) * GLA_DK)
            vsl = slice(hd * GLA_DV, (hd + 1) * GLA_DV)
            qg = q_s[pl.ds(r0, rg), ks]
            vh = v_s[:, vsl]
            inter = jnp.zeros((rg, GLA_DV), F32)
            for j in range(seq_per_group):
                s_loc = gi * seq_per_group + j
                s_row = sub * SAMPLE_STATE_BLOCK + s_loc
                s_old = s0_ref[s_loc, hd]
                inter = jnp.where(row_seq == j, _dot(qg, s_old), inter)
                kdt = jnp.where(lane_seq == s_row, kdt_s[hd], 0.0)
                dm = _col_bcast(dec_g[j * nt:j * nt + 1, ks])
                st_ref[s_loc, hd] = s_old * jnp.concatenate([dm, dm], axis=1) + _dot(kdt, vh)
            o_s[pl.ds(r0, rg), vsl] = o_s[pl.ds(r0, rg), vsl] + inter
        return carry

    lax.fori_loop(0, SAMPLE_STATE_BLOCK // seq_per_group, group, 0)

    @pl.when(sub == steps_per_rows - 1)
    def _():
        yb_ref[...] = _head_norm_gate(o_s[...], gng_ref[...], u_ref[:, _RO:_RO + D_MODEL])


def _sample_gla(u_sm, state_all, new_state_all, w, l, nt):
    rows = u_sm.shape[0]
    br = SAMPLE_ROW_BLOCK
    sb = SAMPLE_STATE_BLOCK
    per = br // (sb * nt)
    state_spec = pl.BlockSpec((None, sb, GLA_HEADS, GLA_DK, GLA_DV), lambda i: (l, i, 0, 0, 0))
    out_state_spec = state_spec
    if new_state_all is None:
        out_state_spec = pl.BlockSpec((state_all.shape[0], sb, GLA_HEADS, GLA_DK, GLA_DV), lambda i: (0, i, 0, 0, 0))
    in_specs = [
        pl.BlockSpec((br, N_IN_B), lambda i: (i // per, 0)),
        _wspec(l, (LANES, GLA_DK_TOTAL)),
        _wspec(l, (1, GLA_DK_TOTAL)),
        _wspec(l, (1, GLA_DV_TOTAL)),
        state_spec,
    ]
    args = [u_sm, w["wa2"], w["gla_ba"], w["gla_norm_g"], state_all]
    aliases = {}
    if new_state_all is not None:
        in_specs.append(pl.BlockSpec(memory_space=pl.ANY))
        args.append(new_state_all)
        aliases = {len(args) - 1: 1}
    return pl.pallas_call(
        functools.partial(_sample_gla_kernel, nt=nt, layer=l, carries_buffer=new_state_all is not None),
        grid=(rows // (sb * nt),),
        in_specs=in_specs,
        out_specs=[
            pl.BlockSpec((br, D_MODEL), lambda i: (i // per, 0)),
            out_state_spec,
        ],
        out_shape=[
            jax.ShapeDtypeStruct((rows, D_MODEL), F32),
            jax.ShapeDtypeStruct(state_all.shape, F32),
        ],
        input_output_aliases=aliases,
        scratch_shapes=[
            pltpu.VMEM((br, GLA_DV_TOTAL), F32),
            pltpu.VMEM((br, GLA_DK_TOTAL), BF16),
            pltpu.VMEM((br, GLA_DV_TOTAL), BF16),
            pltpu.VMEM((GLA_HEADS, GLA_DK, br), F32),
            pltpu.VMEM((br, GLA_DK_TOTAL), F32),
        ],
        compiler_params=_cparams(("arbitrary",)),
        name="sample_gla",
    )(*args)


def _sample_out_kernel(x_ref, mod_ref, ug_ref, ya_ref, yb_ref, pb_ref, wo_ref, g2_ref, w1_ref, w2_ref, fg_ref, o_ref, *,
                       final_norm):
    rows = x_ref.shape[0]
    ns = mod_ref.shape[0]
    nt = rows // ns
    m = mod_ref[...]
    tile = lambda v: jnp.concatenate([v] * nt, axis=0)
    yb = jnp.dot(_row_regroup(rows, ns, nt), yb_ref[...].astype(BF16), preferred_element_type=F32).astype(BF16)
    x1 = _merge_out(x_ref[...], tile(_mod_part(m, 2)), ug_ref[:, :D_MODEL], ug_ref[:, D_MODEL:], ya_ref[...], yb,
                    pb_ref, wo_ref)
    o_ref[...] = _ffn_rows(x1, tile(_mod_part(m, 3)), tile(_mod_part(m, 4)), tile(_mod_part(m, 5)), g2_ref[...],
                           w1_ref, w2_ref, fg_ref[...], final_norm)


def _sample_out(x_tm, mod_s, ug_tm, ya, yb_sm, w, l, final_g, final_norm):
    rows = x_tm.shape[0]
    return pl.pallas_call(
        functools.partial(_sample_out_kernel, final_norm=final_norm),
        grid=(1,),
        in_specs=[
            _full_spec((rows, D_MODEL)),
            _full_spec(mod_s.shape),
            _full_spec((rows, 2 * D_MODEL)),
            _full_spec((rows, D_MODEL)),
            _full_spec((rows, D_MODEL)),
            _wspec(l, (GLA_DV_TOTAL, D_MODEL)),
            _wspec(l, (D_MODEL, D_MODEL)),
            _wspec(l, (1, D_MODEL)),
            _wspec(l, (D_MODEL, 2 * D_FF)),
            _wspec(l, (D_FF, D_MODEL)),
            _full_spec((1, D_MODEL)),
        ],
        out_specs=_full_out((rows, D_MODEL)),
        out_shape=jax.ShapeDtypeStruct((rows, D_MODEL), F32),
        compiler_params=_cparams(("arbitrary",)),
        name="sample_out_final" if final_norm else "sample_out",
    )(x_tm, mod_s, ug_tm, ya, yb_sm, w["proj_b"], w["w_out"], w["g2"], w["ffn_w1"], w["ffn_w2"], final_g)


def _stacked_weights(norm1_g, norm2_g, w_in, conv_w, conv_b, lru_wa, lru_ba, lru_wx, lru_bx, lru_lambda,
                     gla_wa2, gla_ba, gla_norm_g, proj_a, proj_b, w_out, ffn_w1, ffn_w2):
    depth = w_in.shape[0]
    row = lambda v: v.reshape(depth, 1, -1)
    return {
        "g1": row(norm1_g), "g2": row(norm2_g),
        "win_t": jnp.swapaxes(w_in, 1, 2).astype(BF16),
        "conv_w": conv_w, "conv_b": row(conv_b),
        "wgate": jnp.concatenate([lru_wa, lru_wx], axis=-1).astype(BF16),
        "lru_ba": row(lru_ba), "lru_bx": row(lru_bx), "lru_lambda": row(lru_lambda),
        "wa2": jnp.pad(gla_wa2, ((0, 0), (0, LANES - GLA_RANK), (0, 0))).astype(BF16),
        "gla_ba": row(gla_ba), "gla_norm_g": row(gla_norm_g),
        "proj_a": proj_a.astype(BF16), "proj_b": proj_b.astype(BF16), "w_out": w_out.astype(BF16),
        "ffn_w1": ffn_w1.astype(BF16), "ffn_w2": ffn_w2.astype(BF16),
    }


def kernel(x_prompt, x_sample, c_prompt, c_sample, state_conv, state_lru, state_gla, norm1_g, norm2_g, ada_w, ada_b,
           w_in, conv_w, conv_b, lru_wa, lru_ba, lru_wx, lru_bx, lru_lambda, gla_wa2, gla_ba, gla_norm_g, proj_a,
           proj_b, w_out, ffn_w1, ffn_w2, final_g):
    bp, tp, _ = x_prompt.shape
    bs, ts, _ = x_sample.shape
    depth = w_in.shape[0]
    mod = _modulation(jnp.concatenate([c_prompt, c_sample], axis=0), ada_w, ada_b)
    w = _stacked_weights(norm1_g, norm2_g, w_in, conv_w, conv_b, lru_wa, lru_ba, lru_wx, lru_bx, lru_lambda,
                         gla_wa2, gla_ba, gla_norm_g, proj_a, proj_b, w_out, ffn_w1, ffn_w2)
    fg = final_g.reshape(1, D_MODEL)
    xp = x_prompt
    xs = x_sample.transpose(1, 0, 2).reshape(ts * bs, D_MODEL)
    conv_p, lru_p, gla_p, conv_s, lru_s = [], [], [], [], []
    gla_s = None
    for l in range(depth):
        last = l == depth - 1
        mod_p = mod[l, :bp].reshape(bp, 1, N_MOD)
        mod_s = mod[l, bp:]
        x1, cb, ht, st = _prompt_mixer(xp, mod_p, w, l)
        xp = _ffn(x1.reshape(bp * tp, D_MODEL), mod_p, tp, FFN_TILE, w, l, fg, last).reshape(bp, tp, D_MODEL)
        conv_p.append(cb)
        lru_p.append(ht.reshape(bp, D_RNN))
        gla_p.append(st)
        ya_s, cb_s, ht_s, u_sm, ug_tm = _sample_in(xs, mod_s, state_conv[l].transpose(1, 0, 2), state_lru, w, l)
        yb_sm, gla_s = _sample_gla(u_sm, state_gla, gla_s, w, l, ts)
        xs = _sample_out(xs, mod_s, ug_tm, ya_s, yb_sm, w, l, fg, last)
        conv_s.append(cb_s.transpose(1, 0, 2))
        lru_s.append(ht_s)
    y_sample = xs.reshape(ts, bs, D_MODEL).transpose(1, 0, 2)
    return (xp, y_sample, jnp.stack(conv_p), jnp.stack(lru_p), jnp.stack(gla_p),
            jnp.stack(conv_s), jnp.stack(lru_s), gla_s)
```

```python
import functools

import jax
import jax.numpy as jnp
from jax import lax
from jax.experimental import pallas as pl
from jax.experimental.pallas import tpu as pltpu

F32 = jnp.float32
BF16 = jnp.bfloat16

D_MODEL = 1024
D_RNN = 1280
LRU_BW = 128
LRU_BLOCKS = D_RNN // LRU_BW
CONV_W = 4
LRU_C = 8.0
GLA_HEADS = 4
GLA_DK = 128
GLA_DV = 256
GLA_DK_TOTAL = GLA_HEADS * GLA_DK
GLA_DV_TOTAL = GLA_HEADS * GLA_DV
GLA_RANK = 16
GLA_TAU = 16.0
D_FF = 2816
EPS = 1e-6
TINY_F32 = 1.1754944e-38
N_MOD = 6 * D_MODEL

LANES = 128
SUBLANES = 8
VMEM_LIMIT_BYTES = 60 * 1024 * 1024

W_X, W_G, W_Q, W_K, W_V, W_R, W_LR, W_GA, W_GB = 0, 1280, 2560, 3072, 3584, 4608, 5632, 5648, 6672
N_IN = W_GB + D_MODEL
_LRO, _QO, _KO, _VO, _RO = 0, 128, 640, 1152, 2176
N_IN_B = _RO + D_MODEL

PROMPT_TILE = 512
GLA_CHUNK = 128
GLA_SPLITS = (64, 32)
GLA_FINE = 32
FFN_TILE = 512
VRG_PIECE = 512
LRU_PAIR = 2 * LRU_BW
SAMPLE_ROW_BLOCK = 128
SAMPLE_STATE_BLOCK = 8
SAMPLE_ROW_GROUP = 16


def _cparams(sem):
    return pltpu.CompilerParams(dimension_semantics=sem, vmem_limit_bytes=VMEM_LIMIT_BYTES)


def _full_spec(shape):
    n = len(shape)
    return pl.BlockSpec(shape, lambda *_: (0,) * n, pipeline_mode=pl.Buffered(1))


def _full_out(shape):
    n = len(shape)
    return pl.BlockSpec(shape, lambda *_: (0,) * n)


def _wspec(l, shape):
    n = len(shape)
    return pl.BlockSpec((None,) + tuple(shape), lambda *_: (l,) + (0,) * n, pipeline_mode=pl.Buffered(1))


def _softplus(y):
    return jnp.maximum(y, 0.0) + jnp.log1p(jnp.exp(-jnp.abs(y)))


def _sigmoid(y):
    return 0.5 * jnp.tanh(0.5 * y) + 0.5


def _silu(y):
    t = 0.5 * y
    return t + t * jnp.tanh(t)


def _sqrt_nonneg(s):
    return s * lax.rsqrt(jnp.maximum(s, TINY_F32))


def _gelu_tanh(y):
    return 0.5 * y * (1.0 + jnp.tanh(0.7978845608028654 * (y + 0.044715 * (y * y * y))))


def _rms(x):
    return lax.rsqrt(jnp.mean(x * x, axis=-1, keepdims=True) + EPS)


def _norm_mod(x, g, scale, shift):
    return (x * _rms(x)) * (g * (1.0 + scale)) + shift


def _mod_part(m, i):
    return m[:, i * D_MODEL:(i + 1) * D_MODEL]


def _dot(a, b):
    return jnp.dot(a.astype(BF16), b.astype(BF16), preferred_element_type=F32)


def _dot_nt(a, b):
    return lax.dot_general(a.astype(BF16), b.astype(BF16), (((1,), (1,)), ((), ())),
                           preferred_element_type=F32)


def _lru_coeffs_block(xb, wg, ba, bx, lam):
    lamc = -LRU_C * _softplus(-lam)
    pre = _dot(xb, wg)
    r = _sigmoid(pre[:, :LRU_BW] + ba)
    i = _sigmoid(pre[:, LRU_BW:] + bx)
    a = jnp.exp(lamc * r)
    return a, _sqrt_nonneg(1.0 - a * a) * (i * xb)


def _gla_gate(ulr, wa2_ref, gba):
    z = _dot(ulr, wa2_ref[...]) + gba
    return (jnp.minimum(z, 0.0) - jnp.log(1.0 + jnp.exp(-jnp.abs(z)))) * (1.0 / GLA_TAU)


def _group_cumsum(g, group):
    rin = lax.broadcasted_iota(jnp.int32, g.shape, 0) % group
    x = g
    k = 1
    while k < group:
        x = x + jnp.where(rin >= k, pltpu.roll(x, k, 0), 0.0)
        k *= 2
    return x


def _col_bcast(row):
    return jnp.broadcast_to(row, (LANES, LANES)).T


def _head_norm_gate(o, gng, ur):
    parts = []
    for hd in range(GLA_HEADS):
        sl = slice(hd * GLA_DV, (hd + 1) * GLA_DV)
        oh = o[:, sl]
        parts.append(oh * _rms(oh) * gng[:, sl])
    return jnp.concatenate(parts, axis=1) * _silu(ur)


def _merge_out(x, gt1, uga, ugb, ya, yb, pb_ref, wo_ref):
    pbv = _dot(yb, pb_ref[...])
    mm = _sigmoid(uga) * ya + _sigmoid(ugb) * pbv
    return x + gt1 * _dot(mm, wo_ref[...])


def _mod_kernel(c_ref, w_ref, b_ref, o_ref):
    o_ref[0] = _dot(_silu(c_ref[...]), w_ref[0]) + b_ref[0]


def _modulation(c_all, ada_w, ada_b):
    depth = ada_w.shape[0]
    rows = c_all.shape[0]
    nblk = N_MOD // D_MODEL
    return pl.pallas_call(
        _mod_kernel,
        grid=(depth, nblk),
        in_specs=[
            pl.BlockSpec((rows, D_MODEL), lambda l, j: (0, 0)),
            pl.BlockSpec((1, D_MODEL, D_MODEL), lambda l, j: (l, 0, j)),
            pl.BlockSpec((1, 1, D_MODEL), lambda l, j: (l, 0, j)),
        ],
        out_specs=pl.BlockSpec((1, rows, D_MODEL), lambda l, j: (l, 0, j)),
        out_shape=jax.ShapeDtypeStruct((depth, rows, N_MOD), F32),
        compiler_params=_cparams(("arbitrary", "arbitrary")),
        name="modulation",
    )(c_all, ada_w, ada_b.reshape(depth, 1, N_MOD))


def _shift_rows_in(blk, first_row):
    top = lax.broadcasted_iota(jnp.int32, blk.shape, 0) == 0
    return jnp.where(top, first_row, pltpu.roll(blk, 1, 0))


def _regroup_matrices(tt):
    ng = tt // SUBLANES
    ri = lax.broadcasted_iota(jnp.int32, (tt, tt), 0)
    ci = lax.broadcasted_iota(jnp.int32, (tt, tt), 1)
    to_blocks = jnp.where(ci == SUBLANES * (ri % ng) + ri // ng, 1.0, 0.0).astype(BF16)
    from_blocks = jnp.where(ci == ng * (ri % SUBLANES) + ri // SUBLANES, 1.0, 0.0).astype(BF16)
    return to_blocks, from_blocks


def _lru_conv(ux, cw, cb, conv_rows):
    ng = ux.shape[0] // SUBLANES
    first_tail = SUBLANES - (CONV_W - 1)
    xs = [ux[j * ng:(j + 1) * ng, :] for j in range(SUBLANES)]
    prev = {k: _shift_rows_in(xs[k], conv_rows[k - first_tail:k - first_tail + 1, :])
            for k in range(first_tail, SUBLANES)}
    xcs = []
    for j in range(SUBLANES):
        acc = cb
        for i in range(CONV_W):
            d = CONV_W - 1 - i
            src = xs[j - d] if j >= d else prev[j - d + SUBLANES]
            acc = acc + src * cw[i:i + 1, :]
        xcs.append(acc)
    tail = jnp.concatenate([xs[k][ng - 1:ng, :] for k in range(first_tail, SUBLANES)], axis=0)
    return jnp.concatenate(xcs, axis=0), tail


def _lru_scan(a, bb, carry):
    ng = a.shape[0] // SUBLANES
    loc = [bb[0:ng, :]]
    cum = [a[0:ng, :]]
    for j in range(1, SUBLANES):
        aj = a[j * ng:(j + 1) * ng, :]
        loc.append(aj * loc[-1] + bb[j * ng:(j + 1) * ng, :])
        cum.append(aj * cum[-1])
    ga, gb = cum[-1], loc[-1]
    rowg = lax.broadcasted_iota(jnp.int32, ga.shape, 0)
    k = 1
    while k < ng:
        ga_sh = jnp.where(rowg >= k, pltpu.roll(ga, k, 0), 1.0)
        gb_sh = jnp.where(rowg >= k, pltpu.roll(gb, k, 0), 0.0)
        gb = ga * gb_sh + gb
        ga = ga * ga_sh
        k *= 2
    leaving = gb + ga * carry
    entering = _shift_rows_in(leaving, carry)
    hs = jnp.concatenate([loc[j] + cum[j] * entering for j in range(SUBLANES)], axis=0)
    return hs, leaving[ng - 1:ng, :]


def _lru_lane_block(n, uan, cw_ref, cb_ref, wg_ref, ba_ref, bx_ref, lam_ref, conv_s, hc_s):
    sl = slice(n * LRU_BW, (n + 1) * LRU_BW)
    xc, tail = _lru_conv(uan[0], cw_ref[:, sl], cb_ref[:, sl], conv_s[0:CONV_W - 1, sl])
    conv_s[0:CONV_W - 1, sl] = tail
    a, bb = _lru_coeffs_block(xc, wg_ref[n], ba_ref[:, sl], bx_ref[:, sl], lam_ref[:, sl])
    hs, last = _lru_scan(a, bb, hc_s[0:1, sl])
    hc_s[0:1, sl] = last
    return (hs * _gelu_tanh(uan[1])).astype(BF16)


def _rows_per_block(x, size, pick):
    parts = []
    for b in range(x.shape[0] // size):
        r = pick(b)
        row = jnp.zeros((1, x.shape[1]), x.dtype) if r is None else x[r:r + 1, :]
        parts.append(jnp.broadcast_to(row, (size, x.shape[1])))
    return jnp.concatenate(parts, axis=0)


def _gla_prep(ulr, uq, uk, wa2_ref, gba):
    c = GLA_CHUNK
    g = _gla_gate(ulr, wa2_ref, gba)
    bc = _group_cumsum(g, c)
    qs = uq * (GLA_DK ** -0.5)
    factors = []
    for size in GLA_SPLITS:
        before = _rows_per_block(bc, size, lambda b: None if (b * size) % c == 0 else b * size - 1)
        last = _rows_per_block(bc, size, lambda b: b * size + size - 1)
        factors.append((qs * jnp.exp(bc - before), uk * jnp.exp(last - bc)))
    mid = _rows_per_block(bc, GLA_FINE, lambda b: b * GLA_FINE + GLA_FINE // 2 - 1)
    factors.append((qs * jnp.exp(bc - mid), uk * jnp.exp(mid - bc)))
    return bc, qs * jnp.exp(bc), factors


def _gla_level_masks():
    c = GLA_CHUNK
    ri = lax.broadcasted_iota(jnp.int32, (c, c), 0)
    ci = lax.broadcasted_iota(jnp.int32, (c, c), 1)
    masks = [(ri // (2 * size) == ci // (2 * size)) & ((ri // size) % 2 == 1) & ((ci // size) % 2 == 0)
             for size in GLA_SPLITS]
    masks.append((ri // GLA_FINE == ci // GLA_FINE) & (ri >= ci))
    return masks


def _gla_chunks(bc, qt, factors, uk, uv, s_s):
    tt = bc.shape[0]
    c = GLA_CHUNK
    masks = _gla_level_masks()
    o_rows = []
    for ch in range(tt // c):
        rows = slice(ch * c, (ch + 1) * c)
        bl = bc[ch * c + c - 1:ch * c + c, :]
        kd = uk[rows, :] * jnp.exp(bl - bc[rows, :])
        dec = jnp.exp(bl)
        o_heads = []
        for hd in range(GLA_HEADS):
            ks = slice(hd * GLA_DK, (hd + 1) * GLA_DK)
            qh = qt[rows, ks].astype(BF16)
            vh = uv[rows, hd * GLA_DV:(hd + 1) * GLA_DV].astype(BF16)
            att = jnp.zeros((c, c), F32)
            for (qf, kf), mask in zip(factors, masks):
                att = jnp.where(mask, _dot_nt(qf[rows, ks], kf[rows, ks]), att)
            s_old = s_s[hd]
            o_heads.append(_dot(att, vh) + _dot(qh, s_old))
            dm = _col_bcast(dec[:, ks])
            s_s[hd] = s_old * jnp.concatenate([dm, dm], axis=1) + _dot(kd[:, ks].T, vh)
        o_rows.append(jnp.concatenate(o_heads, axis=1))
    return jnp.concatenate(o_rows, axis=0)


def _prompt_mixer_kernel(x_ref, mod_ref, g1_ref, wt_ref, cw_ref, cb_ref, wg_ref, ba_ref, bx_ref, lam_ref,
                         pa_ref, wa2_ref, gba_ref, gng_ref, pb_ref, wo_ref,
                         x1_ref, conv_ref, lru_ref, st_ref, conv_s, hc_s, s_s, uvrg_s):
    t = pl.program_id(1)

    @pl.when(t == 0)
    def _():
        conv_s[...] = jnp.zeros_like(conv_s)
        hc_s[...] = jnp.zeros_like(hc_s)
        s_s[...] = jnp.zeros_like(s_s)

    x = x_ref[0]
    mod = mod_ref[0]
    hb16 = _norm_mod(x, g1_ref[...], _mod_part(mod, 1), _mod_part(mod, 0)).astype(BF16)
    to_blocks, from_blocks = _regroup_matrices(x.shape[0])
    u_qk = _dot_nt(hb16, wt_ref[W_Q:W_V, :])
    u_lr = _dot_nt(hb16, wt_ref[W_LR:W_LR + LANES, :])
    hb = jnp.dot(to_blocks, hb16, preferred_element_type=F32).astype(BF16)
    uk = u_qk[:, GLA_DK_TOTAL:]
    bc, qt, factors = _gla_prep(u_lr, u_qk[:, :GLA_DK_TOTAL], uk, wa2_ref, gba_ref[...])
    piece_rows = ([W_V + i * VRG_PIECE for i in range((W_LR - W_V) // VRG_PIECE)]
                  + [W_GA + i * VRG_PIECE for i in range((N_IN - W_GA) // VRG_PIECE)])
    per_pair = -(-len(piece_rows) // (LRU_BLOCKS // 2))
    ya_blocks = []
    for m in range(LRU_BLOCKS // 2):
        ux2 = _dot_nt(hb, wt_ref[W_X + m * LRU_PAIR:W_X + (m + 1) * LRU_PAIR, :])
        ug2 = _dot_nt(hb, wt_ref[W_G + m * LRU_PAIR:W_G + (m + 1) * LRU_PAIR, :])
        for i in range(m * per_pair, min((m + 1) * per_pair, len(piece_rows))):
            uvrg_s[:, i * VRG_PIECE:(i + 1) * VRG_PIECE] = _dot_nt(hb16, wt_ref[piece_rows[i]:piece_rows[i] + VRG_PIECE, :])
        for j in range(2):
            uan = (ux2[:, j * LRU_BW:(j + 1) * LRU_BW], ug2[:, j * LRU_BW:(j + 1) * LRU_BW])
            ya_blocks.append(_lru_lane_block(2 * m + j, uan, cw_ref, cb_ref, wg_ref, ba_ref, bx_ref, lam_ref,
                                             conv_s, hc_s))
    o = _gla_chunks(bc, qt, factors, uk, uvrg_s, s_s)
    ya = jnp.dot(from_blocks, jnp.concatenate(ya_blocks, axis=1), preferred_element_type=F32).astype(BF16)
    ya = jnp.dot(ya, pa_ref[...], preferred_element_type=F32)
    yb = _head_norm_gate(o, gng_ref[...], uvrg_s[:, GLA_DV_TOTAL:GLA_DV_TOTAL + D_MODEL])
    x1_ref[0] = _merge_out(x, _mod_part(mod, 2), uvrg_s[:, 2 * D_MODEL:3 * D_MODEL], uvrg_s[:, 3 * D_MODEL:],
                           ya, yb, pb_ref, wo_ref)

    @pl.when(t == pl.num_programs(1) - 1)
    def _():
        conv_ref[0] = conv_s[0:CONV_W - 1, :]
        lru_ref[0] = hc_s[0:1, :]
        st_ref[0] = s_s[...]


def _prompt_mixer(x, mod_p, w, l):
    b, t, _ = x.shape
    tt = PROMPT_TILE
    per_seq = t // tt
    seq_spec = pl.BlockSpec((1, tt, D_MODEL), lambda i, j: (i, j, 0))
    return pl.pallas_call(
        _prompt_mixer_kernel,
        grid=(b, per_seq),
        in_specs=[
            seq_spec,
            pl.BlockSpec((1, 1, N_MOD), lambda i, j: (i, 0, 0)),
            _wspec(l, (1, D_MODEL)),
            _wspec(l, (N_IN, D_MODEL)),
            _wspec(l, (CONV_W, D_RNN)),
            _wspec(l, (1, D_RNN)),
            _wspec(l, (LRU_BLOCKS, LRU_BW, 2 * LRU_BW)),
            _wspec(l, (1, D_RNN)),
            _wspec(l, (1, D_RNN)),
            _wspec(l, (1, D_RNN)),
            _wspec(l, (D_RNN, D_MODEL)),
            _wspec(l, (LANES, GLA_DK_TOTAL)),
            _wspec(l, (1, GLA_DK_TOTAL)),
            _wspec(l, (1, GLA_DV_TOTAL)),
            _wspec(l, (GLA_DV_TOTAL, D_MODEL)),
            _wspec(l, (D_MODEL, D_MODEL)),
        ],
        out_specs=[
            seq_spec,
            pl.BlockSpec((1, CONV_W - 1, D_RNN), lambda i, j: (i, 0, 0)),
            pl.BlockSpec((1, 1, D_RNN), lambda i, j: (i, 0, 0)),
            pl.BlockSpec((1, GLA_HEADS, GLA_DK, GLA_DV), lambda i, j: (i, 0, 0, 0)),
        ],
        out_shape=[
            jax.ShapeDtypeStruct((b, t, D_MODEL), F32),
            jax.ShapeDtypeStruct((b, CONV_W - 1, D_RNN), F32),
            jax.ShapeDtypeStruct((b, 1, D_RNN), F32),
            jax.ShapeDtypeStruct((b, GLA_HEADS, GLA_DK, GLA_DV), F32),
        ],
        scratch_shapes=[
            pltpu.VMEM((SUBLANES, D_RNN), F32),
            pltpu.VMEM((SUBLANES, D_RNN), F32),
            pltpu.VMEM((GLA_HEADS, GLA_DK, GLA_DV), F32),
            pltpu.VMEM((tt, 2 * GLA_DV_TOTAL + 2 * D_MODEL), F32),
        ],
        compiler_params=_cparams(("arbitrary", "arbitrary")),
        name="prompt_mixer",
    )(x, mod_p, w["g1"], w["win_t"], w["conv_w"], w["conv_b"], w["wgate"], w["lru_ba"], w["lru_bx"],
      w["lru_lambda"], w["proj_a"], w["wa2"], w["gla_ba"], w["gla_norm_g"], w["proj_b"], w["w_out"])


def _ffn_rows(x, shift, scale, gate, g2, w1_ref, w2_ref, fg, final_norm):
    h2 = _norm_mod(x, g2, scale, shift)
    hb = h2.astype(BF16)
    f1 = _dot(hb, w1_ref[:, :D_FF])
    f2 = _dot(hb, w1_ref[:, D_FF:])
    x2 = x + gate * _dot(_silu(f1) * f2, w2_ref[...])
    if final_norm:
        x2 = x2 * _rms(x2) * fg
    return x2


def _ffn_kernel(x_ref, mod_ref, g2_ref, w1_ref, w2_ref, fg_ref, o_ref, *, final_norm):
    m = mod_ref[0]
    o_ref[...] = _ffn_rows(x_ref[...], _mod_part(m, 3), _mod_part(m, 4), _mod_part(m, 5), g2_ref[...], w1_ref, w2_ref,
                           fg_ref[...], final_norm)


def _ffn(x2d, mod3, rows_per_mod, tile, w, l, final_g, final_norm):
    mrows = x2d.shape[0]
    r = mod3.shape[1]
    per = rows_per_mod // tile
    row_spec = pl.BlockSpec((tile, D_MODEL), lambda i: (i, 0))
    return pl.pallas_call(
        functools.partial(_ffn_kernel, final_norm=final_norm),
        grid=(mrows // tile,),
        in_specs=[
            row_spec,
            pl.BlockSpec((1, r, N_MOD), lambda i: (i // per, 0, 0)),
            _wspec(l, (1, D_MODEL)),
            _wspec(l, (D_MODEL, 2 * D_FF)),
            _wspec(l, (D_FF, D_MODEL)),
            _full_spec((1, D_MODEL)),
        ],
        out_specs=row_spec,
        out_shape=jax.ShapeDtypeStruct((mrows, D_MODEL), F32),
        compiler_params=_cparams(("arbitrary",)),
        name="ffn_final" if final_norm else "ffn",
    )(x2d, mod3, w["g2"], w["ffn_w1"], w["ffn_w2"], final_g)


def _row_regroup(rows, inner, outer):
    ri = lax.broadcasted_iota(jnp.int32, (rows, rows), 0)
    ci = lax.broadcasted_iota(jnp.int32, (rows, rows), 1)
    return jnp.where(ci == (ri % inner) * outer + ri // inner, 1.0, 0.0).astype(BF16)


def _sample_in_kernel(x_ref, mod_ref, g1_ref, win_ref, cw_ref, cb_ref, wg_ref, ba_ref, bx_ref, lam_ref, pa_ref,
                      conv0_ref, h0_ref, ya_ref, conv_ref, lru_ref, u_ref, ug_ref):
    rows = x_ref.shape[0]
    ns = h0_ref.shape[0]
    nt = rows // ns
    m = mod_ref[...]
    tile = lambda v: jnp.concatenate([v] * nt, axis=0)
    h = _norm_mod(x_ref[...], g1_ref[...], tile(_mod_part(m, 1)), tile(_mod_part(m, 0))).astype(BF16)
    ux = _dot_nt(h, win_ref[W_X:W_G, :])
    ug = _dot_nt(h, win_ref[W_G:W_Q, :])
    xa = [conv0_ref[i] for i in range(CONV_W - 1)] + [ux[i * ns:(i + 1) * ns, :] for i in range(nt)]
    cw = cw_ref[...]
    xcs = []
    for ti in range(nt):
        acc = cb_ref[...]
        for i in range(CONV_W):
            acc = acc + xa[ti + i] * cw[i:i + 1, :]
        xcs.append(acc)
    xc = jnp.concatenate(xcs, axis=0)
    coeffs = [_lru_coeffs_block(xc[:, n * LRU_BW:(n + 1) * LRU_BW], wg_ref[n], ba_ref[:, n * LRU_BW:(n + 1) * LRU_BW],
                                bx_ref[:, n * LRU_BW:(n + 1) * LRU_BW], lam_ref[:, n * LRU_BW:(n + 1) * LRU_BW])
              for n in range(LRU_BLOCKS)]
    a = jnp.concatenate([c[0] for c in coeffs], axis=1)
    bb = jnp.concatenate([c[1] for c in coeffs], axis=1)
    hc = h0_ref[...]
    hs = []
    for ti in range(nt):
        sl = slice(ti * ns, (ti + 1) * ns)
        hc = a[sl, :] * hc + bb[sl, :]
        hs.append(hc)
    ya_ref[...] = _dot(jnp.concatenate(hs, axis=0) * _gelu_tanh(ug), pa_ref[...])
    for i in range(CONV_W - 1):
        conv_ref[i] = xa[nt + i]
    lru_ref[...] = hc
    ug_ref[...] = _dot_nt(h, win_ref[W_GA:N_IN, :])
    hs_major = jnp.dot(_row_regroup(rows, nt, ns), h, preferred_element_type=F32).astype(BF16)
    u_ref[...] = jnp.concatenate([_dot_nt(hs_major, win_ref[W_LR:W_LR + LANES, :]),
                                  _dot_nt(hs_major, win_ref[W_Q:W_LR, :])], axis=1)


def _sample_in(x_tm, mod_s, conv0_tm, h0_all, w, l):
    rows = x_tm.shape[0]
    ns = h0_all.shape[1]
    return pl.pallas_call(
        _sample_in_kernel,
        grid=(1,),
        in_specs=[
            _full_spec((rows, D_MODEL)),
            _full_spec((ns, N_MOD)),
            _wspec(l, (1, D_MODEL)),
            _wspec(l, (N_IN, D_MODEL)),
            _wspec(l, (CONV_W, D_RNN)),
            _wspec(l, (1, D_RNN)),
            _wspec(l, (LRU_BLOCKS, LRU_BW, 2 * LRU_BW)),
            _wspec(l, (1, D_RNN)),
            _wspec(l, (1, D_RNN)),
            _wspec(l, (1, D_RNN)),
            _wspec(l, (D_RNN, D_MODEL)),
            _full_spec((CONV_W - 1, ns, D_RNN)),
            _wspec(l, (ns, D_RNN)),
        ],
        out_specs=[
            _full_out((rows, D_MODEL)),
            _full_out((CONV_W - 1, ns, D_RNN)),
            _full_out((ns, D_RNN)),
            _full_out((rows, N_IN_B)),
            _full_out((rows, 2 * D_MODEL)),
        ],
        out_shape=[
            jax.ShapeDtypeStruct((rows, D_MODEL), F32),
            jax.ShapeDtypeStruct((CONV_W - 1, ns, D_RNN), F32),
            jax.ShapeDtypeStruct((ns, D_RNN), F32),
            jax.ShapeDtypeStruct((rows, N_IN_B), F32),
            jax.ShapeDtypeStruct((rows, 2 * D_MODEL), F32),
        ],
        compiler_params=_cparams(("arbitrary",)),
        name="sample_in",
    )(x_tm, mod_s, w["g1"], w["win_t"], w["conv_w"], w["conv_b"], w["wgate"], w["lru_ba"], w["lru_bx"],
      w["lru_lambda"], w["proj_a"], conv0_tm, h0_all)


def _sample_gla_kernel(u_ref, wa2_ref, gba_ref, gng_ref, s0_ref, *rest, nt, layer, carries_buffer):
    if carries_buffer:
        yb_ref, st_ref, o_s, q_s, v_s, kdt_s, bl_s = rest[1:]
    else:
        yb_ref, st_all_ref, o_s, q_s, v_s, kdt_s, bl_s = rest
        st_ref = st_all_ref.at[layer]
        for other in range(st_all_ref.shape[0]):
            if other != layer:
                st_all_ref[other] = jnp.zeros(st_all_ref.shape[1:], F32)
    rows = u_ref.shape[0]
    steps_per_rows = rows // (SAMPLE_STATE_BLOCK * nt)
    sub = pl.program_id(0) % steps_per_rows
    rg = SAMPLE_ROW_GROUP
    seq_per_group = rg // nt

    @pl.when(sub == 0)
    def _():
        u = u_ref[...]
        g = _gla_gate(u[:, _LRO:_LRO + LANES], wa2_ref, gba_ref[...])
        bc = _group_cumsum(g, nt)
        rin = lax.broadcasted_iota(jnp.int32, g.shape, 0) % nt
        sfx = jnp.zeros_like(g)
        for k in range(1, nt):
            sfx = sfx + jnp.where(rin + k < nt, pltpu.roll(g, rows - k, 0), 0.0)
        qt = u[:, _QO:_QO + GLA_DK_TOTAL] * (jnp.exp(bc) * (GLA_DK ** -0.5))
        uk = u[:, _KO:_KO + GLA_DK_TOTAL]
        kt = uk * jnp.exp(-bc)
        kd = uk * jnp.exp(sfx)
        q_s[...] = qt.astype(BF16)
        v_s[...] = u[:, _VO:_VO + GLA_DV_TOTAL].astype(BF16)
        bl_s[...] = bc + sfx
        ri = lax.broadcasted_iota(jnp.int32, (rows, rows), 0)
        ci = lax.broadcasted_iota(jnp.int32, (rows, rows), 1)
        same_seq_causal = (ri // nt == ci // nt) & (ri >= ci)
        for hd in range(GLA_HEADS):
            ks = slice(hd * GLA_DK, (hd + 1) * GLA_DK)
            att = jnp.where(same_seq_causal, _dot_nt(qt[:, ks], kt[:, ks]), 0.0)
            o_s[:, hd * GLA_DV:(hd + 1) * GLA_DV] = _dot(att, v_s[:, hd * GLA_DV:(hd + 1) * GLA_DV])
            kdt_s[hd] = kd[:, ks].T

    lane_seq = lax.broadcasted_iota(jnp.int32, (GLA_DK, rows), 1) // nt
    row_seq = lax.broadcasted_iota(jnp.int32, (rg, GLA_DV), 0) // nt

    def group(gi, carry):
        r0 = pl.multiple_of((sub * (SAMPLE_STATE_BLOCK // seq_per_group) + gi) * rg, rg)
        dec_g = jnp.exp(bl_s[pl.ds(r0, rg), :])
        for hd in range(GLA_HEADS):
            ks = slice(hd * GLA_DK, (hd + 1) * GLA_DK)
            vsl = slice(hd * GLA_DV, (hd + 1) * GLA_DV)
            qg = q_s[pl.ds(r0, rg), ks]
            vh = v_s[:, vsl]
            inter = jnp.zeros((rg, GLA_DV), F32)
            for j in range(seq_per_group):
                s_loc = gi * seq_per_group + j
                s_row = sub * SAMPLE_STATE_BLOCK + s_loc
                s_old = s0_ref[s_loc, hd]
                inter = jnp.where(row_seq == j, _dot(qg, s_old), inter)
                kdt = jnp.where(lane_seq == s_row, kdt_s[hd], 0.0)
                dm = _col_bcast(dec_g[j * nt:j * nt + 1, ks])
                st_ref[s_loc, hd] = s_old * jnp.concatenate([dm, dm], axis=1) + _dot(kdt, vh)
            o_s[pl.ds(r0, rg), vsl] = o_s[pl.ds(r0, rg), vsl] + inter
        return carry

    lax.fori_loop(0, SAMPLE_STATE_BLOCK // seq_per_group, group, 0)

    @pl.when(sub == steps_per_rows - 1)
    def _():
        yb_ref[...] = _head_norm_gate(o_s[...], gng_ref[...], u_ref[:, _RO:_RO + D_MODEL])


def _sample_gla(u_sm, state_all, new_state_all, w, l, nt):
    rows = u_sm.shape[0]
    br = SAMPLE_ROW_BLOCK
    sb = SAMPLE_STATE_BLOCK
    per = br // (sb * nt)
    state_spec = pl.BlockSpec((None, sb, GLA_HEADS, GLA_DK, GLA_DV), lambda i: (l, i, 0, 0, 0))
    out_state_spec = state_spec
    if new_state_all is None:
        out_state_spec = pl.BlockSpec((state_all.shape[0], sb, GLA_HEADS, GLA_DK, GLA_DV), lambda i: (0, i, 0, 0, 0))
    in_specs = [
        pl.BlockSpec((br, N_IN_B), lambda i: (i // per, 0)),
        _wspec(l, (LANES, GLA_DK_TOTAL)),
        _wspec(l, (1, GLA_DK_TOTAL)),
        _wspec(l, (1, GLA_DV_TOTAL)),
        state_spec,
    ]
    args = [u_sm, w["wa2"], w["gla_ba"], w["gla_norm_g"], state_all]
    aliases = {}
    if new_state_all is not None:
        in_specs.append(pl.BlockSpec(memory_space=pl.ANY))
        args.append(new_state_all)
        aliases = {len(args) - 1: 1}
    return pl.pallas_call(
        functools.partial(_sample_gla_kernel, nt=nt, layer=l, carries_buffer=new_state_all is not None),
        grid=(rows // (sb * nt),),
        in_specs=in_specs,
        out_specs=[
            pl.BlockSpec((br, D_MODEL), lambda i: (i // per, 0)),
            out_state_spec,
        ],
        out_shape=[
            jax.ShapeDtypeStruct((rows, D_MODEL), F32),
            jax.ShapeDtypeStruct(state_all.shape, F32),
        ],
        input_output_aliases=aliases,
        scratch_shapes=[
            pltpu.VMEM((br, GLA_DV_TOTAL), F32),
            pltpu.VMEM((br, GLA_DK_TOTAL), BF16),
            pltpu.VMEM((br, GLA_DV_TOTAL), BF16),
            pltpu.VMEM((GLA_HEADS, GLA_DK, br), F32),
            pltpu.VMEM((br, GLA_DK_TOTAL), F32),
        ],
        compiler_params=_cparams(("arbitrary",)),
        name="sample_gla",
    )(*args)


def _sample_out_kernel(x_ref, mod_ref, ug_ref, ya_ref, yb_ref, pb_ref, wo_ref, g2_ref, w1_ref, w2_ref, fg_ref, o_ref, *,
                       final_norm):
    rows = x_ref.shape[0]
    ns = mod_ref.shape[0]
    nt = rows // ns
    m = mod_ref[...]
    tile = lambda v: jnp.concatenate([v] * nt, axis=0)
    yb = jnp.dot(_row_regroup(rows, ns, nt), yb_ref[...].astype(BF16), preferred_element_type=F32).astype(BF16)
    x1 = _merge_out(x_ref[...], tile(_mod_part(m, 2)), ug_ref[:, :D_MODEL], ug_ref[:, D_MODEL:], ya_ref[...], yb,
                    pb_ref, wo_ref)
    o_ref[...] = _ffn_rows(x1, tile(_mod_part(m, 3)), tile(_mod_part(m, 4)), tile(_mod_part(m, 5)), g2_ref[...],
                           w1_ref, w2_ref, fg_ref[...], final_norm)


def _sample_out(x_tm, mod_s, ug_tm, ya, yb_sm, w, l, final_g, final_norm):
    rows = x_tm.shape[0]
    return pl.pallas_call(
        functools.partial(_sample_out_kernel, final_norm=final_norm),
        grid=(1,),
        in_specs=[
            _full_spec((rows, D_MODEL)),
            _full_spec(mod_s.shape),
            _full_spec((rows, 2 * D_MODEL)),
            _full_spec((rows, D_MODEL)),
            _full_spec((rows, D_MODEL)),
            _wspec(l, (GLA_DV_TOTAL, D_MODEL)),
            _wspec(l, (D_MODEL, D_MODEL)),
            _wspec(l, (1, D_MODEL)),
            _wspec(l, (D_MODEL, 2 * D_FF)),
            _wspec(l, (D_FF, D_MODEL)),
            _full_spec((1, D_MODEL)),
        ],
        out_specs=_full_out((rows, D_MODEL)),
        out_shape=jax.ShapeDtypeStruct((rows, D_MODEL), F32),
        compiler_params=_cparams(("arbitrary",)),
        name="sample_out_final" if final_norm else "sample_out",
    )(x_tm, mod_s, ug_tm, ya, yb_sm, w["proj_b"], w["w_out"], w["g2"], w["ffn_w1"], w["ffn_w2"], final_g)


def _stacked_weights(norm1_g, norm2_g, w_in, conv_w, conv_b, lru_wa, lru_ba, lru_wx, lru_bx, lru_lambda,
                     gla_wa2, gla_ba, gla_norm_g, proj_a, proj_b, w_out, ffn_w1, ffn_w2):
    depth = w_in.shape[0]
    row = lambda v: v.reshape(depth, 1, -1)
    return {
        "g1": row(norm1_g), "g2": row(norm2_g),
        "win_t": jnp.swapaxes(w_in, 1, 2).astype(BF16),
        "conv_w": conv_w, "conv_b": row(conv_b),
        "wgate": jnp.concatenate([lru_wa, lru_wx], axis=-1).astype(BF16),
        "lru_ba": row(lru_ba), "lru_bx": row(lru_bx), "lru_lambda": row(lru_lambda),
        "wa2": jnp.pad(gla_wa2, ((0, 0), (0, LANES - GLA_RANK), (0, 0))).astype(BF16),
        "gla_ba": row(gla_ba), "gla_norm_g": row(gla_norm_g),
        "proj_a": proj_a.astype(BF16), "proj_b": proj_b.astype(BF16), "w_out": w_out.astype(BF16),
        "ffn_w1": ffn_w1.astype(BF16), "ffn_w2": ffn_w2.astype(BF16),
    }


def kernel(x_prompt, x_sample, c_prompt, c_sample, state_conv, state_lru, state_gla, norm1_g, norm2_g, ada_w, ada_b,
           w_in, conv_w, conv_b, lru_wa, lru_ba, lru_wx, lru_bx, lru_lambda, gla_wa2, gla_ba, gla_norm_g, proj_a,
           proj_b, w_out, ffn_w1, ffn_w2, final_g):
    bp, tp, _ = x_prompt.shape
    bs, ts, _ = x_sample.shape
    depth = w_in.shape[0]
    mod = _modulation(jnp.concatenate([c_prompt, c_sample], axis=0), ada_w, ada_b)
    w = _stacked_weights(norm1_g, norm2_g, w_in, conv_w, conv_b, lru_wa, lru_ba, lru_wx, lru_bx, lru_lambda,
                         gla_wa2, gla_ba, gla_norm_g, proj_a, proj_b, w_out, ffn_w1, ffn_w2)
    fg = final_g.reshape(1, D_MODEL)
    xp = x_prompt
    xs = x_sample.transpose(1, 0, 2).reshape(ts * bs, D_MODEL)
    conv_p, lru_p, gla_p, conv_s, lru_s = [], [], [], [], []
    gla_s = None
    for l in range(depth):
        last = l == depth - 1
        mod_p = mod[l, :bp].reshape(bp, 1, N_MOD)
        mod_s = mod[l, bp:]
        x1, cb, ht, st = _prompt_mixer(xp, mod_p, w, l)
        xp = _ffn(x1.reshape(bp * tp, D_MODEL), mod_p, tp, FFN_TILE, w, l, fg, last).reshape(bp, tp, D_MODEL)
        conv_p.append(cb)
        lru_p.append(ht.reshape(bp, D_RNN))
        gla_p.append(st)
        ya_s, cb_s, ht_s, u_sm, ug_tm = _sample_in(xs, mod_s, state_conv[l].transpose(1, 0, 2), state_lru, w, l)
        yb_sm, gla_s = _sample_gla(u_sm, state_gla, gla_s, w, l, ts)
        xs = _sample_out(xs, mod_s, ug_tm, ya_s, yb_sm, w, l, fg, last)
        conv_s.append(cb_s.transpose(1, 0, 2))
        lru_s.append(ht_s)
    y_sample = xs.reshape(ts, bs, D_MODEL).transpose(1, 0, 2)
    return (xp, y_sample, jnp.stack(conv_p), jnp.stack(lru_p), jnp.stack(gla_p),
            jnp.stack(conv_s), jnp.stack(lru_s), gla_s)
```

```python
import functools

import jax
import jax.numpy as jnp
from jax import lax
from jax.experimental import pallas as pl
from jax.experimental.pallas import tpu as pltpu

F32 = jnp.float32
BF16 = jnp.bfloat16

D_MODEL = 1024
D_RNN = 1280
LRU_BW = 128
LRU_BLOCKS = D_RNN // LRU_BW
CONV_W = 4
LRU_C = 8.0
GLA_HEADS = 4
GLA_DK = 128
GLA_DV = 256
GLA_DK_TOTAL = GLA_HEADS * GLA_DK
GLA_DV_TOTAL = GLA_HEADS * GLA_DV
GLA_RANK = 16
GLA_TAU = 16.0
D_FF = 2816
EPS = 1e-6
TINY_F32 = 1.1754944e-38
N_MOD = 6 * D_MODEL

LANES = 128
SUBLANES = 8
VMEM_LIMIT_BYTES = 60 * 1024 * 1024

W_X, W_G, W_Q, W_K, W_V, W_R, W_LR, W_GA, W_GB = 0, 1280, 2560, 3072, 3584, 4608, 5632, 5648, 6672
N_IN = W_GB + D_MODEL
_LRO, _QO, _KO, _VO, _RO = 0, 128, 640, 1152, 2176
N_IN_B = _RO + D_MODEL

PROMPT_TILE = 512
GLA_CHUNK = 128
GLA_SPLITS = (64, 32)
GLA_FINE = 32
FFN_TILE = 512
VRG_PIECE = 512
LRU_PAIR = 2 * LRU_BW
SAMPLE_ROW_BLOCK = 128
SAMPLE_STATE_BLOCK = 8
SAMPLE_ROW_GROUP = 16


def _cparams(sem):
    return pltpu.CompilerParams(dimension_semantics=sem, vmem_limit_bytes=VMEM_LIMIT_BYTES)


def _full_spec(shape):
    n = len(shape)
    return pl.BlockSpec(shape, lambda *_: (0,) * n, pipeline_mode=pl.Buffered(1))


def _full_out(shape):
    n = len(shape)
    return pl.BlockSpec(shape, lambda *_: (0,) * n)


def _wspec(l, shape):
    n = len(shape)
    return pl.BlockSpec((None,) + tuple(shape), lambda *_: (l,) + (0,) * n, pipeline_mode=pl.Buffered(1))


def _softplus(y):
    return jnp.maximum(y, 0.0) + jnp.log1p(jnp.exp(-jnp.abs(y)))


def _sigmoid_of_half(t):
    return 0.5 * jnp.tanh(t) + 0.5


def _silu_of_half(t):
    return t + t * jnp.tanh(t)


def _silu(y):
    return _silu_of_half(0.5 * y)


def _sqrt_nonneg(s):
    return s * lax.rsqrt(jnp.maximum(s, TINY_F32))


def _gelu_tanh(y):
    return 0.5 * y * (1.0 + jnp.tanh(0.7978845608028654 * (y + 0.044715 * (y * y * y))))


def _rms(x):
    return lax.rsqrt(jnp.mean(x * x, axis=-1, keepdims=True) + EPS)


def _norm_mod(x, g, scale, shift):
    return (x * _rms(x)) * (g * (1.0 + scale)) + shift


def _mod_part(m, i):
    return m[:, i * D_MODEL:(i + 1) * D_MODEL]


def _dot(a, b):
    return jnp.dot(a.astype(BF16), b.astype(BF16), preferred_element_type=F32)


def _dot_nt(a, b):
    return lax.dot_general(a.astype(BF16), b.astype(BF16), (((1,), (1,)), ((), ())),
                           preferred_element_type=F32)


def _lru_coeffs_block(xb, wg, ba, bx, lam):
    lamc = -LRU_C * _softplus(-lam)
    pre = _dot(xb, wg)
    r = _sigmoid_of_half(pre[:, :LRU_BW] + ba)
    i = _sigmoid_of_half(pre[:, LRU_BW:] + bx)
    a = jnp.exp(lamc * r)
    return a, _sqrt_nonneg(1.0 - a * a) * (i * xb)


def _gla_gate(ulr, wa2_ref, gba):
    z = _dot(ulr, wa2_ref[...]) + gba
    return (jnp.minimum(z, 0.0) - jnp.log(1.0 + jnp.exp(-jnp.abs(z)))) * (1.0 / GLA_TAU)


def _group_cumsum(g, group):
    rin = lax.broadcasted_iota(jnp.int32, g.shape, 0) % group
    x = g
    k = 1
    while k < group:
        x = x + jnp.where(rin >= k, pltpu.roll(x, k, 0), 0.0)
        k *= 2
    return x


def _col_bcast(row):
    return jnp.broadcast_to(row, (LANES, LANES)).T


def _head_norm_gate(o, gng, ur_half):
    parts = []
    for hd in range(GLA_HEADS):
        sl = slice(hd * GLA_DV, (hd + 1) * GLA_DV)
        oh = o[:, sl]
        parts.append(oh * _rms(oh) * gng[:, sl])
    return jnp.concatenate(parts, axis=1) * _silu_of_half(ur_half)


def _merge_out(x, gt1, uga_half, ugb_half, ya, yb, pb_ref, wo_ref):
    pbv = _dot(yb, pb_ref[...])
    mm = _sigmoid_of_half(uga_half) * ya + _sigmoid_of_half(ugb_half) * pbv
    return x + gt1 * _dot(mm, wo_ref[...])


def _mod_kernel(c_ref, w_ref, b_ref, o_ref):
    o_ref[0] = _dot(_silu(c_ref[...]), w_ref[0]) + b_ref[0]


def _modulation(c_all, ada_w, ada_b):
    depth = ada_w.shape[0]
    rows = c_all.shape[0]
    nblk = N_MOD // D_MODEL
    return pl.pallas_call(
        _mod_kernel,
        grid=(depth, nblk),
        in_specs=[
            pl.BlockSpec((rows, D_MODEL), lambda l, j: (0, 0)),
            pl.BlockSpec((1, D_MODEL, D_MODEL), lambda l, j: (l, 0, j)),
            pl.BlockSpec((1, 1, D_MODEL), lambda l, j: (l, 0, j)),
        ],
        out_specs=pl.BlockSpec((1, rows, D_MODEL), lambda l, j: (l, 0, j)),
        out_shape=jax.ShapeDtypeStruct((depth, rows, N_MOD), F32),
        compiler_params=_cparams(("arbitrary", "arbitrary")),
        name="modulation",
    )(c_all, ada_w, ada_b.reshape(depth, 1, N_MOD))


def _shift_rows_in(blk, first_row):
    top = lax.broadcasted_iota(jnp.int32, blk.shape, 0) == 0
    return jnp.where(top, first_row, pltpu.roll(blk, 1, 0))


def _regroup_matrices(tt):
    ng = tt // SUBLANES
    ri = lax.broadcasted_iota(jnp.int32, (tt, tt), 0)
    ci = lax.broadcasted_iota(jnp.int32, (tt, tt), 1)
    to_blocks = jnp.where(ci == SUBLANES * (ri % ng) + ri // ng, 1.0, 0.0).astype(BF16)
    from_blocks = jnp.where(ci == ng * (ri % SUBLANES) + ri // SUBLANES, 1.0, 0.0).astype(BF16)
    return to_blocks, from_blocks


def _lru_conv(ux, cw, cb, conv_rows):
    ng = ux.shape[0] // SUBLANES
    first_tail = SUBLANES - (CONV_W - 1)
    xs = [ux[j * ng:(j + 1) * ng, :] for j in range(SUBLANES)]
    prev = {k: _shift_rows_in(xs[k], conv_rows[k - first_tail:k - first_tail + 1, :])
            for k in range(first_tail, SUBLANES)}
    xcs = []
    for j in range(SUBLANES):
        acc = cb
        for i in range(CONV_W):
            d = CONV_W - 1 - i
            src = xs[j - d] if j >= d else prev[j - d + SUBLANES]
            acc = acc + src * cw[i:i + 1, :]
        xcs.append(acc)
    tail = jnp.concatenate([xs[k][ng - 1:ng, :] for k in range(first_tail, SUBLANES)], axis=0)
    return jnp.concatenate(xcs, axis=0), tail


def _lru_scan(a, bb, carry):
    ng = a.shape[0] // SUBLANES
    loc = [bb[0:ng, :]]
    cum = [a[0:ng, :]]
    for j in range(1, SUBLANES):
        aj = a[j * ng:(j + 1) * ng, :]
        loc.append(aj * loc[-1] + bb[j * ng:(j + 1) * ng, :])
        cum.append(aj * cum[-1])
    ga, gb = cum[-1], loc[-1]
    rowg = lax.broadcasted_iota(jnp.int32, ga.shape, 0)
    k = 1
    while k < ng:
        ga_sh = jnp.where(rowg >= k, pltpu.roll(ga, k, 0), 1.0)
        gb_sh = jnp.where(rowg >= k, pltpu.roll(gb, k, 0), 0.0)
        gb = ga * gb_sh + gb
        ga = ga * ga_sh
        k *= 2
    leaving = gb + ga * carry
    entering = _shift_rows_in(leaving, carry)
    hs = jnp.concatenate([loc[j] + cum[j] * entering for j in range(SUBLANES)], axis=0)
    return hs, leaving[ng - 1:ng, :]


def _lru_lane_block(n, uan, cw_ref, cb_ref, wg_ref, ba_ref, bx_ref, lam_ref, conv_s, hc_s):
    sl = slice(n * LRU_BW, (n + 1) * LRU_BW)
    xc, tail = _lru_conv(uan[0], cw_ref[:, sl], cb_ref[:, sl], conv_s[0:CONV_W - 1, sl])
    conv_s[0:CONV_W - 1, sl] = tail
    a, bb = _lru_coeffs_block(xc, wg_ref[n], ba_ref[:, sl], bx_ref[:, sl], lam_ref[:, sl])
    hs, last = _lru_scan(a, bb, hc_s[0:1, sl])
    hc_s[0:1, sl] = last
    return (hs * _gelu_tanh(uan[1])).astype(BF16)


def _rows_per_block(x, size, pick):
    parts = []
    for b in range(x.shape[0] // size):
        r = pick(b)
        row = jnp.zeros((1, x.shape[1]), x.dtype) if r is None else x[r:r + 1, :]
        parts.append(jnp.broadcast_to(row, (size, x.shape[1])))
    return jnp.concatenate(parts, axis=0)


def _gla_prep(ulr, uq, uk, wa2_ref, gba):
    c = GLA_CHUNK
    g = _gla_gate(ulr, wa2_ref, gba)
    bc = _group_cumsum(g, c)
    qs = uq * (GLA_DK ** -0.5)
    factors = []
    for size in GLA_SPLITS:
        before = _rows_per_block(bc, size, lambda b: None if (b * size) % c == 0 else b * size - 1)
        last = _rows_per_block(bc, size, lambda b: b * size + size - 1)
        factors.append((qs * jnp.exp(bc - before), uk * jnp.exp(last - bc)))
    mid = _rows_per_block(bc, GLA_FINE, lambda b: b * GLA_FINE + GLA_FINE // 2 - 1)
    factors.append((qs * jnp.exp(bc - mid), uk * jnp.exp(mid - bc)))
    return bc, qs * jnp.exp(bc), factors


def _gla_level_masks():
    c = GLA_CHUNK
    ri = lax.broadcasted_iota(jnp.int32, (c, c), 0)
    ci = lax.broadcasted_iota(jnp.int32, (c, c), 1)
    masks = [(ri // (2 * size) == ci // (2 * size)) & ((ri // size) % 2 == 1) & ((ci // size) % 2 == 0)
             for size in GLA_SPLITS]
    masks.append((ri // GLA_FINE == ci // GLA_FINE) & (ri >= ci))
    return masks


def _gla_chunks(bc, qt, factors, uk, uv, s_s):
    tt = bc.shape[0]
    c = GLA_CHUNK
    masks = _gla_level_masks()
    o_rows = []
    for ch in range(tt // c):
        rows = slice(ch * c, (ch + 1) * c)
        bl = bc[ch * c + c - 1:ch * c + c, :]
        kd = uk[rows, :] * jnp.exp(bl - bc[rows, :])
        dec = jnp.exp(bl)
        o_heads = []
        for hd in range(GLA_HEADS):
            ks = slice(hd * GLA_DK, (hd + 1) * GLA_DK)
            qh = qt[rows, ks].astype(BF16)
            vh = uv[rows, hd * GLA_DV:(hd + 1) * GLA_DV].astype(BF16)
            att = jnp.zeros((c, c), F32)
            for (qf, kf), mask in zip(factors, masks):
                att = jnp.where(mask, _dot_nt(qf[rows, ks], kf[rows, ks]), att)
            s_old = s_s[hd]
            o_heads.append(_dot(att, vh) + _dot(qh, s_old))
            dm = _col_bcast(dec[:, ks])
            s_s[hd] = s_old * jnp.concatenate([dm, dm], axis=1) + _dot(kd[:, ks].T, vh)
        o_rows.append(jnp.concatenate(o_heads, axis=1))
    return jnp.concatenate(o_rows, axis=0)


def _prompt_mixer_kernel(x_ref, mod_ref, g1_ref, wt_ref, cw_ref, cb_ref, wg_ref, ba_ref, bx_ref, lam_ref,
                         pa_ref, wa2_ref, gba_ref, gng_ref, pb_ref, wo_ref,
                         x1_ref, conv_ref, lru_ref, st_ref, conv_s, hc_s, s_s, uvrg_s):
    t = pl.program_id(1)

    @pl.when(t == 0)
    def _():
        conv_s[...] = jnp.zeros_like(conv_s)
        hc_s[...] = jnp.zeros_like(hc_s)
        s_s[...] = jnp.zeros_like(s_s)

    x = x_ref[0]
    mod = mod_ref[0]
    hb16 = _norm_mod(x, g1_ref[...], _mod_part(mod, 1), _mod_part(mod, 0)).astype(BF16)
    to_blocks, from_blocks = _regroup_matrices(x.shape[0])
    u_qk = _dot_nt(hb16, wt_ref[W_Q:W_V, :])
    u_lr = _dot_nt(hb16, wt_ref[W_LR:W_LR + LANES, :])
    hb = jnp.dot(to_blocks, hb16, preferred_element_type=F32).astype(BF16)
    uk = u_qk[:, GLA_DK_TOTAL:]
    bc, qt, factors = _gla_prep(u_lr, u_qk[:, :GLA_DK_TOTAL], uk, wa2_ref, gba_ref[...])
    piece_rows = ([W_V + i * VRG_PIECE for i in range((W_LR - W_V) // VRG_PIECE)]
                  + [W_GA + i * VRG_PIECE for i in range((N_IN - W_GA) // VRG_PIECE)])
    per_pair = -(-len(piece_rows) // (LRU_BLOCKS // 2))
    ya_blocks = []
    for m in range(LRU_BLOCKS // 2):
        ux2 = _dot_nt(hb, wt_ref[W_X + m * LRU_PAIR:W_X + (m + 1) * LRU_PAIR, :])
        ug2 = _dot_nt(hb, wt_ref[W_G + m * LRU_PAIR:W_G + (m + 1) * LRU_PAIR, :])
        for i in range(m * per_pair, min((m + 1) * per_pair, len(piece_rows))):
            uvrg_s[:, i * VRG_PIECE:(i + 1) * VRG_PIECE] = _dot_nt(hb16, wt_ref[piece_rows[i]:piece_rows[i] + VRG_PIECE, :])
        for j in range(2):
            uan = (ux2[:, j * LRU_BW:(j + 1) * LRU_BW], ug2[:, j * LRU_BW:(j + 1) * LRU_BW])
            ya_blocks.append(_lru_lane_block(2 * m + j, uan, cw_ref, cb_ref, wg_ref, ba_ref, bx_ref, lam_ref,
                                             conv_s, hc_s))
    o = _gla_chunks(bc, qt, factors, uk, uvrg_s, s_s)
    ya = jnp.dot(from_blocks, jnp.concatenate(ya_blocks, axis=1), preferred_element_type=F32).astype(BF16)
    ya = jnp.dot(ya, pa_ref[...], preferred_element_type=F32)
    yb = _head_norm_gate(o, gng_ref[...], uvrg_s[:, GLA_DV_TOTAL:GLA_DV_TOTAL + D_MODEL])
    x1_ref[0] = _merge_out(x, _mod_part(mod, 2), uvrg_s[:, 2 * D_MODEL:3 * D_MODEL], uvrg_s[:, 3 * D_MODEL:],
                           ya, yb, pb_ref, wo_ref)

    @pl.when(t == pl.num_programs(1) - 1)
    def _():
        conv_ref[0] = conv_s[0:CONV_W - 1, :]
        lru_ref[0] = hc_s[0:1, :]
        st_ref[0] = s_s[...]


def _prompt_mixer(x, mod_p, w, l):
    b, t, _ = x.shape
    tt = PROMPT_TILE
    per_seq = t // tt
    seq_spec = pl.BlockSpec((1, tt, D_MODEL), lambda i, j: (i, j, 0))
    return pl.pallas_call(
        _prompt_mixer_kernel,
        grid=(b, per_seq),
        in_specs=[
            seq_spec,
            pl.BlockSpec((1, 1, N_MOD), lambda i, j: (i, 0, 0)),
            _wspec(l, (1, D_MODEL)),
            _wspec(l, (N_IN, D_MODEL)),
            _wspec(l, (CONV_W, D_RNN)),
            _wspec(l, (1, D_RNN)),
            _wspec(l, (LRU_BLOCKS, LRU_BW, 2 * LRU_BW)),
            _wspec(l, (1, D_RNN)),
            _wspec(l, (1, D_RNN)),
            _wspec(l, (1, D_RNN)),
            _wspec(l, (D_RNN, D_MODEL)),
            _wspec(l, (LANES, GLA_DK_TOTAL)),
            _wspec(l, (1, GLA_DK_TOTAL)),
            _wspec(l, (1, GLA_DV_TOTAL)),
            _wspec(l, (GLA_DV_TOTAL, D_MODEL)),
            _wspec(l, (D_MODEL, D_MODEL)),
        ],
        out_specs=[
            seq_spec,
            pl.BlockSpec((1, CONV_W - 1, D_RNN), lambda i, j: (i, 0, 0)),
            pl.BlockSpec((1, 1, D_RNN), lambda i, j: (i, 0, 0)),
            pl.BlockSpec((1, GLA_HEADS, GLA_DK, GLA_DV), lambda i, j: (i, 0, 0, 0)),
        ],
        out_shape=[
            jax.ShapeDtypeStruct((b, t, D_MODEL), F32),
            jax.ShapeDtypeStruct((b, CONV_W - 1, D_RNN), F32),
            jax.ShapeDtypeStruct((b, 1, D_RNN), F32),
            jax.ShapeDtypeStruct((b, GLA_HEADS, GLA_DK, GLA_DV), F32),
        ],
        scratch_shapes=[
            pltpu.VMEM((SUBLANES, D_RNN), F32),
            pltpu.VMEM((SUBLANES, D_RNN), F32),
            pltpu.VMEM((GLA_HEADS, GLA_DK, GLA_DV), F32),
            pltpu.VMEM((tt, 2 * GLA_DV_TOTAL + 2 * D_MODEL), F32),
        ],
        compiler_params=_cparams(("arbitrary", "arbitrary")),
        name="prompt_mixer",
    )(x, mod_p, w["g1"], w["win_t"], w["conv_w"], w["conv_b"], w["wgate"], w["lru_ba"], w["lru_bx"],
      w["lru_lambda"], w["proj_a"], w["wa2"], w["gla_ba"], w["gla_norm_g"], w["proj_b"], w["w_out"])


def _ffn_rows(x, shift, scale, gate, g2, w1_ref, w2_ref, fg, final_norm):
    h2 = _norm_mod(x, g2, scale, shift)
    hb = h2.astype(BF16)
    f1_half = _dot(hb, w1_ref[:, :D_FF])
    f2 = _dot(hb, w1_ref[:, D_FF:])
    x2 = x + gate * _dot(_silu_of_half(f1_half) * f2, w2_ref[...])
    if final_norm:
        x2 = x2 * _rms(x2) * fg
    return x2


def _ffn_kernel(x_ref, mod_ref, g2_ref, w1_ref, w2_ref, fg_ref, o_ref, *, final_norm):
    m = mod_ref[0]
    o_ref[...] = _ffn_rows(x_ref[...], _mod_part(m, 3), _mod_part(m, 4), _mod_part(m, 5), g2_ref[...], w1_ref, w2_ref,
                           fg_ref[...], final_norm)


def _ffn(x2d, mod3, rows_per_mod, tile, w, l, final_g, final_norm):
    mrows = x2d.shape[0]
    r = mod3.shape[1]
    per = rows_per_mod // tile
    row_spec = pl.BlockSpec((tile, D_MODEL), lambda i: (i, 0))
    return pl.pallas_call(
        functools.partial(_ffn_kernel, final_norm=final_norm),
        grid=(mrows // tile,),
        in_specs=[
            row_spec,
            pl.BlockSpec((1, r, N_MOD), lambda i: (i // per, 0, 0)),
            _wspec(l, (1, D_MODEL)),
            _wspec(l, (D_MODEL, 2 * D_FF)),
            _wspec(l, (D_FF, D_MODEL)),
            _full_spec((1, D_MODEL)),
        ],
        out_specs=row_spec,
        out_shape=jax.ShapeDtypeStruct((mrows, D_MODEL), F32),
        compiler_params=_cparams(("arbitrary",)),
        name="ffn_final" if final_norm else "ffn",
    )(x2d, mod3, w["g2"], w["ffn_w1"], w["ffn_w2"], final_g)


def _row_regroup(rows, inner, outer):
    ri = lax.broadcasted_iota(jnp.int32, (rows, rows), 0)
    ci = lax.broadcasted_iota(jnp.int32, (rows, rows), 1)
    return jnp.where(ci == (ri % inner) * outer + ri // inner, 1.0, 0.0).astype(BF16)


def _sample_in_kernel(x_ref, mod_ref, g1_ref, win_ref, cw_ref, cb_ref, wg_ref, ba_ref, bx_ref, lam_ref, pa_ref,
                      conv0_ref, h0_ref, ya_ref, conv_ref, lru_ref, u_ref, ug_ref):
    rows = x_ref.shape[0]
    ns = h0_ref.shape[0]
    nt = rows // ns
    m = mod_ref[...]
    tile = lambda v: jnp.concatenate([v] * nt, axis=0)
    h = _norm_mod(x_ref[...], g1_ref[...], tile(_mod_part(m, 1)), tile(_mod_part(m, 0))).astype(BF16)
    ux = _dot_nt(h, win_ref[W_X:W_G, :])
    ug = _dot_nt(h, win_ref[W_G:W_Q, :])
    xa = [conv0_ref[i] for i in range(CONV_W - 1)] + [ux[i * ns:(i + 1) * ns, :] for i in range(nt)]
    cw = cw_ref[...]
    xcs = []
    for ti in range(nt):
        acc = cb_ref[...]
        for i in range(CONV_W):
            acc = acc + xa[ti + i] * cw[i:i + 1, :]
        xcs.append(acc)
    xc = jnp.concatenate(xcs, axis=0)
    coeffs = [_lru_coeffs_block(xc[:, n * LRU_BW:(n + 1) * LRU_BW], wg_ref[n], ba_ref[:, n * LRU_BW:(n + 1) * LRU_BW],
                                bx_ref[:, n * LRU_BW:(n + 1) * LRU_BW], lam_ref[:, n * LRU_BW:(n + 1) * LRU_BW])
              for n in range(LRU_BLOCKS)]
    a = jnp.concatenate([c[0] for c in coeffs], axis=1)
    bb = jnp.concatenate([c[1] for c in coeffs], axis=1)
    hc = h0_ref[...]
    hs = []
    for ti in range(nt):
        sl = slice(ti * ns, (ti + 1) * ns)
        hc = a[sl, :] * hc + bb[sl, :]
        hs.append(hc)
    ya_ref[...] = _dot(jnp.concatenate(hs, axis=0) * _gelu_tanh(ug), pa_ref[...])
    for i in range(CONV_W - 1):
        conv_ref[i] = xa[nt + i]
    lru_ref[...] = hc
    ug_ref[...] = _dot_nt(h, win_ref[W_GA:N_IN, :])
    hs_major = jnp.dot(_row_regroup(rows, nt, ns), h, preferred_element_type=F32).astype(BF16)
    u_ref[...] = jnp.concatenate([_dot_nt(hs_major, win_ref[W_LR:W_LR + LANES, :]),
                                  _dot_nt(hs_major, win_ref[W_Q:W_LR, :])], axis=1)


def _sample_in(x_tm, mod_s, conv0_tm, h0_all, w, l):
    rows = x_tm.shape[0]
    ns = h0_all.shape[1]
    return pl.pallas_call(
        _sample_in_kernel,
        grid=(1,),
        in_specs=[
            _full_spec((rows, D_MODEL)),
            _full_spec((ns, N_MOD)),
            _wspec(l, (1, D_MODEL)),
            _wspec(l, (N_IN, D_MODEL)),
            _wspec(l, (CONV_W, D_RNN)),
            _wspec(l, (1, D_RNN)),
            _wspec(l, (LRU_BLOCKS, LRU_BW, 2 * LRU_BW)),
            _wspec(l, (1, D_RNN)),
            _wspec(l, (1, D_RNN)),
            _wspec(l, (1, D_RNN)),
            _wspec(l, (D_RNN, D_MODEL)),
            _full_spec((CONV_W - 1, ns, D_RNN)),
            _wspec(l, (ns, D_RNN)),
        ],
        out_specs=[
            _full_out((rows, D_MODEL)),
            _full_out((CONV_W - 1, ns, D_RNN)),
            _full_out((ns, D_RNN)),
            _full_out((rows, N_IN_B)),
            _full_out((rows, 2 * D_MODEL)),
        ],
        out_shape=[
            jax.ShapeDtypeStruct((rows, D_MODEL), F32),
            jax.ShapeDtypeStruct((CONV_W - 1, ns, D_RNN), F32),
            jax.ShapeDtypeStruct((ns, D_RNN), F32),
            jax.ShapeDtypeStruct((rows, N_IN_B), F32),
            jax.ShapeDtypeStruct((rows, 2 * D_MODEL), F32),
        ],
        compiler_params=_cparams(("arbitrary",)),
        name="sample_in",
    )(x_tm, mod_s, w["g1"], w["win_t"], w["conv_w"], w["conv_b"], w["wgate"], w["lru_ba"], w["lru_bx"],
      w["lru_lambda"], w["proj_a"], conv0_tm, h0_all)


def _sample_gla_kernel(u_ref, wa2_ref, gba_ref, gng_ref, s0_ref, *rest, nt, layer, carries_buffer):
    if carries_buffer:
        yb_ref, st_ref, o_s, q_s, v_s, kdt_s, bl_s = rest[1:]
    else:
        yb_ref, st_all_ref, o_s, q_s, v_s, kdt_s, bl_s = rest
        st_ref = st_all_ref.at[layer]
        for other in range(st_all_ref.shape[0]):
            if other != layer:
                st_all_ref[other] = jnp.zeros(st_all_ref.shape[1:], F32)
    rows = u_ref.shape[0]
    steps_per_rows = rows // (SAMPLE_STATE_BLOCK * nt)
    sub = pl.program_id(0) % steps_per_rows
    rg = SAMPLE_ROW_GROUP
    seq_per_group = rg // nt

    @pl.when(sub == 0)
    def _():
        u = u_ref[...]
        g = _gla_gate(u[:, _LRO:_LRO + LANES], wa2_ref, gba_ref[...])
        bc = _group_cumsum(g, nt)
        rin = lax.broadcasted_iota(jnp.int32, g.shape, 0) % nt
        sfx = jnp.zeros_like(g)
        for k in range(1, nt):
            sfx = sfx + jnp.where(rin + k < nt, pltpu.roll(g, rows - k, 0), 0.0)
        qt = u[:, _QO:_QO + GLA_DK_TOTAL] * (jnp.exp(bc) * (GLA_DK ** -0.5))
        uk = u[:, _KO:_KO + GLA_DK_TOTAL]
        kt = uk * jnp.exp(-bc)
        kd = uk * jnp.exp(sfx)
        q_s[...] = qt.astype(BF16)
        v_s[...] = u[:, _VO:_VO + GLA_DV_TOTAL].astype(BF16)
        bl_s[...] = bc + sfx
        ri = lax.broadcasted_iota(jnp.int32, (rows, rows), 0)
        ci = lax.broadcasted_iota(jnp.int32, (rows, rows), 1)
        same_seq_causal = (ri // nt == ci // nt) & (ri >= ci)
        for hd in range(GLA_HEADS):
            ks = slice(hd * GLA_DK, (hd + 1) * GLA_DK)
            att = jnp.where(same_seq_causal, _dot_nt(qt[:, ks], kt[:, ks]), 0.0)
            o_s[:, hd * GLA_DV:(hd + 1) * GLA_DV] = _dot(att, v_s[:, hd * GLA_DV:(hd + 1) * GLA_DV])
            kdt_s[hd] = kd[:, ks].T

    lane_seq = lax.broadcasted_iota(jnp.int32, (GLA_DK, rows), 1) // nt
    row_seq = lax.broadcasted_iota(jnp.int32, (rg, GLA_DV), 0) // nt

    def group(gi, carry):
        r0 = pl.multiple_of((sub * (SAMPLE_STATE_BLOCK // seq_per_group) + gi) * rg, rg)
        dec_g = jnp.exp(bl_s[pl.ds(r0, rg), :])
        for hd in range(GLA_HEADS):
            ks = slice(hd * GLA_DK, (hd + 1) * GLA_DK)
            vsl = slice(hd * GLA_DV, (hd + 1) * GLA_DV)
            qg = q_s[pl.ds(r0, rg), ks]
            vh = v_s[:, vsl]
            inter = jnp.zeros((rg, GLA_DV), F32)
            for j in range(seq_per_group):
                s_loc = gi * seq_per_group + j
                s_row = sub * SAMPLE_STATE_BLOCK + s_loc
                s_old = s0_ref[s_loc, hd]
                inter = jnp.where(row_seq == j, _dot(qg, s_old), inter)
                kdt = jnp.where(lane_seq == s_row, kdt_s[hd], 0.0)
                dm = _col_bcast(dec_g[j * nt:j * nt + 1, ks])
                st_ref[s_loc, hd] = s_old * jnp.concatenate([dm, dm], axis=1) + _dot(kdt, vh)
            o_s[pl.ds(r0, rg), vsl] = o_s[pl.ds(r0, rg), vsl] + inter
        return carry

    lax.fori_loop(0, SAMPLE_STATE_BLOCK // seq_per_group, group, 0)

    @pl.when(sub == steps_per_rows - 1)
    def _():
        yb_ref[...] = _head_norm_gate(o_s[...], gng_ref[...], u_ref[:, _RO:_RO + D_MODEL])


def _sample_gla(u_sm, state_all, new_state_all, w, l, nt):
    rows = u_sm.shape[0]
    br = SAMPLE_ROW_BLOCK
    sb = SAMPLE_STATE_BLOCK
    per = br // (sb * nt)
    state_spec = pl.BlockSpec((None, sb, GLA_HEADS, GLA_DK, GLA_DV), lambda i: (l, i, 0, 0, 0))
    out_state_spec = state_spec
    if new_state_all is None:
        out_state_spec = pl.BlockSpec((state_all.shape[0], sb, GLA_HEADS, GLA_DK, GLA_DV), lambda i: (0, i, 0, 0, 0))
    in_specs = [
        pl.BlockSpec((br, N_IN_B), lambda i: (i // per, 0)),
        _wspec(l, (LANES, GLA_DK_TOTAL)),
        _wspec(l, (1, GLA_DK_TOTAL)),
        _wspec(l, (1, GLA_DV_TOTAL)),
        state_spec,
    ]
    args = [u_sm, w["wa2"], w["gla_ba"], w["gla_norm_g"], state_all]
    aliases = {}
    if new_state_all is not None:
        in_specs.append(pl.BlockSpec(memory_space=pl.ANY))
        args.append(new_state_all)
        aliases = {len(args) - 1: 1}
    return pl.pallas_call(
        functools.partial(_sample_gla_kernel, nt=nt, layer=l, carries_buffer=new_state_all is not None),
        grid=(rows // (sb * nt),),
        in_specs=in_specs,
        out_specs=[
            pl.BlockSpec((br, D_MODEL), lambda i: (i // per, 0)),
            out_state_spec,
        ],
        out_shape=[
            jax.ShapeDtypeStruct((rows, D_MODEL), F32),
            jax.ShapeDtypeStruct(state_all.shape, F32),
        ],
        input_output_aliases=aliases,
        scratch_shapes=[
            pltpu.VMEM((br, GLA_DV_TOTAL), F32),
            pltpu.VMEM((br, GLA_DK_TOTAL), BF16),
            pltpu.VMEM((br, GLA_DV_TOTAL), BF16),
            pltpu.VMEM((GLA_HEADS, GLA_DK, br), F32),
            pltpu.VMEM((br, GLA_DK_TOTAL), F32),
        ],
        compiler_params=_cparams(("arbitrary",)),
        name="sample_gla",
    )(*args)


def _sample_out_kernel(x_ref, mod_ref, ug_ref, ya_ref, yb_ref, pb_ref, wo_ref, g2_ref, w1_ref, w2_ref, fg_ref, o_ref, *,
                       final_norm):
    rows = x_ref.shape[0]
    ns = mod_ref.shape[0]
    nt = rows // ns
    m = mod_ref[...]
    tile = lambda v: jnp.concatenate([v] * nt, axis=0)
    yb = jnp.dot(_row_regroup(rows, ns, nt), yb_ref[...].astype(BF16), preferred_element_type=F32).astype(BF16)
    x1 = _merge_out(x_ref[...], tile(_mod_part(m, 2)), ug_ref[:, :D_MODEL], ug_ref[:, D_MODEL:], ya_ref[...], yb,
                    pb_ref, wo_ref)
    o_ref[...] = _ffn_rows(x1, tile(_mod_part(m, 3)), tile(_mod_part(m, 4)), tile(_mod_part(m, 5)), g2_ref[...],
                           w1_ref, w2_ref, fg_ref[...], final_norm)


def _sample_out(x_tm, mod_s, ug_tm, ya, yb_sm, w, l, final_g, final_norm):
    rows = x_tm.shape[0]
    return pl.pallas_call(
        functools.partial(_sample_out_kernel, final_norm=final_norm),
        grid=(1,),
        in_specs=[
            _full_spec((rows, D_MODEL)),
            _full_spec(mod_s.shape),
            _full_spec((rows, 2 * D_MODEL)),
            _full_spec((rows, D_MODEL)),
            _full_spec((rows, D_MODEL)),
            _wspec(l, (GLA_DV_TOTAL, D_MODEL)),
            _wspec(l, (D_MODEL, D_MODEL)),
            _wspec(l, (1, D_MODEL)),
            _wspec(l, (D_MODEL, 2 * D_FF)),
            _wspec(l, (D_FF, D_MODEL)),
            _full_spec((1, D_MODEL)),
        ],
        out_specs=_full_out((rows, D_MODEL)),
        out_shape=jax.ShapeDtypeStruct((rows, D_MODEL), F32),
        compiler_params=_cparams(("arbitrary",)),
        name="sample_out_final" if final_norm else "sample_out",
    )(x_tm, mod_s, ug_tm, ya, yb_sm, w["proj_b"], w["w_out"], w["g2"], w["ffn_w1"], w["ffn_w2"], final_g)


def _stacked_weights(norm1_g, norm2_g, w_in, conv_w, conv_b, lru_wa, lru_ba, lru_wx, lru_bx, lru_lambda,
                     gla_wa2, gla_ba, gla_norm_g, proj_a, proj_b, w_out, ffn_w1, ffn_w2):
    depth = w_in.shape[0]
    row = lambda v: v.reshape(depth, 1, -1)
    rows_in = jnp.arange(N_IN)
    in_scale = jnp.where(((rows_in >= W_R) & (rows_in < W_LR)) | (rows_in >= W_GA), 0.5, 1.0).astype(w_in.dtype)
    ff_scale = jnp.where(jnp.arange(2 * D_FF) < D_FF, 0.5, 1.0).astype(ffn_w1.dtype)
    return {
        "g1": row(norm1_g), "g2": row(norm2_g),
        "win_t": (jnp.swapaxes(w_in, 1, 2) * in_scale[None, :, None]).astype(BF16),
        "conv_w": conv_w, "conv_b": row(conv_b),
        "wgate": (0.5 * jnp.concatenate([lru_wa, lru_wx], axis=-1)).astype(BF16),
        "lru_ba": row(0.5 * lru_ba), "lru_bx": row(0.5 * lru_bx), "lru_lambda": row(lru_lambda),
        "wa2": jnp.pad(gla_wa2, ((0, 0), (0, LANES - GLA_RANK), (0, 0))).astype(BF16),
        "gla_ba": row(gla_ba), "gla_norm_g": row(gla_norm_g),
        "proj_a": proj_a.astype(BF16), "proj_b": proj_b.astype(BF16), "w_out": w_out.astype(BF16),
        "ffn_w1": (ffn_w1 * ff_scale).astype(BF16), "ffn_w2": ffn_w2.astype(BF16),
    }


def kernel(x_prompt, x_sample, c_prompt, c_sample, state_conv, state_lru, state_gla, norm1_g, norm2_g, ada_w, ada_b,
           w_in, conv_w, conv_b, lru_wa, lru_ba, lru_wx, lru_bx, lru_lambda, gla_wa2, gla_ba, gla_norm_g, proj_a,
           proj_b, w_out, ffn_w1, ffn_w2, final_g):
    bp, tp, _ = x_prompt.shape
    bs, ts, _ = x_sample.shape
    depth = w_in.shape[0]
    mod = _modulation(jnp.concatenate([c_prompt, c_sample], axis=0), ada_w, ada_b)
    w = _stacked_weights(norm1_g, norm2_g, w_in, conv_w, conv_b, lru_wa, lru_ba, lru_wx, lru_bx, lru_lambda,
                         gla_wa2, gla_ba, gla_norm_g, proj_a, proj_b, w_out, ffn_w1, ffn_w2)
    fg = final_g.reshape(1, D_MODEL)
    xp = x_prompt
    xs = x_sample.transpose(1, 0, 2).reshape(ts * bs, D_MODEL)
    conv_p, lru_p, gla_p, conv_s, lru_s = [], [], [], [], []
    gla_s = None
    for l in range(depth):
        last = l == depth - 1
        mod_p = mod[l, :bp].reshape(bp, 1, N_MOD)
        mod_s = mod[l, bp:]
        x1, cb, ht, st = _prompt_mixer(xp, mod_p, w, l)
        xp = _ffn(x1.reshape(bp * tp, D_MODEL), mod_p, tp, FFN_TILE, w, l, fg, last).reshape(bp, tp, D_MODEL)
        conv_p.append(cb)
        lru_p.append(ht.reshape(bp, D_RNN))
        gla_p.append(st)
        ya_s, cb_s, ht_s, u_sm, ug_tm = _sample_in(xs, mod_s, state_conv[l].transpose(1, 0, 2), state_lru, w, l)
        yb_sm, gla_s = _sample_gla(u_sm, state_gla, gla_s, w, l, ts)
        xs = _sample_out(xs, mod_s, ug_tm, ya_s, yb_sm, w, l, fg, last)
        conv_s.append(cb_s.transpose(1, 0, 2))
        lru_s.append(ht_s)
    y_sample = xs.reshape(ts, bs, D_MODEL).transpose(1, 0, 2)
    return (xp, y_sample, jnp.stack(conv_p), jnp.stack(lru_p), jnp.stack(gla_p),
            jnp.stack(conv_s), jnp.stack(lru_s), gla_s)
```

```python
import functools

import jax
import jax.numpy as jnp
from jax import lax
from jax.experimental import pallas as pl
from jax.experimental.pallas import tpu as pltpu

F32 = jnp.float32
BF16 = jnp.bfloat16

D_MODEL = 1024
D_RNN = 1280
LRU_BW = 128
LRU_BLOCKS = D_RNN // LRU_BW
CONV_W = 4
LRU_C = 8.0
GLA_HEADS = 4
GLA_DK = 128
GLA_DV = 256
GLA_DK_TOTAL = GLA_HEADS * GLA_DK
GLA_DV_TOTAL = GLA_HEADS * GLA_DV
GLA_RANK = 16
GLA_TAU = 16.0
D_FF = 2816
EPS = 1e-6
TINY_F32 = 1.1754944e-38
N_MOD = 6 * D_MODEL

LANES = 128
SUBLANES = 8
VMEM_LIMIT_BYTES = 60 * 1024 * 1024

W_X, W_G, W_Q, W_K, W_V, W_R, W_LR, W_GA, W_GB = 0, 1280, 2560, 3072, 3584, 4608, 5632, 5648, 6672
N_IN = W_GB + D_MODEL
_LRO, _QO, _KO, _VO, _RO = 0, 128, 640, 1152, 2176
N_IN_B = _RO + D_MODEL

PROMPT_TILE = 512
GLA_CHUNK = 128
GLA_SPLITS = (64, 32)
GLA_FINE = 32
FFN_TILE = 512
VRG_PIECE = 512
LRU_PAIR = 2 * LRU_BW
SAMPLE_ROW_BLOCK = 128
SAMPLE_STATE_BLOCK = 8
SAMPLE_ROW_GROUP = 16


def _cparams(sem):
    return pltpu.CompilerParams(dimension_semantics=sem, vmem_limit_bytes=VMEM_LIMIT_BYTES)


def _full_spec(shape):
    n = len(shape)
    return pl.BlockSpec(shape, lambda *_: (0,) * n, pipeline_mode=pl.Buffered(1))


def _full_out(shape):
    n = len(shape)
    return pl.BlockSpec(shape, lambda *_: (0,) * n)


def _wspec(l, shape):
    n = len(shape)
    return pl.BlockSpec((None,) + tuple(shape), lambda *_: (l,) + (0,) * n, pipeline_mode=pl.Buffered(1))


def _softplus(y):
    return jnp.maximum(y, 0.0) + jnp.log1p(jnp.exp(-jnp.abs(y)))


def _sigmoid_of_half(t):
    return 0.5 * jnp.tanh(t) + 0.5


def _silu_of_half(t):
    return t + t * jnp.tanh(t)


def _silu(y):
    return _silu_of_half(0.5 * y)


def _sqrt_nonneg(s):
    return s * lax.rsqrt(jnp.maximum(s, TINY_F32))


def _gelu_tanh(y):
    return 0.5 * y * (1.0 + jnp.tanh(0.7978845608028654 * (y + 0.044715 * (y * y * y))))


def _rms(x):
    return lax.rsqrt(jnp.mean(x * x, axis=-1, keepdims=True) + EPS)


def _norm_mod(x, g, scale, shift):
    return (x * _rms(x)) * (g * (1.0 + scale)) + shift


def _mod_part(m, i):
    return m[:, i * D_MODEL:(i + 1) * D_MODEL]


def _dot(a, b):
    return jnp.dot(a.astype(BF16), b.astype(BF16), preferred_element_type=F32)


def _dot_nt(a, b):
    return lax.dot_general(a.astype(BF16), b.astype(BF16), (((1,), (1,)), ((), ())),
                           preferred_element_type=F32)


def _lru_coeffs_block(xb, wg, ba, bx, lam):
    lamc = -LRU_C * _softplus(-lam)
    pre = _dot(xb, wg)
    r = _sigmoid_of_half(pre[:, :LRU_BW] + ba)
    i = _sigmoid_of_half(pre[:, LRU_BW:] + bx)
    a = jnp.exp(lamc * r)
    return a, _sqrt_nonneg(1.0 - a * a) * (i * xb)


def _gla_gate(ulr, wa2_ref, gba):
    z = _dot(ulr, wa2_ref[...]) + gba
    return (jnp.minimum(z, 0.0) - jnp.log(1.0 + jnp.exp(-jnp.abs(z)))) * (1.0 / GLA_TAU)


def _group_cumsum(g, group):
    rin = lax.broadcasted_iota(jnp.int32, g.shape, 0) % group
    x = g
    k = 1
    while k < group:
        x = x + jnp.where(rin >= k, pltpu.roll(x, k, 0), 0.0)
        k *= 2
    return x


def _col_bcast(row):
    return jnp.broadcast_to(row, (LANES, LANES)).T


def _head_norm_gate(o, gng, ur_half):
    parts = []
    for hd in range(GLA_HEADS):
        sl = slice(hd * GLA_DV, (hd + 1) * GLA_DV)
        oh = o[:, sl]
        parts.append(oh * _rms(oh) * gng[:, sl])
    return jnp.concatenate(parts, axis=1) * _silu_of_half(ur_half)


def _merge_out(x, gt1, uga_half, ugb_half, ya, yb, pb_ref, wo_ref):
    pbv = _dot(yb, pb_ref[...])
    mm = _sigmoid_of_half(uga_half) * ya + _sigmoid_of_half(ugb_half) * pbv
    return x + gt1 * _dot(mm, wo_ref[...])


def _mod_kernel(c_ref, w_ref, b_ref, o_ref):
    o_ref[0] = _dot(_silu(c_ref[...]), w_ref[0]) + b_ref[0]


def _modulation(c_all, ada_w, ada_b):
    depth = ada_w.shape[0]
    rows = c_all.shape[0]
    nblk = N_MOD // D_MODEL
    return pl.pallas_call(
        _mod_kernel,
        grid=(depth, nblk),
        in_specs=[
            pl.BlockSpec((rows, D_MODEL), lambda l, j: (0, 0)),
            pl.BlockSpec((1, D_MODEL, D_MODEL), lambda l, j: (l, 0, j)),
            pl.BlockSpec((1, 1, D_MODEL), lambda l, j: (l, 0, j)),
        ],
        out_specs=pl.BlockSpec((1, rows, D_MODEL), lambda l, j: (l, 0, j)),
        out_shape=jax.ShapeDtypeStruct((depth, rows, N_MOD), F32),
        compiler_params=_cparams(("arbitrary", "arbitrary")),
        name="modulation",
    )(c_all, ada_w, ada_b.reshape(depth, 1, N_MOD))


def _shift_rows_in(blk, first_row):
    top = lax.broadcasted_iota(jnp.int32, blk.shape, 0) == 0
    return jnp.where(top, first_row, pltpu.roll(blk, 1, 0))


def _regroup_matrices(tt):
    ng = tt // SUBLANES
    ri = lax.broadcasted_iota(jnp.int32, (tt, tt), 0)
    ci = lax.broadcasted_iota(jnp.int32, (tt, tt), 1)
    to_blocks = jnp.where(ci == SUBLANES * (ri % ng) + ri // ng, 1.0, 0.0).astype(BF16)
    from_blocks = jnp.where(ci == ng * (ri % SUBLANES) + ri // SUBLANES, 1.0, 0.0).astype(BF16)
    return to_blocks, from_blocks


def _lru_conv(ux, cw, cb, conv_rows):
    ng = ux.shape[0] // SUBLANES
    first_tail = SUBLANES - (CONV_W - 1)
    xs = [ux[j * ng:(j + 1) * ng, :] for j in range(SUBLANES)]
    prev = {k: _shift_rows_in(xs[k], conv_rows[k - first_tail:k - first_tail + 1, :])
            for k in range(first_tail, SUBLANES)}
    xcs = []
    for j in range(SUBLANES):
        acc = cb
        for i in range(CONV_W):
            d = CONV_W - 1 - i
            src = xs[j - d] if j >= d else prev[j - d + SUBLANES]
            acc = acc + src * cw[i:i + 1, :]
        xcs.append(acc)
    tail = jnp.concatenate([xs[k][ng - 1:ng, :] for k in range(first_tail, SUBLANES)], axis=0)
    return jnp.concatenate(xcs, axis=0), tail


def _lru_scan(a, bb, carry):
    ng = a.shape[0] // SUBLANES
    loc = [bb[0:ng, :]]
    cum = [a[0:ng, :]]
    for j in range(1, SUBLANES):
        aj = a[j * ng:(j + 1) * ng, :]
        loc.append(aj * loc[-1] + bb[j * ng:(j + 1) * ng, :])
        cum.append(aj * cum[-1])
    ga, gb = cum[-1], loc[-1]
    rowg = lax.broadcasted_iota(jnp.int32, ga.shape, 0)
    k = 1
    while k < ng:
        ga_sh = jnp.where(rowg >= k, pltpu.roll(ga, k, 0), 1.0)
        gb_sh = jnp.where(rowg >= k, pltpu.roll(gb, k, 0), 0.0)
        gb = ga * gb_sh + gb
        ga = ga * ga_sh
        k *= 2
    leaving = gb + ga * carry
    entering = _shift_rows_in(leaving, carry)
    hs = jnp.concatenate([loc[j] + cum[j] * entering for j in range(SUBLANES)], axis=0)
    return hs, leaving[ng - 1:ng, :]


def _lru_lane_block(n, uan, cw_ref, cb_ref, wg_ref, ba_ref, bx_ref, lam_ref, conv_s, hc_s):
    sl = slice(n * LRU_BW, (n + 1) * LRU_BW)
    xc, tail = _lru_conv(uan[0], cw_ref[:, sl], cb_ref[:, sl], conv_s[0:CONV_W - 1, sl])
    conv_s[0:CONV_W - 1, sl] = tail
    a, bb = _lru_coeffs_block(xc, wg_ref[n], ba_ref[:, sl], bx_ref[:, sl], lam_ref[:, sl])
    hs, last = _lru_scan(a, bb, hc_s[0:1, sl])
    hc_s[0:1, sl] = last
    return (hs * _gelu_tanh(uan[1])).astype(BF16)


def _rows_per_block(x, size, pick):
    parts = []
    for b in range(x.shape[0] // size):
        r = pick(b)
        row = jnp.zeros((1, x.shape[1]), x.dtype) if r is None else x[r:r + 1, :]
        parts.append(jnp.broadcast_to(row, (size, x.shape[1])))
    return jnp.concatenate(parts, axis=0)


def _gla_prep(ulr, uq, uk, wa2_ref, gba):
    c = GLA_CHUNK
    g = _gla_gate(ulr, wa2_ref, gba)
    bc = _group_cumsum(g, c)
    qs = uq * (GLA_DK ** -0.5)
    factors = []
    for size in GLA_SPLITS:
        before = _rows_per_block(bc, size, lambda b: None if (b * size) % c == 0 else b * size - 1)
        last = _rows_per_block(bc, size, lambda b: b * size + size - 1)
        factors.append((qs * jnp.exp(bc - before), uk * jnp.exp(last - bc)))
    mid = _rows_per_block(bc, GLA_FINE, lambda b: b * GLA_FINE + GLA_FINE // 2 - 1)
    factors.append((qs * jnp.exp(bc - mid), uk * jnp.exp(mid - bc)))
    return bc, qs * jnp.exp(bc), factors


def _gla_level_masks():
    c = GLA_CHUNK
    ri = lax.broadcasted_iota(jnp.int32, (c, c), 0)
    ci = lax.broadcasted_iota(jnp.int32, (c, c), 1)
    masks = [(ri // (2 * size) == ci // (2 * size)) & ((ri // size) % 2 == 1) & ((ci // size) % 2 == 0)
             for size in GLA_SPLITS]
    masks.append((ri // GLA_FINE == ci // GLA_FINE) & (ri >= ci))
    spans = [[(b * size, (b + 1) * size) for b in range(1, c // size, 2)] for size in GLA_SPLITS] + [[(0, c)]]
    return masks, spans


def _rows_at(parts, spans, total):
    out, pos, off = [], 0, 0
    for a, b in spans:
        if a > pos:
            out.append(jnp.zeros((a - pos, parts.shape[1]), parts.dtype))
        out.append(parts[off:off + b - a, :])
        off, pos = off + b - a, b
    if pos < total:
        out.append(jnp.zeros((total - pos, parts.shape[1]), parts.dtype))
    return jnp.concatenate(out, axis=0)


def _gla_chunks(bc, qt, factors, uk, uv, s_s):
    tt = bc.shape[0]
    c = GLA_CHUNK
    masks, spans = _gla_level_masks()
    o_rows = []
    for ch in range(tt // c):
        rows = slice(ch * c, (ch + 1) * c)
        bl = bc[ch * c + c - 1:ch * c + c, :]
        kd = uk[rows, :] * jnp.exp(bl - bc[rows, :])
        dec = jnp.exp(bl)
        o_heads = []
        for hd in range(GLA_HEADS):
            ks = slice(hd * GLA_DK, (hd + 1) * GLA_DK)
            qh = qt[rows, ks].astype(BF16)
            vh = uv[rows, hd * GLA_DV:(hd + 1) * GLA_DV].astype(BF16)
            att = jnp.zeros((c, c), F32)
            for (qf, kf), mask, sp in zip(factors, masks, spans):
                qsel = jnp.concatenate([qf[ch * c + a:ch * c + b, ks] for a, b in sp], axis=0)
                att = jnp.where(mask, _rows_at(_dot_nt(qsel, kf[rows, ks]), sp, c), att)
            s_old = s_s[hd]
            o_heads.append(_dot(att, vh) + _dot(qh, s_old))
            dm = _col_bcast(dec[:, ks])
            s_s[hd] = s_old * jnp.concatenate([dm, dm], axis=1) + _dot(kd[:, ks].T, vh)
        o_rows.append(jnp.concatenate(o_heads, axis=1))
    return jnp.concatenate(o_rows, axis=0)


def _prompt_mixer_kernel(x_ref, mod_ref, g1_ref, wt_ref, cw_ref, cb_ref, wg_ref, ba_ref, bx_ref, lam_ref,
                         pa_ref, wa2_ref, gba_ref, gng_ref, pb_ref, wo_ref,
                         x1_ref, conv_ref, lru_ref, st_ref, conv_s, hc_s, s_s, uvrg_s):
    t = pl.program_id(1)

    @pl.when(t == 0)
    def _():
        conv_s[...] = jnp.zeros_like(conv_s)
        hc_s[...] = jnp.zeros_like(hc_s)
        s_s[...] = jnp.zeros_like(s_s)

    x = x_ref[0]
    mod = mod_ref[0]
    hb16 = _norm_mod(x, g1_ref[...], _mod_part(mod, 1), _mod_part(mod, 0)).astype(BF16)
    to_blocks, from_blocks = _regroup_matrices(x.shape[0])
    u_qk = _dot_nt(hb16, wt_ref[W_Q:W_V, :])
    u_lr = _dot_nt(hb16, wt_ref[W_LR:W_LR + LANES, :])
    hb = jnp.dot(to_blocks, hb16, preferred_element_type=F32).astype(BF16)
    uk = u_qk[:, GLA_DK_TOTAL:]
    bc, qt, factors = _gla_prep(u_lr, u_qk[:, :GLA_DK_TOTAL], uk, wa2_ref, gba_ref[...])
    piece_rows = ([W_V + i * VRG_PIECE for i in range((W_LR - W_V) // VRG_PIECE)]
                  + [W_GA + i * VRG_PIECE for i in range((N_IN - W_GA) // VRG_PIECE)])
    per_pair = -(-len(piece_rows) // (LRU_BLOCKS // 2))
    ya_blocks = []
    for m in range(LRU_BLOCKS // 2):
        ux2 = _dot_nt(hb, wt_ref[W_X + m * LRU_PAIR:W_X + (m + 1) * LRU_PAIR, :])
        ug2 = _dot_nt(hb, wt_ref[W_G + m * LRU_PAIR:W_G + (m + 1) * LRU_PAIR, :])
        for i in range(m * per_pair, min((m + 1) * per_pair, len(piece_rows))):
            uvrg_s[:, i * VRG_PIECE:(i + 1) * VRG_PIECE] = _dot_nt(hb16, wt_ref[piece_rows[i]:piece_rows[i] + VRG_PIECE, :])
        for j in range(2):
            uan = (ux2[:, j * LRU_BW:(j + 1) * LRU_BW], ug2[:, j * LRU_BW:(j + 1) * LRU_BW])
            ya_blocks.append(_lru_lane_block(2 * m + j, uan, cw_ref, cb_ref, wg_ref, ba_ref, bx_ref, lam_ref,
                                             conv_s, hc_s))
    o = _gla_chunks(bc, qt, factors, uk, uvrg_s, s_s)
    ya = jnp.dot(from_blocks, jnp.concatenate(ya_blocks, axis=1), preferred_element_type=F32).astype(BF16)
    ya = jnp.dot(ya, pa_ref[...], preferred_element_type=F32)
    yb = _head_norm_gate(o, gng_ref[...], uvrg_s[:, GLA_DV_TOTAL:GLA_DV_TOTAL + D_MODEL])
    x1_ref[0] = _merge_out(x, _mod_part(mod, 2), uvrg_s[:, 2 * D_MODEL:3 * D_MODEL], uvrg_s[:, 3 * D_MODEL:],
                           ya, yb, pb_ref, wo_ref)

    @pl.when(t == pl.num_programs(1) - 1)
    def _():
        conv_ref[0] = conv_s[0:CONV_W - 1, :]
        lru_ref[0] = hc_s[0:1, :]
        st_ref[0] = s_s[...]


def _prompt_mixer(x, mod_p, w, l):
    b, t, _ = x.shape
    tt = PROMPT_TILE
    per_seq = t // tt
    seq_spec = pl.BlockSpec((1, tt, D_MODEL), lambda i, j: (i, j, 0))
    return pl.pallas_call(
        _prompt_mixer_kernel,
        grid=(b, per_seq),
        in_specs=[
            seq_spec,
            pl.BlockSpec((1, 1, N_MOD), lambda i, j: (i, 0, 0)),
            _wspec(l, (1, D_MODEL)),
            _wspec(l, (N_IN, D_MODEL)),
            _wspec(l, (CONV_W, D_RNN)),
            _wspec(l, (1, D_RNN)),
            _wspec(l, (LRU_BLOCKS, LRU_BW, 2 * LRU_BW)),
            _wspec(l, (1, D_RNN)),
            _wspec(l, (1, D_RNN)),
            _wspec(l, (1, D_RNN)),
            _wspec(l, (D_RNN, D_MODEL)),
            _wspec(l, (LANES, GLA_DK_TOTAL)),
            _wspec(l, (1, GLA_DK_TOTAL)),
            _wspec(l, (1, GLA_DV_TOTAL)),
            _wspec(l, (GLA_DV_TOTAL, D_MODEL)),
            _wspec(l, (D_MODEL, D_MODEL)),
        ],
        out_specs=[
            seq_spec,
            pl.BlockSpec((1, CONV_W - 1, D_RNN), lambda i, j: (i, 0, 0)),
            pl.BlockSpec((1, 1, D_RNN), lambda i, j: (i, 0, 0)),
            pl.BlockSpec((1, GLA_HEADS, GLA_DK, GLA_DV), lambda i, j: (i, 0, 0, 0)),
        ],
        out_shape=[
            jax.ShapeDtypeStruct((b, t, D_MODEL), F32),
            jax.ShapeDtypeStruct((b, CONV_W - 1, D_RNN), F32),
            jax.ShapeDtypeStruct((b, 1, D_RNN), F32),
            jax.ShapeDtypeStruct((b, GLA_HEADS, GLA_DK, GLA_DV), F32),
        ],
        scratch_shapes=[
            pltpu.VMEM((SUBLANES, D_RNN), F32),
            pltpu.VMEM((SUBLANES, D_RNN), F32),
            pltpu.VMEM((GLA_HEADS, GLA_DK, GLA_DV), F32),
            pltpu.VMEM((tt, 2 * GLA_DV_TOTAL + 2 * D_MODEL), F32),
        ],
        compiler_params=_cparams(("arbitrary", "arbitrary")),
        name="prompt_mixer",
    )(x, mod_p, w["g1"], w["win_t"], w["conv_w"], w["conv_b"], w["wgate"], w["lru_ba"], w["lru_bx"],
      w["lru_lambda"], w["proj_a"], w["wa2"], w["gla_ba"], w["gla_norm_g"], w["proj_b"], w["w_out"])


def _ffn_rows(x, shift, scale, gate, g2, w1_ref, w2_ref, fg, final_norm):
    h2 = _norm_mod(x, g2, scale, shift)
    hb = h2.astype(BF16)
    f1_half = _dot(hb, w1_ref[:, :D_FF])
    f2 = _dot(hb, w1_ref[:, D_FF:])
    x2 = x + gate * _dot(_silu_of_half(f1_half) * f2, w2_ref[...])
    if final_norm:
        x2 = x2 * _rms(x2) * fg
    return x2


def _ffn_kernel(x_ref, mod_ref, g2_ref, w1_ref, w2_ref, fg_ref, o_ref, *, final_norm):
    m = mod_ref[0]
    o_ref[...] = _ffn_rows(x_ref[...], _mod_part(m, 3), _mod_part(m, 4), _mod_part(m, 5), g2_ref[...], w1_ref, w2_ref,
                           fg_ref[...], final_norm)


def _ffn(x2d, mod3, rows_per_mod, tile, w, l, final_g, final_norm):
    mrows = x2d.shape[0]
    r = mod3.shape[1]
    per = rows_per_mod // tile
    row_spec = pl.BlockSpec((tile, D_MODEL), lambda i: (i, 0))
    return pl.pallas_call(
        functools.partial(_ffn_kernel, final_norm=final_norm),
        grid=(mrows // tile,),
        in_specs=[
            row_spec,
            pl.BlockSpec((1, r, N_MOD), lambda i: (i // per, 0, 0)),
            _wspec(l, (1, D_MODEL)),
            _wspec(l, (D_MODEL, 2 * D_FF)),
            _wspec(l, (D_FF, D_MODEL)),
            _full_spec((1, D_MODEL)),
        ],
        out_specs=row_spec,
        out_shape=jax.ShapeDtypeStruct((mrows, D_MODEL), F32),
        compiler_params=_cparams(("arbitrary",)),
        name="ffn_final" if final_norm else "ffn",
    )(x2d, mod3, w["g2"], w["ffn_w1"], w["ffn_w2"], final_g)


def _row_regroup(rows, inner, outer):
    ri = lax.broadcasted_iota(jnp.int32, (rows, rows), 0)
    ci = lax.broadcasted_iota(jnp.int32, (rows, rows), 1)
    return jnp.where(ci == (ri % inner) * outer + ri // inner, 1.0, 0.0).astype(BF16)


def _sample_in_kernel(x_ref, mod_ref, g1_ref, win_ref, cw_ref, cb_ref, wg_ref, ba_ref, bx_ref, lam_ref, pa_ref,
                      conv0_ref, h0_ref, ya_ref, conv_ref, lru_ref, u_ref, ug_ref):
    rows = x_ref.shape[0]
    ns = h0_ref.shape[0]
    nt = rows // ns
    m = mod_ref[...]
    tile = lambda v: jnp.concatenate([v] * nt, axis=0)
    h = _norm_mod(x_ref[...], g1_ref[...], tile(_mod_part(m, 1)), tile(_mod_part(m, 0))).astype(BF16)
    ux = _dot_nt(h, win_ref[W_X:W_G, :])
    ug = _dot_nt(h, win_ref[W_G:W_Q, :])
    xa = [conv0_ref[i] for i in range(CONV_W - 1)] + [ux[i * ns:(i + 1) * ns, :] for i in range(nt)]
    cw = cw_ref[...]
    xcs = []
    for ti in range(nt):
        acc = cb_ref[...]
        for i in range(CONV_W):
            acc = acc + xa[ti + i] * cw[i:i + 1, :]
        xcs.append(acc)
    xc = jnp.concatenate(xcs, axis=0)
    coeffs = [_lru_coeffs_block(xc[:, n * LRU_BW:(n + 1) * LRU_BW], wg_ref[n], ba_ref[:, n * LRU_BW:(n + 1) * LRU_BW],
                                bx_ref[:, n * LRU_BW:(n + 1) * LRU_BW], lam_ref[:, n * LRU_BW:(n + 1) * LRU_BW])
              for n in range(LRU_BLOCKS)]
    a = jnp.concatenate([c[0] for c in coeffs], axis=1)
    bb = jnp.concatenate([c[1] for c in coeffs], axis=1)
    hc = h0_ref[...]
    hs = []
    for ti in range(nt):
        sl = slice(ti * ns, (ti + 1) * ns)
        hc = a[sl, :] * hc + bb[sl, :]
        hs.append(hc)
    ya_ref[...] = _dot(jnp.concatenate(hs, axis=0) * _gelu_tanh(ug), pa_ref[...])
    for i in range(CONV_W - 1):
        conv_ref[i] = xa[nt + i]
    lru_ref[...] = hc
    ug_ref[...] = _dot_nt(h, win_ref[W_GA:N_IN, :])
    hs_major = jnp.dot(_row_regroup(rows, nt, ns), h, preferred_element_type=F32).astype(BF16)
    u_ref[...] = jnp.concatenate([_dot_nt(hs_major, win_ref[W_LR:W_LR + LANES, :]),
                                  _dot_nt(hs_major, win_ref[W_Q:W_LR, :])], axis=1)


def _sample_in(x_tm, mod_s, conv0_tm, h0_all, w, l):
    rows = x_tm.shape[0]
    ns = h0_all.shape[1]
    return pl.pallas_call(
        _sample_in_kernel,
        grid=(1,),
        in_specs=[
            _full_spec((rows, D_MODEL)),
            _full_spec((ns, N_MOD)),
            _wspec(l, (1, D_MODEL)),
            _wspec(l, (N_IN, D_MODEL)),
            _wspec(l, (CONV_W, D_RNN)),
            _wspec(l, (1, D_RNN)),
            _wspec(l, (LRU_BLOCKS, LRU_BW, 2 * LRU_BW)),
            _wspec(l, (1, D_RNN)),
            _wspec(l, (1, D_RNN)),
            _wspec(l, (1, D_RNN)),
            _wspec(l, (D_RNN, D_MODEL)),
            _full_spec((CONV_W - 1, ns, D_RNN)),
            _wspec(l, (ns, D_RNN)),
        ],
        out_specs=[
            _full_out((rows, D_MODEL)),
            _full_out((CONV_W - 1, ns, D_RNN)),
            _full_out((ns, D_RNN)),
            _full_out((rows, N_IN_B)),
            _full_out((rows, 2 * D_MODEL)),
        ],
        out_shape=[
            jax.ShapeDtypeStruct((rows, D_MODEL), F32),
            jax.ShapeDtypeStruct((CONV_W - 1, ns, D_RNN), F32),
            jax.ShapeDtypeStruct((ns, D_RNN), F32),
            jax.ShapeDtypeStruct((rows, N_IN_B), F32),
            jax.ShapeDtypeStruct((rows, 2 * D_MODEL), F32),
        ],
        compiler_params=_cparams(("arbitrary",)),
        name="sample_in",
    )(x_tm, mod_s, w["g1"], w["win_t"], w["conv_w"], w["conv_b"], w["wgate"], w["lru_ba"], w["lru_bx"],
      w["lru_lambda"], w["proj_a"], conv0_tm, h0_all)


def _sample_gla_kernel(u_ref, wa2_ref, gba_ref, gng_ref, s0_ref, *rest, nt, layer, carries_buffer):
    if carries_buffer:
        yb_ref, st_ref, o_s, q_s, v_s, kdt_s, bl_s = rest[1:]
    else:
        yb_ref, st_all_ref, o_s, q_s, v_s, kdt_s, bl_s = rest
        st_ref = st_all_ref.at[layer]
        for other in range(st_all_ref.shape[0]):
            if other != layer:
                st_all_ref[other] = jnp.zeros(st_all_ref.shape[1:], F32)
    rows = u_ref.shape[0]
    steps_per_rows = rows // (SAMPLE_STATE_BLOCK * nt)
    sub = pl.program_id(0) % steps_per_rows
    rg = SAMPLE_ROW_GROUP
    seq_per_group = rg // nt

    @pl.when(sub == 0)
    def _():
        u = u_ref[...]
        g = _gla_gate(u[:, _LRO:_LRO + LANES], wa2_ref, gba_ref[...])
        bc = _group_cumsum(g, nt)
        rin = lax.broadcasted_iota(jnp.int32, g.shape, 0) % nt
        sfx = jnp.zeros_like(g)
        for k in range(1, nt):
            sfx = sfx + jnp.where(rin + k < nt, pltpu.roll(g, rows - k, 0), 0.0)
        qt = u[:, _QO:_QO + GLA_DK_TOTAL] * (jnp.exp(bc) * (GLA_DK ** -0.5))
        uk = u[:, _KO:_KO + GLA_DK_TOTAL]
        kt = uk * jnp.exp(-bc)
        kd = uk * jnp.exp(sfx)
        q_s[...] = qt.astype(BF16)
        v_s[...] = u[:, _VO:_VO + GLA_DV_TOTAL].astype(BF16)
        bl_s[...] = bc + sfx
        ri = lax.broadcasted_iota(jnp.int32, (rows, rows), 0)
        ci = lax.broadcasted_iota(jnp.int32, (rows, rows), 1)
        same_seq_causal = (ri // nt == ci // nt) & (ri >= ci)
        for hd in range(GLA_HEADS):
            ks = slice(hd * GLA_DK, (hd + 1) * GLA_DK)
            att = jnp.where(same_seq_causal, _dot_nt(qt[:, ks], kt[:, ks]), 0.0)
            o_s[:, hd * GLA_DV:(hd + 1) * GLA_DV] = _dot(att, v_s[:, hd * GLA_DV:(hd + 1) * GLA_DV])
            kdt_s[hd] = kd[:, ks].T

    lane_seq = lax.broadcasted_iota(jnp.int32, (GLA_DK, rows), 1) // nt
    row_seq = lax.broadcasted_iota(jnp.int32, (rg, GLA_DV), 0) // nt

    def group(gi, carry):
        r0 = pl.multiple_of((sub * (SAMPLE_STATE_BLOCK // seq_per_group) + gi) * rg, rg)
        dec_g = jnp.exp(bl_s[pl.ds(r0, rg), :])
        for hd in range(GLA_HEADS):
            ks = slice(hd * GLA_DK, (hd + 1) * GLA_DK)
            vsl = slice(hd * GLA_DV, (hd + 1) * GLA_DV)
            qg = q_s[pl.ds(r0, rg), ks]
            vh = v_s[:, vsl]
            inter = jnp.zeros((rg, GLA_DV), F32)
            for j in range(seq_per_group):
                s_loc = gi * seq_per_group + j
                s_row = sub * SAMPLE_STATE_BLOCK + s_loc
                s_old = s0_ref[s_loc, hd]
                inter = jnp.where(row_seq == j, _dot(qg, s_old), inter)
                kdt = jnp.where(lane_seq == s_row, kdt_s[hd], 0.0)
                dm = _col_bcast(dec_g[j * nt:j * nt + 1, ks])
                st_ref[s_loc, hd] = s_old * jnp.concatenate([dm, dm], axis=1) + _dot(kdt, vh)
            o_s[pl.ds(r0, rg), vsl] = o_s[pl.ds(r0, rg), vsl] + inter
        return carry

    lax.fori_loop(0, SAMPLE_STATE_BLOCK // seq_per_group, group, 0)

    @pl.when(sub == steps_per_rows - 1)
    def _():
        yb_ref[...] = _head_norm_gate(o_s[...], gng_ref[...], u_ref[:, _RO:_RO + D_MODEL])


def _sample_gla(u_sm, state_all, new_state_all, w, l, nt):
    rows = u_sm.shape[0]
    br = SAMPLE_ROW_BLOCK
    sb = SAMPLE_STATE_BLOCK
    per = br // (sb * nt)
    state_spec = pl.BlockSpec((None, sb, GLA_HEADS, GLA_DK, GLA_DV), lambda i: (l, i, 0, 0, 0))
    out_state_spec = state_spec
    if new_state_all is None:
        out_state_spec = pl.BlockSpec((state_all.shape[0], sb, GLA_HEADS, GLA_DK, GLA_DV), lambda i: (0, i, 0, 0, 0))
    in_specs = [
        pl.BlockSpec((br, N_IN_B), lambda i: (i // per, 0)),
        _wspec(l, (LANES, GLA_DK_TOTAL)),
        _wspec(l, (1, GLA_DK_TOTAL)),
        _wspec(l, (1, GLA_DV_TOTAL)),
        state_spec,
    ]
    args = [u_sm, w["wa2"], w["gla_ba"], w["gla_norm_g"], state_all]
    aliases = {}
    if new_state_all is not None:
        in_specs.append(pl.BlockSpec(memory_space=pl.ANY))
        args.append(new_state_all)
        aliases = {len(args) - 1: 1}
    return pl.pallas_call(
        functools.partial(_sample_gla_kernel, nt=nt, layer=l, carries_buffer=new_state_all is not None),
        grid=(rows // (sb * nt),),
        in_specs=in_specs,
        out_specs=[
            pl.BlockSpec((br, D_MODEL), lambda i: (i // per, 0)),
            out_state_spec,
        ],
        out_shape=[
            jax.ShapeDtypeStruct((rows, D_MODEL), F32),
            jax.ShapeDtypeStruct(state_all.shape, F32),
        ],
        input_output_aliases=aliases,
        scratch_shapes=[
            pltpu.VMEM((br, GLA_DV_TOTAL), F32),
            pltpu.VMEM((br, GLA_DK_TOTAL), BF16),
            pltpu.VMEM((br, GLA_DV_TOTAL), BF16),
            pltpu.VMEM((GLA_HEADS, GLA_DK, br), F32),
            pltpu.VMEM((br, GLA_DK_TOTAL), F32),
        ],
        compiler_params=_cparams(("arbitrary",)),
        name="sample_gla",
    )(*args)


def _sample_out_kernel(x_ref, mod_ref, ug_ref, ya_ref, yb_ref, pb_ref, wo_ref, g2_ref, w1_ref, w2_ref, fg_ref, o_ref, *,
                       final_norm):
    rows = x_ref.shape[0]
    ns = mod_ref.shape[0]
    nt = rows // ns
    m = mod_ref[...]
    tile = lambda v: jnp.concatenate([v] * nt, axis=0)
    yb = jnp.dot(_row_regroup(rows, ns, nt), yb_ref[...].astype(BF16), preferred_element_type=F32).astype(BF16)
    x1 = _merge_out(x_ref[...], tile(_mod_part(m, 2)), ug_ref[:, :D_MODEL], ug_ref[:, D_MODEL:], ya_ref[...], yb,
                    pb_ref, wo_ref)
    o_ref[...] = _ffn_rows(x1, tile(_mod_part(m, 3)), tile(_mod_part(m, 4)), tile(_mod_part(m, 5)), g2_ref[...],
                           w1_ref, w2_ref, fg_ref[...], final_norm)


def _sample_out(x_tm, mod_s, ug_tm, ya, yb_sm, w, l, final_g, final_norm):
    rows = x_tm.shape[0]
    return pl.pallas_call(
        functools.partial(_sample_out_kernel, final_norm=final_norm),
        grid=(1,),
        in_specs=[
            _full_spec((rows, D_MODEL)),
            _full_spec(mod_s.shape),
            _full_spec((rows, 2 * D_MODEL)),
            _full_spec((rows, D_MODEL)),
            _full_spec((rows, D_MODEL)),
            _wspec(l, (GLA_DV_TOTAL, D_MODEL)),
            _wspec(l, (D_MODEL, D_MODEL)),
            _wspec(l, (1, D_MODEL)),
            _wspec(l, (D_MODEL, 2 * D_FF)),
            _wspec(l, (D_FF, D_MODEL)),
            _full_spec((1, D_MODEL)),
        ],
        out_specs=_full_out((rows, D_MODEL)),
        out_shape=jax.ShapeDtypeStruct((rows, D_MODEL), F32),
        compiler_params=_cparams(("arbitrary",)),
        name="sample_out_final" if final_norm else "sample_out",
    )(x_tm, mod_s, ug_tm, ya, yb_sm, w["proj_b"], w["w_out"], w["g2"], w["ffn_w1"], w["ffn_w2"], final_g)


def _stacked_weights(norm1_g, norm2_g, w_in, conv_w, conv_b, lru_wa, lru_ba, lru_wx, lru_bx, lru_lambda,
                     gla_wa2, gla_ba, gla_norm_g, proj_a, proj_b, w_out, ffn_w1, ffn_w2):
    depth = w_in.shape[0]
    row = lambda v: v.reshape(depth, 1, -1)
    rows_in = jnp.arange(N_IN)
    in_scale = jnp.where(((rows_in >= W_R) & (rows_in < W_LR)) | (rows_in >= W_GA), 0.5, 1.0).astype(w_in.dtype)
    ff_scale = jnp.where(jnp.arange(2 * D_FF) < D_FF, 0.5, 1.0).astype(ffn_w1.dtype)
    return {
        "g1": row(norm1_g), "g2": row(norm2_g),
        "win_t": (jnp.swapaxes(w_in, 1, 2) * in_scale[None, :, None]).astype(BF16),
        "conv_w": conv_w, "conv_b": row(conv_b),
        "wgate": (0.5 * jnp.concatenate([lru_wa, lru_wx], axis=-1)).astype(BF16),
        "lru_ba": row(0.5 * lru_ba), "lru_bx": row(0.5 * lru_bx), "lru_lambda": row(lru_lambda),
        "wa2": jnp.pad(gla_wa2, ((0, 0), (0, LANES - GLA_RANK), (0, 0))).astype(BF16),
        "gla_ba": row(gla_ba), "gla_norm_g": row(gla_norm_g),
        "proj_a": proj_a.astype(BF16), "proj_b": proj_b.astype(BF16), "w_out": w_out.astype(BF16),
        "ffn_w1": (ffn_w1 * ff_scale).astype(BF16), "ffn_w2": ffn_w2.astype(BF16),
    }


def kernel(x_prompt, x_sample, c_prompt, c_sample, state_conv, state_lru, state_gla, norm1_g, norm2_g, ada_w, ada_b,
           w_in, conv_w, conv_b, lru_wa, lru_ba, lru_wx, lru_bx, lru_lambda, gla_wa2, gla_ba, gla_norm_g, proj_a,
           proj_b, w_out, ffn_w1, ffn_w2, final_g):
    bp, tp, _ = x_prompt.shape
    bs, ts, _ = x_sample.shape
    depth = w_in.shape[0]
    mod = _modulation(jnp.concatenate([c_prompt, c_sample], axis=0), ada_w, ada_b)
    w = _stacked_weights(norm1_g, norm2_g, w_in, conv_w, conv_b, lru_wa, lru_ba, lru_wx, lru_bx, lru_lambda,
                         gla_wa2, gla_ba, gla_norm_g, proj_a, proj_b, w_out, ffn_w1, ffn_w2)
    fg = final_g.reshape(1, D_MODEL)
    xp = x_prompt
    xs = x_sample.transpose(1, 0, 2).reshape(ts * bs, D_MODEL)
    conv_p, lru_p, gla_p, conv_s, lru_s = [], [], [], [], []
    gla_s = None
    for l in range(depth):
        last = l == depth - 1
        mod_p = mod[l, :bp].reshape(bp, 1, N_MOD)
        mod_s = mod[l, bp:]
        x1, cb, ht, st = _prompt_mixer(xp, mod_p, w, l)
        xp = _ffn(x1.reshape(bp * tp, D_MODEL), mod_p, tp, FFN_TILE, w, l, fg, last).reshape(bp, tp, D_MODEL)
        conv_p.append(cb)
        lru_p.append(ht.reshape(bp, D_RNN))
        gla_p.append(st)
        ya_s, cb_s, ht_s, u_sm, ug_tm = _sample_in(xs, mod_s, state_conv[l].transpose(1, 0, 2), state_lru, w, l)
        yb_sm, gla_s = _sample_gla(u_sm, state_gla, gla_s, w, l, ts)
        xs = _sample_out(xs, mod_s, ug_tm, ya_s, yb_sm, w, l, fg, last)
        conv_s.append(cb_s.transpose(1, 0, 2))
        lru_s.append(ht_s)
    y_sample = xs.reshape(ts, bs, D_MODEL).transpose(1, 0, 2)
    return (xp, y_sample, jnp.stack(conv_p), jnp.stack(lru_p), jnp.stack(gla_p),
            jnp.stack(conv_s), jnp.stack(lru_s), gla_s)
```

```python
import functools

import jax
import jax.numpy as jnp
from jax import lax
from jax.experimental import pallas as pl
from jax.experimental.pallas import tpu as pltpu

F32 = jnp.float32
BF16 = jnp.bfloat16

D_MODEL = 1024
D_RNN = 1280
LRU_BW = 128
LRU_BLOCKS = D_RNN // LRU_BW
CONV_W = 4
LRU_C = 8.0
GLA_HEADS = 4
GLA_DK = 128
GLA_DV = 256
GLA_DK_TOTAL = GLA_HEADS * GLA_DK
GLA_DV_TOTAL = GLA_HEADS * GLA_DV
GLA_RANK = 16
GLA_TAU = 16.0
D_FF = 2816
EPS = 1e-6
TINY_F32 = 1.1754944e-38
N_MOD = 6 * D_MODEL

LANES = 128
SUBLANES = 8
VMEM_LIMIT_BYTES = 60 * 1024 * 1024

W_X, W_G, W_Q, W_K, W_V, W_R, W_LR, W_GA, W_GB = 0, 1280, 2560, 3072, 3584, 4608, 5632, 5648, 6672
N_IN = W_GB + D_MODEL
_LRO, _QO, _KO, _VO, _RO = 0, 128, 640, 1152, 2176
N_IN_B = _RO + D_MODEL

PROMPT_TILE = 512
GLA_CHUNK = 128
GLA_SPLITS = (64, 32)
GLA_FINE = 32
FFN_TILE = 512
VRG_PIECE = 512
LRU_PAIR = 2 * LRU_BW
REGROUP_SPAN = 256
SAMPLE_ROW_BLOCK = 128
SAMPLE_STATE_BLOCK = 8
SAMPLE_ROW_GROUP = 16


def _cparams(sem):
    return pltpu.CompilerParams(dimension_semantics=sem, vmem_limit_bytes=VMEM_LIMIT_BYTES)


def _full_spec(shape):
    n = len(shape)
    return pl.BlockSpec(shape, lambda *_: (0,) * n, pipeline_mode=pl.Buffered(1))


def _full_out(shape):
    n = len(shape)
    return pl.BlockSpec(shape, lambda *_: (0,) * n)


def _wspec(l, shape):
    n = len(shape)
    return pl.BlockSpec((None,) + tuple(shape), lambda *_: (l,) + (0,) * n, pipeline_mode=pl.Buffered(1))


def _softplus(y):
    return jnp.maximum(y, 0.0) + jnp.log1p(jnp.exp(-jnp.abs(y)))


def _sigmoid_of_half(t):
    return 0.5 * jnp.tanh(t) + 0.5


def _silu_of_half(t):
    return t + t * jnp.tanh(t)


def _silu(y):
    return _silu_of_half(0.5 * y)


def _sqrt_nonneg(s):
    return s * lax.rsqrt(jnp.maximum(s, TINY_F32))


def _gelu_tanh(y):
    return 0.5 * y * (1.0 + jnp.tanh(0.7978845608028654 * (y + 0.044715 * (y * y * y))))


def _rms(x):
    return lax.rsqrt(jnp.mean(x * x, axis=-1, keepdims=True) + EPS)


def _norm_mod(x, g, scale, shift):
    return (x * _rms(x)) * (g * (1.0 + scale)) + shift


def _mod_part(m, i):
    return m[:, i * D_MODEL:(i + 1) * D_MODEL]


def _dot(a, b):
    return jnp.dot(a.astype(BF16), b.astype(BF16), preferred_element_type=F32)


def _dot_nt(a, b):
    return lax.dot_general(a.astype(BF16), b.astype(BF16), (((1,), (1,)), ((), ())),
                           preferred_element_type=F32)


def _lru_coeffs_block(xb, wg, ba, bx, lam):
    lamc = -LRU_C * _softplus(-lam)
    pre = _dot(xb, wg)
    r = _sigmoid_of_half(pre[:, :LRU_BW] + ba)
    i = _sigmoid_of_half(pre[:, LRU_BW:] + bx)
    a = jnp.exp(lamc * r)
    return a, _sqrt_nonneg(1.0 - a * a) * (i * xb)


def _gla_gate(ulr, wa2_ref, gba):
    z = _dot(ulr, wa2_ref[...]) + gba
    return (jnp.minimum(z, 0.0) - jnp.log(1.0 + jnp.exp(-jnp.abs(z)))) * (1.0 / GLA_TAU)


def _group_cumsum(g, group):
    rin = lax.broadcasted_iota(jnp.int32, g.shape, 0) % group
    x = g
    k = 1
    while k < group:
        x = x + jnp.where(rin >= k, pltpu.roll(x, k, 0), 0.0)
        k *= 2
    return x


def _col_bcast(row):
    return jnp.broadcast_to(row, (LANES, LANES)).T


def _head_norm_gate(o, gng, ur_half):
    parts = []
    for hd in range(GLA_HEADS):
        sl = slice(hd * GLA_DV, (hd + 1) * GLA_DV)
        oh = o[:, sl]
        parts.append(oh * _rms(oh) * gng[:, sl])
    return jnp.concatenate(parts, axis=1) * _silu_of_half(ur_half)


def _merge_out(x, gt1, uga_half, ugb_half, ya, yb, pb_ref, wo_ref):
    pbv = _dot(yb, pb_ref[...])
    mm = _sigmoid_of_half(uga_half) * ya + _sigmoid_of_half(ugb_half) * pbv
    return x + gt1 * _dot(mm, wo_ref[...])


def _mod_kernel(c_ref, w_ref, b_ref, o_ref):
    o_ref[0] = _dot(_silu(c_ref[...]), w_ref[0]) + b_ref[0]


def _modulation(c_all, ada_w, ada_b):
    depth = ada_w.shape[0]
    rows = c_all.shape[0]
    nblk = N_MOD // D_MODEL
    return pl.pallas_call(
        _mod_kernel,
        grid=(depth, nblk),
        in_specs=[
            pl.BlockSpec((rows, D_MODEL), lambda l, j: (0, 0)),
            pl.BlockSpec((1, D_MODEL, D_MODEL), lambda l, j: (l, 0, j)),
            pl.BlockSpec((1, 1, D_MODEL), lambda l, j: (l, 0, j)),
        ],
        out_specs=pl.BlockSpec((1, rows, D_MODEL), lambda l, j: (l, 0, j)),
        out_shape=jax.ShapeDtypeStruct((depth, rows, N_MOD), F32),
        compiler_params=_cparams(("arbitrary", "arbitrary")),
        name="modulation",
    )(c_all, ada_w, ada_b.reshape(depth, 1, N_MOD))


def _shift_rows_in(blk, first_row):
    top = lax.broadcasted_iota(jnp.int32, blk.shape, 0) == 0
    return jnp.where(top, first_row, pltpu.roll(blk, 1, 0))


def _regroup_matrices(tt):
    ng = tt // SUBLANES
    ri = lax.broadcasted_iota(jnp.int32, (tt, tt), 0)
    ci = lax.broadcasted_iota(jnp.int32, (tt, tt), 1)
    to_blocks = jnp.where(ci == SUBLANES * (ri % ng) + ri // ng, 1.0, 0.0).astype(BF16)
    from_blocks = jnp.where(ci == ng * (ri % SUBLANES) + ri // SUBLANES, 1.0, 0.0).astype(BF16)
    return to_blocks, from_blocks


def _to_blocks(to_span, h16):
    span = to_span.shape[0]
    ngs = span // SUBLANES
    parts = [jnp.dot(to_span, h16[s * span:(s + 1) * span, :], preferred_element_type=F32).astype(BF16)
             for s in range(h16.shape[0] // span)]
    return jnp.concatenate([p[j * ngs:(j + 1) * ngs, :] for j in range(SUBLANES) for p in parts], axis=0)


def _lru_conv(ux, cw, cb, conv_rows):
    ng = ux.shape[0] // SUBLANES
    first_tail = SUBLANES - (CONV_W - 1)
    xs = [ux[j * ng:(j + 1) * ng, :] for j in range(SUBLANES)]
    prev = {k: _shift_rows_in(xs[k], conv_rows[k - first_tail:k - first_tail + 1, :])
            for k in range(first_tail, SUBLANES)}
    xcs = []
    for j in range(SUBLANES):
        acc = cb
        for i in range(CONV_W):
            d = CONV_W - 1 - i
            src = xs[j - d] if j >= d else prev[j - d + SUBLANES]
            acc = acc + src * cw[i:i + 1, :]
        xcs.append(acc)
    tail = jnp.concatenate([xs[k][ng - 1:ng, :] for k in range(first_tail, SUBLANES)], axis=0)
    return jnp.concatenate(xcs, axis=0), tail


def _lru_scan(a, bb, carry):
    ng = a.shape[0] // SUBLANES
    loc = [bb[0:ng, :]]
    cum = [a[0:ng, :]]
    for j in range(1, SUBLANES):
        aj = a[j * ng:(j + 1) * ng, :]
        loc.append(aj * loc[-1] + bb[j * ng:(j + 1) * ng, :])
        cum.append(aj * cum[-1])
    ga, gb = cum[-1], loc[-1]
    rowg = lax.broadcasted_iota(jnp.int32, ga.shape, 0)
    k = 1
    while k < ng:
        ga_sh = jnp.where(rowg >= k, pltpu.roll(ga, k, 0), 1.0)
        gb_sh = jnp.where(rowg >= k, pltpu.roll(gb, k, 0), 0.0)
        gb = ga * gb_sh + gb
        ga = ga * ga_sh
        k *= 2
    leaving = gb + ga * carry
    entering = _shift_rows_in(leaving, carry)
    hs = jnp.concatenate([loc[j] + cum[j] * entering for j in range(SUBLANES)], axis=0)
    return hs, leaving[ng - 1:ng, :]


def _lru_lane_block(n, uan, cw_ref, cb_ref, wg_ref, ba_ref, bx_ref, lam_ref, conv_s, hc_s):
    sl = slice(n * LRU_BW, (n + 1) * LRU_BW)
    xc, tail = _lru_conv(uan[0], cw_ref[:, sl], cb_ref[:, sl], conv_s[0:CONV_W - 1, sl])
    conv_s[0:CONV_W - 1, sl] = tail
    a, bb = _lru_coeffs_block(xc, wg_ref[n], ba_ref[:, sl], bx_ref[:, sl], lam_ref[:, sl])
    hs, last = _lru_scan(a, bb, hc_s[0:1, sl])
    hc_s[0:1, sl] = last
    return (hs * _gelu_tanh(uan[1])).astype(BF16)


def _rows_per_block(x, size, pick):
    parts = []
    for b in range(x.shape[0] // size):
        r = pick(b)
        row = jnp.zeros((1, x.shape[1]), x.dtype) if r is None else x[r:r + 1, :]
        parts.append(jnp.broadcast_to(row, (size, x.shape[1])))
    return jnp.concatenate(parts, axis=0)


def _gla_prep(ulr, uq, uk, wa2_ref, gba):
    c = GLA_CHUNK
    g = _gla_gate(ulr, wa2_ref, gba)
    bc = _group_cumsum(g, c)
    qs = uq * (GLA_DK ** -0.5)
    factors = []
    for size in GLA_SPLITS:
        before = _rows_per_block(bc, size, lambda b: None if (b * size) % c == 0 else b * size - 1)
        last = _rows_per_block(bc, size, lambda b: b * size + size - 1)
        factors.append((qs * jnp.exp(bc - before), uk * jnp.exp(last - bc)))
    mid = _rows_per_block(bc, GLA_FINE, lambda b: b * GLA_FINE + GLA_FINE // 2 - 1)
    factors.append((qs * jnp.exp(bc - mid), uk * jnp.exp(mid - bc)))
    return bc, qs * jnp.exp(bc), factors


def _gla_level_masks():
    c = GLA_CHUNK
    ri = lax.broadcasted_iota(jnp.int32, (c, c), 0)
    ci = lax.broadcasted_iota(jnp.int32, (c, c), 1)
    masks = [(ri // (2 * size) == ci // (2 * size)) & ((ri // size) % 2 == 1) & ((ci // size) % 2 == 0)
             for size in GLA_SPLITS]
    masks.append((ri // GLA_FINE == ci // GLA_FINE) & (ri >= ci))
    spans = [[(b * size, (b + 1) * size) for b in range(1, c // size, 2)] for size in GLA_SPLITS] + [[(0, c)]]
    return masks, spans


def _rows_at(parts, spans, total):
    out, pos, off = [], 0, 0
    for a, b in spans:
        if a > pos:
            out.append(jnp.zeros((a - pos, parts.shape[1]), parts.dtype))
        out.append(parts[off:off + b - a, :])
        off, pos = off + b - a, b
    if pos < total:
        out.append(jnp.zeros((total - pos, parts.shape[1]), parts.dtype))
    return jnp.concatenate(out, axis=0)


def _gla_chunks(bc, qt, factors, uk, uv, s_s):
    tt = bc.shape[0]
    c = GLA_CHUNK
    masks, spans = _gla_level_masks()
    o_rows = []
    for ch in range(tt // c):
        rows = slice(ch * c, (ch + 1) * c)
        bl = bc[ch * c + c - 1:ch * c + c, :]
        kd = uk[rows, :] * jnp.exp(bl - bc[rows, :])
        dec = jnp.exp(bl)
        o_heads = []
        for hd in range(GLA_HEADS):
            ks = slice(hd * GLA_DK, (hd + 1) * GLA_DK)
            qh = qt[rows, ks].astype(BF16)
            vh = uv[rows, hd * GLA_DV:(hd + 1) * GLA_DV].astype(BF16)
            att = jnp.zeros((c, c), F32)
            for (qf, kf), mask, sp in zip(factors, masks, spans):
                qsel = jnp.concatenate([qf[ch * c + a:ch * c + b, ks] for a, b in sp], axis=0)
                att = jnp.where(mask, _rows_at(_dot_nt(qsel, kf[rows, ks]), sp, c), att)
            s_old = s_s[hd]
            o_heads.append(_dot(att, vh) + _dot(qh, s_old))
            dm = _col_bcast(dec[:, ks])
            s_s[hd] = s_old * jnp.concatenate([dm, dm], axis=1) + _dot(kd[:, ks].T, vh)
        o_rows.append(jnp.concatenate(o_heads, axis=1))
    return jnp.concatenate(o_rows, axis=0)


def _prompt_mixer_kernel(x_ref, mod_ref, g1_ref, wt_ref, cw_ref, cb_ref, wg_ref, ba_ref, bx_ref, lam_ref,
                         pa_ref, wa2_ref, gba_ref, gng_ref, pb_ref, wo_ref,
                         x1_ref, conv_ref, lru_ref, st_ref, conv_s, hc_s, s_s, uvrg_s):
    t = pl.program_id(1)

    @pl.when(t == 0)
    def _():
        conv_s[...] = jnp.zeros_like(conv_s)
        hc_s[...] = jnp.zeros_like(hc_s)
        s_s[...] = jnp.zeros_like(s_s)

    x = x_ref[0]
    mod = mod_ref[0]
    hb16 = _norm_mod(x, g1_ref[...], _mod_part(mod, 1), _mod_part(mod, 0)).astype(BF16)
    to_span, _ = _regroup_matrices(REGROUP_SPAN)
    _, from_blocks = _regroup_matrices(x.shape[0])
    u_qk = _dot_nt(hb16, wt_ref[W_Q:W_V, :])
    u_lr = _dot_nt(hb16, wt_ref[W_LR:W_LR + LANES, :])
    hb = _to_blocks(to_span, hb16)
    uk = u_qk[:, GLA_DK_TOTAL:]
    bc, qt, factors = _gla_prep(u_lr, u_qk[:, :GLA_DK_TOTAL], uk, wa2_ref, gba_ref[...])
    piece_rows = ([W_V + i * VRG_PIECE for i in range((W_LR - W_V) // VRG_PIECE)]
                  + [W_GA + i * VRG_PIECE for i in range((N_IN - W_GA) // VRG_PIECE)])
    per_pair = -(-len(piece_rows) // (LRU_BLOCKS // 2))
    ya_blocks = []
    for m in range(LRU_BLOCKS // 2):
        ux2 = _dot_nt(hb, wt_ref[W_X + m * LRU_PAIR:W_X + (m + 1) * LRU_PAIR, :])
        ug2 = _dot_nt(hb, wt_ref[W_G + m * LRU_PAIR:W_G + (m + 1) * LRU_PAIR, :])
        for i in range(m * per_pair, min((m + 1) * per_pair, len(piece_rows))):
            uvrg_s[:, i * VRG_PIECE:(i + 1) * VRG_PIECE] = _dot_nt(hb16, wt_ref[piece_rows[i]:piece_rows[i] + VRG_PIECE, :])
        for j in range(2):
            uan = (ux2[:, j * LRU_BW:(j + 1) * LRU_BW], ug2[:, j * LRU_BW:(j + 1) * LRU_BW])
            ya_blocks.append(_lru_lane_block(2 * m + j, uan, cw_ref, cb_ref, wg_ref, ba_ref, bx_ref, lam_ref,
                                             conv_s, hc_s))
    o = _gla_chunks(bc, qt, factors, uk, uvrg_s, s_s)
    ya = jnp.dot(from_blocks, jnp.concatenate(ya_blocks, axis=1), preferred_element_type=F32).astype(BF16)
    ya = jnp.dot(ya, pa_ref[...], preferred_element_type=F32)
    yb = _head_norm_gate(o, gng_ref[...], uvrg_s[:, GLA_DV_TOTAL:GLA_DV_TOTAL + D_MODEL])
    x1_ref[0] = _merge_out(x, _mod_part(mod, 2), uvrg_s[:, 2 * D_MODEL:3 * D_MODEL], uvrg_s[:, 3 * D_MODEL:],
                           ya, yb, pb_ref, wo_ref)

    @pl.when(t == pl.num_programs(1) - 1)
    def _():
        conv_ref[0] = conv_s[0:CONV_W - 1, :]
        lru_ref[0] = hc_s[0:1, :]
        st_ref[0] = s_s[...]


def _prompt_mixer(x, mod_p, w, l):
    b, t, _ = x.shape
    tt = PROMPT_TILE
    per_seq = t // tt
    seq_spec = pl.BlockSpec((1, tt, D_MODEL), lambda i, j: (i, j, 0))
    return pl.pallas_call(
        _prompt_mixer_kernel,
        grid=(b, per_seq),
        in_specs=[
            seq_spec,
            pl.BlockSpec((1, 1, N_MOD), lambda i, j: (i, 0, 0)),
            _wspec(l, (1, D_MODEL)),
            _wspec(l, (N_IN, D_MODEL)),
            _wspec(l, (CONV_W, D_RNN)),
            _wspec(l, (1, D_RNN)),
            _wspec(l, (LRU_BLOCKS, LRU_BW, 2 * LRU_BW)),
            _wspec(l, (1, D_RNN)),
            _wspec(l, (1, D_RNN)),
            _wspec(l, (1, D_RNN)),
            _wspec(l, (D_RNN, D_MODEL)),
            _wspec(l, (LANES, GLA_DK_TOTAL)),
            _wspec(l, (1, GLA_DK_TOTAL)),
            _wspec(l, (1, GLA_DV_TOTAL)),
            _wspec(l, (GLA_DV_TOTAL, D_MODEL)),
            _wspec(l, (D_MODEL, D_MODEL)),
        ],
        out_specs=[
            seq_spec,
            pl.BlockSpec((1, CONV_W - 1, D_RNN), lambda i, j: (i, 0, 0)),
            pl.BlockSpec((1, 1, D_RNN), lambda i, j: (i, 0, 0)),
            pl.BlockSpec((1, GLA_HEADS, GLA_DK, GLA_DV), lambda i, j: (i, 0, 0, 0)),
        ],
        out_shape=[
            jax.ShapeDtypeStruct((b, t, D_MODEL), F32),
            jax.ShapeDtypeStruct((b, CONV_W - 1, D_RNN), F32),
            jax.ShapeDtypeStruct((b, 1, D_RNN), F32),
            jax.ShapeDtypeStruct((b, GLA_HEADS, GLA_DK, GLA_DV), F32),
        ],
        scratch_shapes=[
            pltpu.VMEM((SUBLANES, D_RNN), F32),
            pltpu.VMEM((SUBLANES, D_RNN), F32),
            pltpu.VMEM((GLA_HEADS, GLA_DK, GLA_DV), F32),
            pltpu.VMEM((tt, 2 * GLA_DV_TOTAL + 2 * D_MODEL), F32),
        ],
        compiler_params=_cparams(("arbitrary", "arbitrary")),
        name="prompt_mixer",
    )(x, mod_p, w["g1"], w["win_t"], w["conv_w"], w["conv_b"], w["wgate"], w["lru_ba"], w["lru_bx"],
      w["lru_lambda"], w["proj_a"], w["wa2"], w["gla_ba"], w["gla_norm_g"], w["proj_b"], w["w_out"])


def _ffn_rows(x, shift, scale, gate, g2, w1_ref, w2_ref, fg, final_norm):
    h2 = _norm_mod(x, g2, scale, shift)
    hb = h2.astype(BF16)
    f1_half = _dot(hb, w1_ref[:, :D_FF])
    f2 = _dot(hb, w1_ref[:, D_FF:])
    x2 = x + gate * _dot(_silu_of_half(f1_half) * f2, w2_ref[...])
    if final_norm:
        x2 = x2 * _rms(x2) * fg
    return x2


def _ffn_kernel(x_ref, mod_ref, g2_ref, w1_ref, w2_ref, fg_ref, o_ref, *, final_norm):
    m = mod_ref[0]
    o_ref[...] = _ffn_rows(x_ref[...], _mod_part(m, 3), _mod_part(m, 4), _mod_part(m, 5), g2_ref[...], w1_ref, w2_ref,
                           fg_ref[...], final_norm)


def _ffn(x2d, mod3, rows_per_mod, tile, w, l, final_g, final_norm):
    mrows = x2d.shape[0]
    r = mod3.shape[1]
    per = rows_per_mod // tile
    row_spec = pl.BlockSpec((tile, D_MODEL), lambda i: (i, 0))
    return pl.pallas_call(
        functools.partial(_ffn_kernel, final_norm=final_norm),
        grid=(mrows // tile,),
        in_specs=[
            row_spec,
            pl.BlockSpec((1, r, N_MOD), lambda i: (i // per, 0, 0)),
            _wspec(l, (1, D_MODEL)),
            _wspec(l, (D_MODEL, 2 * D_FF)),
            _wspec(l, (D_FF, D_MODEL)),
            _full_spec((1, D_MODEL)),
        ],
        out_specs=row_spec,
        out_shape=jax.ShapeDtypeStruct((mrows, D_MODEL), F32),
        compiler_params=_cparams(("arbitrary",)),
        name="ffn_final" if final_norm else "ffn",
    )(x2d, mod3, w["g2"], w["ffn_w1"], w["ffn_w2"], final_g)


def _row_regroup(rows, inner, outer):
    ri = lax.broadcasted_iota(jnp.int32, (rows, rows), 0)
    ci = lax.broadcasted_iota(jnp.int32, (rows, rows), 1)
    return jnp.where(ci == (ri % inner) * outer + ri // inner, 1.0, 0.0).astype(BF16)


def _sample_in_kernel(x_ref, mod_ref, g1_ref, win_ref, cw_ref, cb_ref, wg_ref, ba_ref, bx_ref, lam_ref, pa_ref,
                      conv0_ref, h0_ref, ya_ref, conv_ref, lru_ref, u_ref, ug_ref):
    rows = x_ref.shape[0]
    ns = h0_ref.shape[0]
    nt = rows // ns
    m = mod_ref[...]
    tile = lambda v: jnp.concatenate([v] * nt, axis=0)
    h = _norm_mod(x_ref[...], g1_ref[...], tile(_mod_part(m, 1)), tile(_mod_part(m, 0))).astype(BF16)
    ux = _dot_nt(h, win_ref[W_X:W_G, :])
    ug = _dot_nt(h, win_ref[W_G:W_Q, :])
    xa = [conv0_ref[i] for i in range(CONV_W - 1)] + [ux[i * ns:(i + 1) * ns, :] for i in range(nt)]
    cw = cw_ref[...]
    xcs = []
    for ti in range(nt):
        acc = cb_ref[...]
        for i in range(CONV_W):
            acc = acc + xa[ti + i] * cw[i:i + 1, :]
        xcs.append(acc)
    xc = jnp.concatenate(xcs, axis=0)
    coeffs = [_lru_coeffs_block(xc[:, n * LRU_BW:(n + 1) * LRU_BW], wg_ref[n], ba_ref[:, n * LRU_BW:(n + 1) * LRU_BW],
                                bx_ref[:, n * LRU_BW:(n + 1) * LRU_BW], lam_ref[:, n * LRU_BW:(n + 1) * LRU_BW])
              for n in range(LRU_BLOCKS)]
    a = jnp.concatenate([c[0] for c in coeffs], axis=1)
    bb = jnp.concatenate([c[1] for c in coeffs], axis=1)
    hc = h0_ref[...]
    hs = []
    for ti in range(nt):
        sl = slice(ti * ns, (ti + 1) * ns)
        hc = a[sl, :] * hc + bb[sl, :]
        hs.append(hc)
    ya_ref[...] = _dot(jnp.concatenate(hs, axis=0) * _gelu_tanh(ug), pa_ref[...])
    for i in range(CONV_W - 1):
        conv_ref[i] = xa[nt + i]
    lru_ref[...] = hc
    ug_ref[...] = _dot_nt(h, win_ref[W_GA:N_IN, :])
    hs_major = jnp.dot(_row_regroup(rows, nt, ns), h, preferred_element_type=F32).astype(BF16)
    u_ref[...] = jnp.concatenate([_dot_nt(hs_major, win_ref[W_LR:W_LR + LANES, :]),
                                  _dot_nt(hs_major, win_ref[W_Q:W_LR, :])], axis=1)


def _sample_in(x_tm, mod_s, conv0_tm, h0_all, w, l):
    rows = x_tm.shape[0]
    ns = h0_all.shape[1]
    return pl.pallas_call(
        _sample_in_kernel,
        grid=(1,),
        in_specs=[
            _full_spec((rows, D_MODEL)),
            _full_spec((ns, N_MOD)),
            _wspec(l, (1, D_MODEL)),
            _wspec(l, (N_IN, D_MODEL)),
            _wspec(l, (CONV_W, D_RNN)),
            _wspec(l, (1, D_RNN)),
            _wspec(l, (LRU_BLOCKS, LRU_BW, 2 * LRU_BW)),
            _wspec(l, (1, D_RNN)),
            _wspec(l, (1, D_RNN)),
            _wspec(l, (1, D_RNN)),
            _wspec(l, (D_RNN, D_MODEL)),
            _full_spec((CONV_W - 1, ns, D_RNN)),
            _wspec(l, (ns, D_RNN)),
        ],
        out_specs=[
            _full_out((rows, D_MODEL)),
            _full_out((CONV_W - 1, ns, D_RNN)),
            _full_out((ns, D_RNN)),
            _full_out((rows, N_IN_B)),
            _full_out((rows, 2 * D_MODEL)),
        ],
        out_shape=[
            jax.ShapeDtypeStruct((rows, D_MODEL), F32),
            jax.ShapeDtypeStruct((CONV_W - 1, ns, D_RNN), F32),
            jax.ShapeDtypeStruct((ns, D_RNN), F32),
            jax.ShapeDtypeStruct((rows, N_IN_B), F32),
            jax.ShapeDtypeStruct((rows, 2 * D_MODEL), F32),
        ],
        compiler_params=_cparams(("arbitrary",)),
        name="sample_in",
    )(x_tm, mod_s, w["g1"], w["win_t"], w["conv_w"], w["conv_b"], w["wgate"], w["lru_ba"], w["lru_bx"],
      w["lru_lambda"], w["proj_a"], conv0_tm, h0_all)


def _sample_gla_kernel(u_ref, wa2_ref, gba_ref, gng_ref, s0_ref, *rest, nt, layer, carries_buffer):
    if carries_buffer:
        yb_ref, st_ref, o_s, q_s, v_s, kdt_s, bl_s = rest[1:]
    else:
        yb_ref, st_all_ref, o_s, q_s, v_s, kdt_s, bl_s = rest
        st_ref = st_all_ref.at[layer]
        for other in range(st_all_ref.shape[0]):
            if other != layer:
                st_all_ref[other] = jnp.zeros(st_all_ref.shape[1:], F32)
    rows = u_ref.shape[0]
    steps_per_rows = rows // (SAMPLE_STATE_BLOCK * nt)
    sub = pl.program_id(0) % steps_per_rows
    rg = SAMPLE_ROW_GROUP
    seq_per_group = rg // nt

    @pl.when(sub == 0)
    def _():
        u = u_ref[...]
        g = _gla_gate(u[:, _LRO:_LRO + LANES], wa2_ref, gba_ref[...])
        bc = _group_cumsum(g, nt)
        rin = lax.broadcasted_iota(jnp.int32, g.shape, 0) % nt
        sfx = jnp.zeros_like(g)
        for k in range(1, nt):
            sfx = sfx + jnp.where(rin + k < nt, pltpu.roll(g, rows - k, 0), 0.0)
        qt = u[:, _QO:_QO + GLA_DK_TOTAL] * (jnp.exp(bc) * (GLA_DK ** -0.5))
        uk = u[:, _KO:_KO + GLA_DK_TOTAL]
        kt = uk * jnp.exp(-bc)
        kd = uk * jnp.exp(sfx)
        q_s[...] = qt.astype(BF16)
        v_s[...] = u[:, _VO:_VO + GLA_DV_TOTAL].astype(BF16)
        bl_s[...] = bc + sfx
        ri = lax.broadcasted_iota(jnp.int32, (rows, rows), 0)
        ci = lax.broadcasted_iota(jnp.int32, (rows, rows), 1)
        same_seq_causal = (ri // nt == ci // nt) & (ri >= ci)
        for hd in range(GLA_HEADS):
            ks = slice(hd * GLA_DK, (hd + 1) * GLA_DK)
            att = jnp.where(same_seq_causal, _dot_nt(qt[:, ks], kt[:, ks]), 0.0)
            o_s[:, hd * GLA_DV:(hd + 1) * GLA_DV] = _dot(att, v_s[:, hd * GLA_DV:(hd + 1) * GLA_DV])
            kdt_s[hd] = kd[:, ks].T

    lane_seq = lax.broadcasted_iota(jnp.int32, (GLA_DK, rows), 1) // nt
    row_seq = lax.broadcasted_iota(jnp.int32, (rg, GLA_DV), 0) // nt

    def group(gi, carry):
        r0 = pl.multiple_of((sub * (SAMPLE_STATE_BLOCK // seq_per_group) + gi) * rg, rg)
        dec_g = jnp.exp(bl_s[pl.ds(r0, rg), :])
        for hd in range(GLA_HEADS):
            ks = slice(hd * GLA_DK, (hd + 1) * GLA_DK)
            vsl = slice(hd * GLA_DV, (hd + 1) * GLA_DV)
            qg = q_s[pl.ds(r0, rg), ks]
            vh = v_s[:, vsl]
            inter = jnp.zeros((rg, GLA_DV), F32)
            for j in range(seq_per_group):
                s_loc = gi * seq_per_group + j
                s_row = sub * SAMPLE_STATE_BLOCK + s_loc
                s_old = s0_ref[s_loc, hd]
                inter = jnp.where(row_seq == j, _dot(qg, s_old), inter)
                kdt = jnp.where(lane_seq == s_row, kdt_s[hd], 0.0)
                dm = _col_bcast(dec_g[j * nt:j * nt + 1, ks])
                st_ref[s_loc, hd] = s_old * jnp.concatenate([dm, dm], axis=1) + _dot(kdt, vh)
            o_s[pl.ds(r0, rg), vsl] = o_s[pl.ds(r0, rg), vsl] + inter
        return carry

    lax.fori_loop(0, SAMPLE_STATE_BLOCK // seq_per_group, group, 0)

    @pl.when(sub == steps_per_rows - 1)
    def _():
        yb_ref[...] = _head_norm_gate(o_s[...], gng_ref[...], u_ref[:, _RO:_RO + D_MODEL])


def _sample_gla(u_sm, state_all, new_state_all, w, l, nt):
    rows = u_sm.shape[0]
    br = SAMPLE_ROW_BLOCK
    sb = SAMPLE_STATE_BLOCK
    per = br // (sb * nt)
    state_spec = pl.BlockSpec((None, sb, GLA_HEADS, GLA_DK, GLA_DV), lambda i: (l, i, 0, 0, 0))
    out_state_spec = state_spec
    if new_state_all is None:
        out_state_spec = pl.BlockSpec((state_all.shape[0], sb, GLA_HEADS, GLA_DK, GLA_DV), lambda i: (0, i, 0, 0, 0))
    in_specs = [
        pl.BlockSpec((br, N_IN_B), lambda i: (i // per, 0)),
        _wspec(l, (LANES, GLA_DK_TOTAL)),
        _wspec(l, (1, GLA_DK_TOTAL)),
        _wspec(l, (1, GLA_DV_TOTAL)),
        state_spec,
    ]
    args = [u_sm, w["wa2"], w["gla_ba"], w["gla_norm_g"], state_all]
    aliases = {}
    if new_state_all is not None:
        in_specs.append(pl.BlockSpec(memory_space=pl.ANY))
        args.append(new_state_all)
        aliases = {len(args) - 1: 1}
    return pl.pallas_call(
        functools.partial(_sample_gla_kernel, nt=nt, layer=l, carries_buffer=new_state_all is not None),
        grid=(rows // (sb * nt),),
        in_specs=in_specs,
        out_specs=[
            pl.BlockSpec((br, D_MODEL), lambda i: (i // per, 0)),
            out_state_spec,
        ],
        out_shape=[
            jax.ShapeDtypeStruct((rows, D_MODEL), F32),
            jax.ShapeDtypeStruct(state_all.shape, F32),
        ],
        input_output_aliases=aliases,
        scratch_shapes=[
            pltpu.VMEM((br, GLA_DV_TOTAL), F32),
            pltpu.VMEM((br, GLA_DK_TOTAL), BF16),
            pltpu.VMEM((br, GLA_DV_TOTAL), BF16),
            pltpu.VMEM((GLA_HEADS, GLA_DK, br), F32),
            pltpu.VMEM((br, GLA_DK_TOTAL), F32),
        ],
        compiler_params=_cparams(("arbitrary",)),
        name="sample_gla",
    )(*args)


def _sample_out_kernel(x_ref, mod_ref, ug_ref, ya_ref, yb_ref, pb_ref, wo_ref, g2_ref, w1_ref, w2_ref, fg_ref, o_ref, *,
                       final_norm):
    rows = x_ref.shape[0]
    ns = mod_ref.shape[0]
    nt = rows // ns
    m = mod_ref[...]
    tile = lambda v: jnp.concatenate([v] * nt, axis=0)
    yb = jnp.dot(_row_regroup(rows, ns, nt), yb_ref[...].astype(BF16), preferred_element_type=F32).astype(BF16)
    x1 = _merge_out(x_ref[...], tile(_mod_part(m, 2)), ug_ref[:, :D_MODEL], ug_ref[:, D_MODEL:], ya_ref[...], yb,
                    pb_ref, wo_ref)
    o_ref[...] = _ffn_rows(x1, tile(_mod_part(m, 3)), tile(_mod_part(m, 4)), tile(_mod_part(m, 5)), g2_ref[...],
                           w1_ref, w2_ref, fg_ref[...], final_norm)


def _sample_out(x_tm, mod_s, ug_tm, ya, yb_sm, w, l, final_g, final_norm):
    rows = x_tm.shape[0]
    return pl.pallas_call(
        functools.partial(_sample_out_kernel, final_norm=final_norm),
        grid=(1,),
        in_specs=[
            _full_spec((rows, D_MODEL)),
            _full_spec(mod_s.shape),
            _full_spec((rows, 2 * D_MODEL)),
            _full_spec((rows, D_MODEL)),
            _full_spec((rows, D_MODEL)),
            _wspec(l, (GLA_DV_TOTAL, D_MODEL)),
            _wspec(l, (D_MODEL, D_MODEL)),
            _wspec(l, (1, D_MODEL)),
            _wspec(l, (D_MODEL, 2 * D_FF)),
            _wspec(l, (D_FF, D_MODEL)),
            _full_spec((1, D_MODEL)),
        ],
        out_specs=_full_out((rows, D_MODEL)),
        out_shape=jax.ShapeDtypeStruct((rows, D_MODEL), F32),
        compiler_params=_cparams(("arbitrary",)),
        name="sample_out_final" if final_norm else "sample_out",
    )(x_tm, mod_s, ug_tm, ya, yb_sm, w["proj_b"], w["w_out"], w["g2"], w["ffn_w1"], w["ffn_w2"], final_g)


def _stacked_weights(norm1_g, norm2_g, w_in, conv_w, conv_b, lru_wa, lru_ba, lru_wx, lru_bx, lru_lambda,
                     gla_wa2, gla_ba, gla_norm_g, proj_a, proj_b, w_out, ffn_w1, ffn_w2):
    depth = w_in.shape[0]
    row = lambda v: v.reshape(depth, 1, -1)
    rows_in = jnp.arange(N_IN)
    in_scale = jnp.where(((rows_in >= W_R) & (rows_in < W_LR)) | (rows_in >= W_GA), 0.5, 1.0).astype(w_in.dtype)
    ff_scale = jnp.where(jnp.arange(2 * D_FF) < D_FF, 0.5, 1.0).astype(ffn_w1.dtype)
    return {
        "g1": row(norm1_g), "g2": row(norm2_g),
        "win_t": (jnp.swapaxes(w_in, 1, 2) * in_scale[None, :, None]).astype(BF16),
        "conv_w": conv_w, "conv_b": row(conv_b),
        "wgate": (0.5 * jnp.concatenate([lru_wa, lru_wx], axis=-1)).astype(BF16),
        "lru_ba": row(0.5 * lru_ba), "lru_bx": row(0.5 * lru_bx), "lru_lambda": row(lru_lambda),
        "wa2": jnp.pad(gla_wa2, ((0, 0), (0, LANES - GLA_RANK), (0, 0))).astype(BF16),
        "gla_ba": row(gla_ba), "gla_norm_g": row(gla_norm_g),
        "proj_a": proj_a.astype(BF16), "proj_b": proj_b.astype(BF16), "w_out": w_out.astype(BF16),
        "ffn_w1": (ffn_w1 * ff_scale).astype(BF16), "ffn_w2": ffn_w2.astype(BF16),
    }


def kernel(x_prompt, x_sample, c_prompt, c_sample, state_conv, state_lru, state_gla, norm1_g, norm2_g, ada_w, ada_b,
           w_in, conv_w, conv_b, lru_wa, lru_ba, lru_wx, lru_bx, lru_lambda, gla_wa2, gla_ba, gla_norm_g, proj_a,
           proj_b, w_out, ffn_w1, ffn_w2, final_g):
    bp, tp, _ = x_prompt.shape
    bs, ts, _ = x_sample.shape
    depth = w_in.shape[0]
    mod = _modulation(jnp.concatenate([c_prompt, c_sample], axis=0), ada_w, ada_b)
    w = _stacked_weights(norm1_g, norm2_g, w_in, conv_w, conv_b, lru_wa, lru_ba, lru_wx, lru_bx, lru_lambda,
                         gla_wa2, gla_ba, gla_norm_g, proj_a, proj_b, w_out, ffn_w1, ffn_w2)
    fg = final_g.reshape(1, D_MODEL)
    xp = x_prompt
    xs = x_sample.transpose(1, 0, 2).reshape(ts * bs, D_MODEL)
    conv_p, lru_p, gla_p, conv_s, lru_s = [], [], [], [], []
    gla_s = None
    for l in range(depth):
        last = l == depth - 1
        mod_p = mod[l, :bp].reshape(bp, 1, N_MOD)
        mod_s = mod[l, bp:]
        x1, cb, ht, st = _prompt_mixer(xp, mod_p, w, l)
        xp = _ffn(x1.reshape(bp * tp, D_MODEL), mod_p, tp, FFN_TILE, w, l, fg, last).reshape(bp, tp, D_MODEL)
        conv_p.append(cb)
        lru_p.append(ht.reshape(bp, D_RNN))
        gla_p.append(st)
        ya_s, cb_s, ht_s, u_sm, ug_tm = _sample_in(xs, mod_s, state_conv[l].transpose(1, 0, 2), state_lru, w, l)
        yb_sm, gla_s = _sample_gla(u_sm, state_gla, gla_s, w, l, ts)
        xs = _sample_out(xs, mod_s, ug_tm, ya_s, yb_sm, w, l, fg, last)
        conv_s.append(cb_s.transpose(1, 0, 2))
        lru_s.append(ht_s)
    y_sample = xs.reshape(ts, bs, D_MODEL).transpose(1, 0, 2)
    return (xp, y_sample, jnp.stack(conv_p), jnp.stack(lru_p), jnp.stack(gla_p),
            jnp.stack(conv_s), jnp.stack(lru_s), gla_s)
```

```python
import functools

import jax
import jax.numpy as jnp
from jax import lax
from jax.experimental import pallas as pl
from jax.experimental.pallas import tpu as pltpu

F32 = jnp.float32
BF16 = jnp.bfloat16

D_MODEL = 1024
D_RNN = 1280
LRU_BW = 128
LRU_BLOCKS = D_RNN // LRU_BW
CONV_W = 4
LRU_C = 8.0
GLA_HEADS = 4
GLA_DK = 128
GLA_DV = 256
GLA_DK_TOTAL = GLA_HEADS * GLA_DK
GLA_DV_TOTAL = GLA_HEADS * GLA_DV
GLA_RANK = 16
GLA_TAU = 16.0
D_FF = 2816
EPS = 1e-6
TINY_F32 = 1.1754944e-38
N_MOD = 6 * D_MODEL

LANES = 128
SUBLANES = 8
VMEM_LIMIT_BYTES = 60 * 1024 * 1024

W_X, W_G, W_Q, W_K, W_V, W_R, W_LR, W_GA, W_GB = 0, 1280, 2560, 3072, 3584, 4608, 5632, 5648, 6672
N_IN = W_GB + D_MODEL
_LRO, _QO, _KO, _VO, _RO = 0, 128, 640, 1152, 2176
N_IN_B = _RO + D_MODEL

PROMPT_TILE = 512
GLA_CHUNK = 128
GLA_SPLITS = (64, 32)
GLA_FINE = 32
FFN_TILE = 512
VRG_PIECE = 512
LRU_PAIR = 2 * LRU_BW
REGROUP_SPAN = 256
SAMPLE_ROW_BLOCK = 128
SAMPLE_STATE_BLOCK = 8
SAMPLE_ROW_GROUP = 16


def _cparams(sem):
    return pltpu.CompilerParams(dimension_semantics=sem, vmem_limit_bytes=VMEM_LIMIT_BYTES)


def _full_spec(shape):
    n = len(shape)
    return pl.BlockSpec(shape, lambda *_: (0,) * n, pipeline_mode=pl.Buffered(1))


def _full_out(shape):
    n = len(shape)
    return pl.BlockSpec(shape, lambda *_: (0,) * n)


def _wspec(l, shape):
    n = len(shape)
    return pl.BlockSpec((None,) + tuple(shape), lambda *_: (l,) + (0,) * n, pipeline_mode=pl.Buffered(1))


def _softplus(y):
    return jnp.maximum(y, 0.0) + jnp.log1p(jnp.exp(-jnp.abs(y)))


def _sigmoid_of_half(t):
    return 0.5 * jnp.tanh(t) + 0.5


def _silu_of_half(t):
    return t + t * jnp.tanh(t)


def _silu(y):
    return _silu_of_half(0.5 * y)


def _sqrt_nonneg(s):
    return s * lax.rsqrt(jnp.maximum(s, TINY_F32))


def _gelu_tanh(y):
    return 0.5 * y * (1.0 + jnp.tanh(0.7978845608028654 * (y + 0.044715 * (y * y * y))))


def _rms(x):
    return lax.rsqrt(jnp.mean(x * x, axis=-1, keepdims=True) + EPS)


def _norm_mod(x, g, scale, shift):
    return (x * _rms(x)) * (g * (1.0 + scale)) + shift


def _mod_part(m, i):
    return m[:, i * D_MODEL:(i + 1) * D_MODEL]


def _dot(a, b):
    return jnp.dot(a.astype(BF16), b.astype(BF16), preferred_element_type=F32)


def _dot_nt(a, b):
    return lax.dot_general(a.astype(BF16), b.astype(BF16), (((1,), (1,)), ((), ())),
                           preferred_element_type=F32)


def _lru_coeffs_block(xb, wg, ba, bx, lam):
    lamc = -LRU_C * _softplus(-lam)
    pre = _dot(xb, wg)
    r = _sigmoid_of_half(pre[:, :LRU_BW] + ba)
    i = _sigmoid_of_half(pre[:, LRU_BW:] + bx)
    a = jnp.exp(lamc * r)
    return a, _sqrt_nonneg(1.0 - a * a) * (i * xb)


def _gla_gate(ulr, wa2_ref, gba):
    z = _dot(ulr, wa2_ref[...]) + gba
    return (jnp.minimum(z, 0.0) - jnp.log(1.0 + jnp.exp(-jnp.abs(z)))) * (1.0 / GLA_TAU)


def _group_cumsum(g, group):
    rin = lax.broadcasted_iota(jnp.int32, g.shape, 0) % group
    x = g
    k = 1
    while k < group:
        x = x + jnp.where(rin >= k, pltpu.roll(x, k, 0), 0.0)
        k *= 2
    return x


def _col_bcast(row):
    return jnp.broadcast_to(row, (LANES, LANES)).T


def _head_norm_gate(o, gng, ur_half):
    parts = []
    for hd in range(GLA_HEADS):
        sl = slice(hd * GLA_DV, (hd + 1) * GLA_DV)
        oh = o[:, sl]
        parts.append(oh * _rms(oh) * gng[:, sl])
    return jnp.concatenate(parts, axis=1) * _silu_of_half(ur_half)


def _merge_out(x, gt1, uga_half, ugb_half, ya, yb, pb_ref, wo_ref):
    pbv = _dot(yb, pb_ref[...])
    mm = _sigmoid_of_half(uga_half) * ya + _sigmoid_of_half(ugb_half) * pbv
    return x + gt1 * _dot(mm, wo_ref[...])


def _mod_kernel(c_ref, w_ref, b_ref, o_ref):
    o_ref[0] = _dot(_silu(c_ref[...]), w_ref[0]) + b_ref[0]


def _modulation(c_all, ada_w, ada_b):
    depth = ada_w.shape[0]
    rows = c_all.shape[0]
    nblk = N_MOD // D_MODEL
    return pl.pallas_call(
        _mod_kernel,
        grid=(depth, nblk),
        in_specs=[
            pl.BlockSpec((rows, D_MODEL), lambda l, j: (0, 0)),
            pl.BlockSpec((1, D_MODEL, D_MODEL), lambda l, j: (l, 0, j)),
            pl.BlockSpec((1, 1, D_MODEL), lambda l, j: (l, 0, j)),
        ],
        out_specs=pl.BlockSpec((1, rows, D_MODEL), lambda l, j: (l, 0, j)),
        out_shape=jax.ShapeDtypeStruct((depth, rows, N_MOD), F32),
        compiler_params=_cparams(("arbitrary", "arbitrary")),
        name="modulation",
    )(c_all, ada_w, ada_b.reshape(depth, 1, N_MOD))


def _shift_rows_in(blk, first_row):
    top = lax.broadcasted_iota(jnp.int32, blk.shape, 0) == 0
    return jnp.where(top, first_row, pltpu.roll(blk, 1, 0))


def _regroup_matrices(tt):
    ng = tt // SUBLANES
    ri = lax.broadcasted_iota(jnp.int32, (tt, tt), 0)
    ci = lax.broadcasted_iota(jnp.int32, (tt, tt), 1)
    to_blocks = jnp.where(ci == SUBLANES * (ri % ng) + ri // ng, 1.0, 0.0).astype(BF16)
    from_blocks = jnp.where(ci == ng * (ri % SUBLANES) + ri // SUBLANES, 1.0, 0.0).astype(BF16)
    return to_blocks, from_blocks


def _to_blocks(to_span, h16):
    span = to_span.shape[0]
    ngs = span // SUBLANES
    parts = [jnp.dot(to_span, h16[s * span:(s + 1) * span, :], preferred_element_type=F32).astype(BF16)
             for s in range(h16.shape[0] // span)]
    return jnp.concatenate([p[j * ngs:(j + 1) * ngs, :] for j in range(SUBLANES) for p in parts], axis=0)


def _lru_conv(ux, cw, cb, conv_rows):
    ng = ux.shape[0] // SUBLANES
    first_tail = SUBLANES - (CONV_W - 1)
    xs = [ux[j * ng:(j + 1) * ng, :] for j in range(SUBLANES)]
    prev = {k: _shift_rows_in(xs[k], conv_rows[k - first_tail:k - first_tail + 1, :])
            for k in range(first_tail, SUBLANES)}
    xcs = []
    for j in range(SUBLANES):
        acc = cb
        for i in range(CONV_W):
            d = CONV_W - 1 - i
            src = xs[j - d] if j >= d else prev[j - d + SUBLANES]
            acc = acc + src * cw[i:i + 1, :]
        xcs.append(acc)
    tail = jnp.concatenate([xs[k][ng - 1:ng, :] for k in range(first_tail, SUBLANES)], axis=0)
    return jnp.concatenate(xcs, axis=0), tail


def _lru_scan(a, bb, carry):
    ng = a.shape[0] // SUBLANES
    loc = [bb[0:ng, :]]
    cum = [a[0:ng, :]]
    for j in range(1, SUBLANES):
        aj = a[j * ng:(j + 1) * ng, :]
        loc.append(aj * loc[-1] + bb[j * ng:(j + 1) * ng, :])
        cum.append(aj * cum[-1])
    ga, gb = cum[-1], loc[-1]
    rowg = lax.broadcasted_iota(jnp.int32, ga.shape, 0)
    k = 1
    while k < ng:
        ga_sh = jnp.where(rowg >= k, pltpu.roll(ga, k, 0), 1.0)
        gb_sh = jnp.where(rowg >= k, pltpu.roll(gb, k, 0), 0.0)
        gb = ga * gb_sh + gb
        ga = ga * ga_sh
        k *= 2
    leaving = gb + ga * carry
    entering = _shift_rows_in(leaving, carry)
    hs = jnp.concatenate([loc[j] + cum[j] * entering for j in range(SUBLANES)], axis=0)
    return hs, leaving[ng - 1:ng, :]


def _lru_lane_block(n, uan, cw_ref, cb_ref, wg_ref, ba_ref, bx_ref, lam_ref, conv_s, hc_s):
    sl = slice(n * LRU_BW, (n + 1) * LRU_BW)
    xc, tail = _lru_conv(uan[0], cw_ref[:, sl], cb_ref[:, sl], conv_s[0:CONV_W - 1, sl])
    conv_s[0:CONV_W - 1, sl] = tail
    a, bb = _lru_coeffs_block(xc, wg_ref[n], ba_ref[:, sl], bx_ref[:, sl], lam_ref[:, sl])
    hs, last = _lru_scan(a, bb, hc_s[0:1, sl])
    hc_s[0:1, sl] = last
    return (hs * _gelu_tanh(uan[1])).astype(BF16)


def _rows_per_block(x, size, pick):
    parts = []
    for b in range(x.shape[0] // size):
        r = pick(b)
        row = jnp.zeros((1, x.shape[1]), x.dtype) if r is None else x[r:r + 1, :]
        parts.append(jnp.broadcast_to(row, (size, x.shape[1])))
    return jnp.concatenate(parts, axis=0)


def _gla_prep(ulr, uq, uk, wa2_ref, gba):
    c = GLA_CHUNK
    g = _gla_gate(ulr, wa2_ref, gba)
    bc = _group_cumsum(g, c)
    qs = uq * (GLA_DK ** -0.5)
    factors = []
    for size in GLA_SPLITS:
        before = _rows_per_block(bc, size, lambda b: None if (b * size) % c == 0 else b * size - 1)
        last = _rows_per_block(bc, size, lambda b: b * size + size - 1)
        factors.append((qs * jnp.exp(bc - before), uk * jnp.exp(last - bc)))
    mid = _rows_per_block(bc, GLA_FINE, lambda b: b * GLA_FINE + GLA_FINE // 2 - 1)
    factors.append((qs * jnp.exp(bc - mid), uk * jnp.exp(mid - bc)))
    return bc, qs * jnp.exp(bc), factors


def _gla_level_masks():
    c = GLA_CHUNK
    ri = lax.broadcasted_iota(jnp.int32, (c, c), 0)
    ci = lax.broadcasted_iota(jnp.int32, (c, c), 1)
    masks = [(ri // (2 * size) == ci // (2 * size)) & ((ri // size) % 2 == 1) & ((ci // size) % 2 == 0)
             for size in GLA_SPLITS]
    masks.append((ri // GLA_FINE == ci // GLA_FINE) & (ri >= ci))
    spans = [[(b * size, (b + 1) * size) for b in range(1, c // size, 2)] for size in GLA_SPLITS] + [[(0, c)]]
    return masks, spans


def _rows_at(parts, spans, total):
    out, pos, off = [], 0, 0
    for a, b in spans:
        if a > pos:
            out.append(jnp.zeros((a - pos, parts.shape[1]), parts.dtype))
        out.append(parts[off:off + b - a, :])
        off, pos = off + b - a, b
    if pos < total:
        out.append(jnp.zeros((total - pos, parts.shape[1]), parts.dtype))
    return jnp.concatenate(out, axis=0)


def _gla_chunks(bc, qt, factors, uk, uv, s_s):
    tt = bc.shape[0]
    c = GLA_CHUNK
    masks, spans = _gla_level_masks()
    o_rows = []
    for ch in range(tt // c):
        rows = slice(ch * c, (ch + 1) * c)
        bl = bc[ch * c + c - 1:ch * c + c, :]
        kd = uk[rows, :] * jnp.exp(bl - bc[rows, :])
        dec = jnp.exp(bl)
        o_heads = []
        for hd in range(GLA_HEADS):
            ks = slice(hd * GLA_DK, (hd + 1) * GLA_DK)
            qh = qt[rows, ks].astype(BF16)
            vh = uv[rows, hd * GLA_DV:(hd + 1) * GLA_DV].astype(BF16)
            att = jnp.zeros((c, c), F32)
            for (qf, kf), mask, sp in zip(factors, masks, spans):
                qsel = jnp.concatenate([qf[ch * c + a:ch * c + b, ks] for a, b in sp], axis=0)
                att = jnp.where(mask, _rows_at(_dot_nt(qsel, kf[rows, ks]), sp, c), att)
            s_old = s_s[hd]
            o_heads.append(_dot(att, vh) + _dot(qh, s_old))
            dm = _col_bcast(dec[:, ks])
            s_s[hd] = s_old * jnp.concatenate([dm, dm], axis=1) + _dot(kd[:, ks].T, vh)
        o_rows.append(jnp.concatenate(o_heads, axis=1))
    return jnp.concatenate(o_rows, axis=0)


def _prompt_mixer_kernel(x_ref, mod_ref, g1_ref, wt_ref, cw_ref, cb_ref, wg_ref, ba_ref, bx_ref, lam_ref,
                         pa_ref, wa2_ref, gba_ref, gng_ref, pb_ref, wo_ref,
                         x1_ref, conv_ref, lru_ref, st_ref, conv_s, hc_s, s_s, uvrg_s):
    t = pl.program_id(1)

    @pl.when(t == 0)
    def _():
        conv_s[...] = jnp.zeros_like(conv_s)
        hc_s[...] = jnp.zeros_like(hc_s)
        s_s[...] = jnp.zeros_like(s_s)

    x = x_ref[0]
    mod = mod_ref[0]
    hb16 = _norm_mod(x, g1_ref[...], _mod_part(mod, 1), _mod_part(mod, 0)).astype(BF16)
    to_span, _ = _regroup_matrices(REGROUP_SPAN)
    _, from_blocks = _regroup_matrices(x.shape[0])
    u_qk = _dot_nt(hb16, wt_ref[W_Q:W_V, :])
    u_lr = _dot_nt(hb16, wt_ref[W_LR:W_LR + LANES, :])
    hb = _to_blocks(to_span, hb16)
    uk = u_qk[:, GLA_DK_TOTAL:]
    bc, qt, factors = _gla_prep(u_lr, u_qk[:, :GLA_DK_TOTAL], uk, wa2_ref, gba_ref[...])
    piece_rows = ([W_V + i * VRG_PIECE for i in range((W_LR - W_V) // VRG_PIECE)]
                  + [W_GA + i * VRG_PIECE for i in range((N_IN - W_GA) // VRG_PIECE)])
    first_piece = [-(-len(piece_rows) * k // (LRU_BLOCKS // 2)) for k in range(LRU_BLOCKS // 2 + 1)]
    ya_blocks = []
    for m in range(LRU_BLOCKS // 2):
        ux2 = _dot_nt(hb, wt_ref[W_X + m * LRU_PAIR:W_X + (m + 1) * LRU_PAIR, :])
        ug2 = _dot_nt(hb, wt_ref[W_G + m * LRU_PAIR:W_G + (m + 1) * LRU_PAIR, :])
        for i in range(first_piece[m], first_piece[m + 1]):
            uvrg_s[:, i * VRG_PIECE:(i + 1) * VRG_PIECE] = _dot_nt(hb16, wt_ref[piece_rows[i]:piece_rows[i] + VRG_PIECE, :])
        for j in range(2):
            uan = (ux2[:, j * LRU_BW:(j + 1) * LRU_BW], ug2[:, j * LRU_BW:(j + 1) * LRU_BW])
            ya_blocks.append(_lru_lane_block(2 * m + j, uan, cw_ref, cb_ref, wg_ref, ba_ref, bx_ref, lam_ref,
                                             conv_s, hc_s))
    o = _gla_chunks(bc, qt, factors, uk, uvrg_s, s_s)
    ya = jnp.dot(from_blocks, jnp.concatenate(ya_blocks, axis=1), preferred_element_type=F32).astype(BF16)
    ya = jnp.dot(ya, pa_ref[...], preferred_element_type=F32)
    yb = _head_norm_gate(o, gng_ref[...], uvrg_s[:, GLA_DV_TOTAL:GLA_DV_TOTAL + D_MODEL])
    x1_ref[0] = _merge_out(x, _mod_part(mod, 2), uvrg_s[:, 2 * D_MODEL:3 * D_MODEL], uvrg_s[:, 3 * D_MODEL:],
                           ya, yb, pb_ref, wo_ref)

    @pl.when(t == pl.num_programs(1) - 1)
    def _():
        conv_ref[0] = conv_s[0:CONV_W - 1, :]
        lru_ref[0] = hc_s[0:1, :]
        st_ref[0] = s_s[...]


def _prompt_mixer(x, mod_p, w, l):
    b, t, _ = x.shape
    tt = PROMPT_TILE
    per_seq = t // tt
    seq_spec = pl.BlockSpec((1, tt, D_MODEL), lambda i, j: (i, j, 0))
    return pl.pallas_call(
        _prompt_mixer_kernel,
        grid=(b, per_seq),
        in_specs=[
            seq_spec,
            pl.BlockSpec((1, 1, N_MOD), lambda i, j: (i, 0, 0)),
            _wspec(l, (1, D_MODEL)),
            _wspec(l, (N_IN, D_MODEL)),
            _wspec(l, (CONV_W, D_RNN)),
            _wspec(l, (1, D_RNN)),
            _wspec(l, (LRU_BLOCKS, LRU_BW, 2 * LRU_BW)),
            _wspec(l, (1, D_RNN)),
            _wspec(l, (1, D_RNN)),
            _wspec(l, (1, D_RNN)),
            _wspec(l, (D_RNN, D_MODEL)),
            _wspec(l, (LANES, GLA_DK_TOTAL)),
            _wspec(l, (1, GLA_DK_TOTAL)),
            _wspec(l, (1, GLA_DV_TOTAL)),
            _wspec(l, (GLA_DV_TOTAL, D_MODEL)),
            _wspec(l, (D_MODEL, D_MODEL)),
        ],
        out_specs=[
            seq_spec,
            pl.BlockSpec((1, CONV_W - 1, D_RNN), lambda i, j: (i, 0, 0)),
            pl.BlockSpec((1, 1, D_RNN), lambda i, j: (i, 0, 0)),
            pl.BlockSpec((1, GLA_HEADS, GLA_DK, GLA_DV), lambda i, j: (i, 0, 0, 0)),
        ],
        out_shape=[
            jax.ShapeDtypeStruct((b, t, D_MODEL), F32),
            jax.ShapeDtypeStruct((b, CONV_W - 1, D_RNN), F32),
            jax.ShapeDtypeStruct((b, 1, D_RNN), F32),
            jax.ShapeDtypeStruct((b, GLA_HEADS, GLA_DK, GLA_DV), F32),
        ],
        scratch_shapes=[
            pltpu.VMEM((SUBLANES, D_RNN), F32),
            pltpu.VMEM((SUBLANES, D_RNN), F32),
            pltpu.VMEM((GLA_HEADS, GLA_DK, GLA_DV), F32),
            pltpu.VMEM((tt, 2 * GLA_DV_TOTAL + 2 * D_MODEL), F32),
        ],
        compiler_params=_cparams(("arbitrary", "arbitrary")),
        name="prompt_mixer",
    )(x, mod_p, w["g1"], w["win_t"], w["conv_w"], w["conv_b"], w["wgate"], w["lru_ba"], w["lru_bx"],
      w["lru_lambda"], w["proj_a"], w["wa2"], w["gla_ba"], w["gla_norm_g"], w["proj_b"], w["w_out"])


def _ffn_rows(x, shift, scale, gate, g2, w1_ref, w2_ref, fg, final_norm):
    h2 = _norm_mod(x, g2, scale, shift)
    hb = h2.astype(BF16)
    f1_half = _dot(hb, w1_ref[:, :D_FF])
    f2 = _dot(hb, w1_ref[:, D_FF:])
    x2 = x + gate * _dot(_silu_of_half(f1_half) * f2, w2_ref[...])
    if final_norm:
        x2 = x2 * _rms(x2) * fg
    return x2


def _ffn_kernel(x_ref, mod_ref, g2_ref, w1_ref, w2_ref, fg_ref, o_ref, *, final_norm):
    m = mod_ref[0]
    o_ref[...] = _ffn_rows(x_ref[...], _mod_part(m, 3), _mod_part(m, 4), _mod_part(m, 5), g2_ref[...], w1_ref, w2_ref,
                           fg_ref[...], final_norm)


def _ffn(x2d, mod3, rows_per_mod, tile, w, l, final_g, final_norm):
    mrows = x2d.shape[0]
    r = mod3.shape[1]
    per = rows_per_mod // tile
    row_spec = pl.BlockSpec((tile, D_MODEL), lambda i: (i, 0))
    return pl.pallas_call(
        functools.partial(_ffn_kernel, final_norm=final_norm),
        grid=(mrows // tile,),
        in_specs=[
            row_spec,
            pl.BlockSpec((1, r, N_MOD), lambda i: (i // per, 0, 0)),
            _wspec(l, (1, D_MODEL)),
            _wspec(l, (D_MODEL, 2 * D_FF)),
            _wspec(l, (D_FF, D_MODEL)),
            _full_spec((1, D_MODEL)),
        ],
        out_specs=row_spec,
        out_shape=jax.ShapeDtypeStruct((mrows, D_MODEL), F32),
        compiler_params=_cparams(("arbitrary",)),
        name="ffn_final" if final_norm else "ffn",
    )(x2d, mod3, w["g2"], w["ffn_w1"], w["ffn_w2"], final_g)


def _row_regroup(rows, inner, outer):
    ri = lax.broadcasted_iota(jnp.int32, (rows, rows), 0)
    ci = lax.broadcasted_iota(jnp.int32, (rows, rows), 1)
    return jnp.where(ci == (ri % inner) * outer + ri // inner, 1.0, 0.0).astype(BF16)


def _sample_in_kernel(x_ref, mod_ref, g1_ref, win_ref, cw_ref, cb_ref, wg_ref, ba_ref, bx_ref, lam_ref, pa_ref,
                      conv0_ref, h0_ref, ya_ref, conv_ref, lru_ref, u_ref, ug_ref):
    rows = x_ref.shape[0]
    ns = h0_ref.shape[0]
    nt = rows // ns
    m = mod_ref[...]
    tile = lambda v: jnp.concatenate([v] * nt, axis=0)
    h = _norm_mod(x_ref[...], g1_ref[...], tile(_mod_part(m, 1)), tile(_mod_part(m, 0))).astype(BF16)
    ux = _dot_nt(h, win_ref[W_X:W_G, :])
    ug = _dot_nt(h, win_ref[W_G:W_Q, :])
    xa = [conv0_ref[i] for i in range(CONV_W - 1)] + [ux[i * ns:(i + 1) * ns, :] for i in range(nt)]
    cw = cw_ref[...]
    xcs = []
    for ti in range(nt):
        acc = cb_ref[...]
        for i in range(CONV_W):
            acc = acc + xa[ti + i] * cw[i:i + 1, :]
        xcs.append(acc)
    xc = jnp.concatenate(xcs, axis=0)
    coeffs = [_lru_coeffs_block(xc[:, n * LRU_BW:(n + 1) * LRU_BW], wg_ref[n], ba_ref[:, n * LRU_BW:(n + 1) * LRU_BW],
                                bx_ref[:, n * LRU_BW:(n + 1) * LRU_BW], lam_ref[:, n * LRU_BW:(n + 1) * LRU_BW])
              for n in range(LRU_BLOCKS)]
    a = jnp.concatenate([c[0] for c in coeffs], axis=1)
    bb = jnp.concatenate([c[1] for c in coeffs], axis=1)
    hc = h0_ref[...]
    hs = []
    for ti in range(nt):
        sl = slice(ti * ns, (ti + 1) * ns)
        hc = a[sl, :] * hc + bb[sl, :]
        hs.append(hc)
    ya_ref[...] = _dot(jnp.concatenate(hs, axis=0) * _gelu_tanh(ug), pa_ref[...])
    for i in range(CONV_W - 1):
        conv_ref[i] = xa[nt + i]
    lru_ref[...] = hc
    ug_ref[...] = _dot_nt(h, win_ref[W_GA:N_IN, :])
    hs_major = jnp.dot(_row_regroup(rows, nt, ns), h, preferred_element_type=F32).astype(BF16)
    u_ref[...] = jnp.concatenate([_dot_nt(hs_major, win_ref[W_LR:W_LR + LANES, :]),
                                  _dot_nt(hs_major, win_ref[W_Q:W_LR, :])], axis=1)


def _sample_in(x_tm, mod_s, conv0_tm, h0_all, w, l):
    rows = x_tm.shape[0]
    ns = h0_all.shape[1]
    return pl.pallas_call(
        _sample_in_kernel,
        grid=(1,),
        in_specs=[
            _full_spec((rows, D_MODEL)),
            _full_spec((ns, N_MOD)),
            _wspec(l, (1, D_MODEL)),
            _wspec(l, (N_IN, D_MODEL)),
            _wspec(l, (CONV_W, D_RNN)),
            _wspec(l, (1, D_RNN)),
            _wspec(l, (LRU_BLOCKS, LRU_BW, 2 * LRU_BW)),
            _wspec(l, (1, D_RNN)),
            _wspec(l, (1, D_RNN)),
            _wspec(l, (1, D_RNN)),
            _wspec(l, (D_RNN, D_MODEL)),
            _full_spec((CONV_W - 1, ns, D_RNN)),
            _wspec(l, (ns, D_RNN)),
        ],
        out_specs=[
            _full_out((rows, D_MODEL)),
            _full_out((CONV_W - 1, ns, D_RNN)),
            _full_out((ns, D_RNN)),
            _full_out((rows, N_IN_B)),
            _full_out((rows, 2 * D_MODEL)),
        ],
        out_shape=[
            jax.ShapeDtypeStruct((rows, D_MODEL), F32),
            jax.ShapeDtypeStruct((CONV_W - 1, ns, D_RNN), F32),
            jax.ShapeDtypeStruct((ns, D_RNN), F32),
            jax.ShapeDtypeStruct((rows, N_IN_B), F32),
            jax.ShapeDtypeStruct((rows, 2 * D_MODEL), F32),
        ],
        compiler_params=_cparams(("arbitrary",)),
        name="sample_in",
    )(x_tm, mod_s, w["g1"], w["win_t"], w["conv_w"], w["conv_b"], w["wgate"], w["lru_ba"], w["lru_bx"],
      w["lru_lambda"], w["proj_a"], conv0_tm, h0_all)


def _sample_gla_kernel(u_ref, wa2_ref, gba_ref, gng_ref, s0_ref, *rest, nt, layer, carries_buffer):
    if carries_buffer:
        yb_ref, st_ref, o_s, q_s, v_s, kdt_s, bl_s = rest[1:]
    else:
        yb_ref, st_all_ref, o_s, q_s, v_s, kdt_s, bl_s = rest
        st_ref = st_all_ref.at[layer]
        for other in range(st_all_ref.shape[0]):
            if other != layer:
                st_all_ref[other] = jnp.zeros(st_all_ref.shape[1:], F32)
    rows = u_ref.shape[0]
    steps_per_rows = rows // (SAMPLE_STATE_BLOCK * nt)
    sub = pl.program_id(0) % steps_per_rows
    rg = SAMPLE_ROW_GROUP
    seq_per_group = rg // nt

    @pl.when(sub == 0)
    def _():
        u = u_ref[...]
        g = _gla_gate(u[:, _LRO:_LRO + LANES], wa2_ref, gba_ref[...])
        bc = _group_cumsum(g, nt)
        rin = lax.broadcasted_iota(jnp.int32, g.shape, 0) % nt
        sfx = jnp.zeros_like(g)
        for k in range(1, nt):
            sfx = sfx + jnp.where(rin + k < nt, pltpu.roll(g, rows - k, 0), 0.0)
        qt = u[:, _QO:_QO + GLA_DK_TOTAL] * (jnp.exp(bc) * (GLA_DK ** -0.5))
        uk = u[:, _KO:_KO + GLA_DK_TOTAL]
        kt = uk * jnp.exp(-bc)
        kd = uk * jnp.exp(sfx)
        q_s[...] = qt.astype(BF16)
        v_s[...] = u[:, _VO:_VO + GLA_DV_TOTAL].astype(BF16)
        bl_s[...] = bc + sfx
        ri = lax.broadcasted_iota(jnp.int32, (rows, rows), 0)
        ci = lax.broadcasted_iota(jnp.int32, (rows, rows), 1)
        same_seq_causal = (ri // nt == ci // nt) & (ri >= ci)
        for hd in range(GLA_HEADS):
            ks = slice(hd * GLA_DK, (hd + 1) * GLA_DK)
            att = jnp.where(same_seq_causal, _dot_nt(qt[:, ks], kt[:, ks]), 0.0)
            o_s[:, hd * GLA_DV:(hd + 1) * GLA_DV] = _dot(att, v_s[:, hd * GLA_DV:(hd + 1) * GLA_DV])
            kdt_s[hd] = kd[:, ks].T

    lane_seq = lax.broadcasted_iota(jnp.int32, (GLA_DK, rows), 1) // nt
    row_seq = lax.broadcasted_iota(jnp.int32, (rg, GLA_DV), 0) // nt

    def group(gi, carry):
        r0 = pl.multiple_of((sub * (SAMPLE_STATE_BLOCK // seq_per_group) + gi) * rg, rg)
        dec_g = jnp.exp(bl_s[pl.ds(r0, rg), :])
        for hd in range(GLA_HEADS):
            ks = slice(hd * GLA_DK, (hd + 1) * GLA_DK)
            vsl = slice(hd * GLA_DV, (hd + 1) * GLA_DV)
            qg = q_s[pl.ds(r0, rg), ks]
            vh = v_s[:, vsl]
            inter = jnp.zeros((rg, GLA_DV), F32)
            for j in range(seq_per_group):
                s_loc = gi * seq_per_group + j
                s_row = sub * SAMPLE_STATE_BLOCK + s_loc
                s_old = s0_ref[s_loc, hd]
                inter = jnp.where(row_seq == j, _dot(qg, s_old), inter)
                kdt = jnp.where(lane_seq == s_row, kdt_s[hd], 0.0)
                dm = _col_bcast(dec_g[j * nt:j * nt + 1, ks])
                st_ref[s_loc, hd] = s_old * jnp.concatenate([dm, dm], axis=1) + _dot(kdt, vh)
            o_s[pl.ds(r0, rg), vsl] = o_s[pl.ds(r0, rg), vsl] + inter
        return carry

    lax.fori_loop(0, SAMPLE_STATE_BLOCK // seq_per_group, group, 0)

    @pl.when(sub == steps_per_rows - 1)
    def _():
        yb_ref[...] = _head_norm_gate(o_s[...], gng_ref[...], u_ref[:, _RO:_RO + D_MODEL])


def _sample_gla(u_sm, state_all, new_state_all, w, l, nt):
    rows = u_sm.shape[0]
    br = SAMPLE_ROW_BLOCK
    sb = SAMPLE_STATE_BLOCK
    per = br // (sb * nt)
    state_spec = pl.BlockSpec((None, sb, GLA_HEADS, GLA_DK, GLA_DV), lambda i: (l, i, 0, 0, 0))
    out_state_spec = state_spec
    if new_state_all is None:
        out_state_spec = pl.BlockSpec((state_all.shape[0], sb, GLA_HEADS, GLA_DK, GLA_DV), lambda i: (0, i, 0, 0, 0))
    in_specs = [
        pl.BlockSpec((br, N_IN_B), lambda i: (i // per, 0)),
        _wspec(l, (LANES, GLA_DK_TOTAL)),
        _wspec(l, (1, GLA_DK_TOTAL)),
        _wspec(l, (1, GLA_DV_TOTAL)),
        state_spec,
    ]
    args = [u_sm, w["wa2"], w["gla_ba"], w["gla_norm_g"], state_all]
    aliases = {}
    if new_state_all is not None:
        in_specs.append(pl.BlockSpec(memory_space=pl.ANY))
        args.append(new_state_all)
        aliases = {len(args) - 1: 1}
    return pl.pallas_call(
        functools.partial(_sample_gla_kernel, nt=nt, layer=l, carries_buffer=new_state_all is not None),
        grid=(rows // (sb * nt),),
        in_specs=in_specs,
        out_specs=[
            pl.BlockSpec((br, D_MODEL), lambda i: (i // per, 0)),
            out_state_spec,
        ],
        out_shape=[
            jax.ShapeDtypeStruct((rows, D_MODEL), F32),
            jax.ShapeDtypeStruct(state_all.shape, F32),
        ],
        input_output_aliases=aliases,
        scratch_shapes=[
            pltpu.VMEM((br, GLA_DV_TOTAL), F32),
            pltpu.VMEM((br, GLA_DK_TOTAL), BF16),
            pltpu.VMEM((br, GLA_DV_TOTAL), BF16),
            pltpu.VMEM((GLA_HEADS, GLA_DK, br), F32),
            pltpu.VMEM((br, GLA_DK_TOTAL), F32),
        ],
        compiler_params=_cparams(("arbitrary",)),
        name="sample_gla",
    )(*args)


def _sample_out_kernel(x_ref, mod_ref, ug_ref, ya_ref, yb_ref, pb_ref, wo_ref, g2_ref, w1_ref, w2_ref, fg_ref, o_ref, *,
                       final_norm):
    rows = x_ref.shape[0]
    ns = mod_ref.shape[0]
    nt = rows // ns
    m = mod_ref[...]
    tile = lambda v: jnp.concatenate([v] * nt, axis=0)
    yb = jnp.dot(_row_regroup(rows, ns, nt), yb_ref[...].astype(BF16), preferred_element_type=F32).astype(BF16)
    x1 = _merge_out(x_ref[...], tile(_mod_part(m, 2)), ug_ref[:, :D_MODEL], ug_ref[:, D_MODEL:], ya_ref[...], yb,
                    pb_ref, wo_ref)
    o_ref[...] = _ffn_rows(x1, tile(_mod_part(m, 3)), tile(_mod_part(m, 4)), tile(_mod_part(m, 5)), g2_ref[...],
                           w1_ref, w2_ref, fg_ref[...], final_norm)


def _sample_out(x_tm, mod_s, ug_tm, ya, yb_sm, w, l, final_g, final_norm):
    rows = x_tm.shape[0]
    return pl.pallas_call(
        functools.partial(_sample_out_kernel, final_norm=final_norm),
        grid=(1,),
        in_specs=[
            _full_spec((rows, D_MODEL)),
            _full_spec(mod_s.shape),
            _full_spec((rows, 2 * D_MODEL)),
            _full_spec((rows, D_MODEL)),
            _full_spec((rows, D_MODEL)),
            _wspec(l, (GLA_DV_TOTAL, D_MODEL)),
            _wspec(l, (D_MODEL, D_MODEL)),
            _wspec(l, (1, D_MODEL)),
            _wspec(l, (D_MODEL, 2 * D_FF)),
            _wspec(l, (D_FF, D_MODEL)),
            _full_spec((1, D_MODEL)),
        ],
        out_specs=_full_out((rows, D_MODEL)),
        out_shape=jax.ShapeDtypeStruct((rows, D_MODEL), F32),
        compiler_params=_cparams(("arbitrary",)),
        name="sample_out_final" if final_norm else "sample_out",
    )(x_tm, mod_s, ug_tm, ya, yb_sm, w["proj_b"], w["w_out"], w["g2"], w["ffn_w1"], w["ffn_w2"], final_g)


def _stacked_weights(norm1_g, norm2_g, w_in, conv_w, conv_b, lru_wa, lru_ba, lru_wx, lru_bx, lru_lambda,
                     gla_wa2, gla_ba, gla_norm_g, proj_a, proj_b, w_out, ffn_w1, ffn_w2):
    depth = w_in.shape[0]
    row = lambda v: v.reshape(depth, 1, -1)
    rows_in = jnp.arange(N_IN)
    in_scale = jnp.where(((rows_in >= W_R) & (rows_in < W_LR)) | (rows_in >= W_GA), 0.5, 1.0).astype(w_in.dtype)
    ff_scale = jnp.where(jnp.arange(2 * D_FF) < D_FF, 0.5, 1.0).astype(ffn_w1.dtype)
    return {
        "g1": row(norm1_g), "g2": row(norm2_g),
        "win_t": (jnp.swapaxes(w_in, 1, 2) * in_scale[None, :, None]).astype(BF16),
        "conv_w": conv_w, "conv_b": row(conv_b),
        "wgate": (0.5 * jnp.concatenate([lru_wa, lru_wx], axis=-1)).astype(BF16),
        "lru_ba": row(0.5 * lru_ba), "lru_bx": row(0.5 * lru_bx), "lru_lambda": row(lru_lambda),
        "wa2": jnp.pad(gla_wa2, ((0, 0), (0, LANES - GLA_RANK), (0, 0))).astype(BF16),
        "gla_ba": row(gla_ba), "gla_norm_g": row(gla_norm_g),
        "proj_a": proj_a.astype(BF16), "proj_b": proj_b.astype(BF16), "w_out": w_out.astype(BF16),
        "ffn_w1": (ffn_w1 * ff_scale).astype(BF16), "ffn_w2": ffn_w2.astype(BF16),
    }


def kernel(x_prompt, x_sample, c_prompt, c_sample, state_conv, state_lru, state_gla, norm1_g, norm2_g, ada_w, ada_b,
           w_in, conv_w, conv_b, lru_wa, lru_ba, lru_wx, lru_bx, lru_lambda, gla_wa2, gla_ba, gla_norm_g, proj_a,
           proj_b, w_out, ffn_w1, ffn_w2, final_g):
    bp, tp, _ = x_prompt.shape
    bs, ts, _ = x_sample.shape
    depth = w_in.shape[0]
    mod = _modulation(jnp.concatenate([c_prompt, c_sample], axis=0), ada_w, ada_b)
    w = _stacked_weights(norm1_g, norm2_g, w_in, conv_w, conv_b, lru_wa, lru_ba, lru_wx, lru_bx, lru_lambda,
                         gla_wa2, gla_ba, gla_norm_g, proj_a, proj_b, w_out, ffn_w1, ffn_w2)
    fg = final_g.reshape(1, D_MODEL)
    xp = x_prompt
    xs = x_sample.transpose(1, 0, 2).reshape(ts * bs, D_MODEL)
    conv_p, lru_p, gla_p, conv_s, lru_s = [], [], [], [], []
    gla_s = None
    for l in range(depth):
        last = l == depth - 1
        mod_p = mod[l, :bp].reshape(bp, 1, N_MOD)
        mod_s = mod[l, bp:]
        x1, cb, ht, st = _prompt_mixer(xp, mod_p, w, l)
        xp = _ffn(x1.reshape(bp * tp, D_MODEL), mod_p, tp, FFN_TILE, w, l, fg, last).reshape(bp, tp, D_MODEL)
        conv_p.append(cb)
        lru_p.append(ht.reshape(bp, D_RNN))
        gla_p.append(st)
        ya_s, cb_s, ht_s, u_sm, ug_tm = _sample_in(xs, mod_s, state_conv[l].transpose(1, 0, 2), state_lru, w, l)
        yb_sm, gla_s = _sample_gla(u_sm, state_gla, gla_s, w, l, ts)
        xs = _sample_out(xs, mod_s, ug_tm, ya_s, yb_sm, w, l, fg, last)
        conv_s.append(cb_s.transpose(1, 0, 2))
        lru_s.append(ht_s)
    y_sample = xs.reshape(ts, bs, D_MODEL).transpose(1, 0, 2)
    return (xp, y_sample, jnp.stack(conv_p), jnp.stack(lru_p), jnp.stack(gla_p),
            jnp.stack(conv_s), jnp.stack(lru_s), gla_s)
```
